```python
import numpy as np
import jax
import jax.numpy as jnp
from jax import lax

D_MODEL = 1024
BATCH = 8
SEQ = 2048
DEPTH = 1
DEC_BATCH = 128
DEC_SEQ = 1
PAST_LEN = 8192
PAGE_SIZE = 128

HEAD_DIM = 64
MIX_WIDTH = D_MODEL
MOBA_HEADS = MIX_WIDTH // 2 // HEAD_DIM
MOBA_KV_HEADS = MOBA_HEADS // 2
MOBA_GROUP = MOBA_HEADS // MOBA_KV_HEADS
NSA_HEADS = MIX_WIDTH // 2 // HEAD_DIM
NSA_KV_HEADS = NSA_HEADS // 4
NSA_GROUP = NSA_HEADS // NSA_KV_HEADS
ROPE_DIM = HEAD_DIM // 4
ROPE_THETA = 500000.0
MOBA_BLOCK = 256
MOBA_TOPK = 3
CMP_LEN = 32
CMP_STRIDE = 16
CMP_RATIO = CMP_LEN // CMP_STRIDE
CMP_HIDDEN = 2 * HEAD_DIM
SEL_LEN = 64
SEL_TOPN = 16
WINDOW = 512
D_FF = 2816
MACARON_WEIGHT = 0.5
N_MOD = 9
Q_BLOCK = 32
EPS = 1e-6
SCALE = HEAD_DIM ** -0.5
D_MOBA_Q = MOBA_HEADS * HEAD_DIM
D_MOBA_KV = MOBA_KV_HEADS * HEAD_DIM
D_NSA_Q = NSA_HEADS * HEAD_DIM
D_NSA_KV = NSA_KV_HEADS * HEAD_DIM
PROJ_SPLITS = (D_MOBA_Q, D_MOBA_KV, D_MOBA_KV, D_NSA_Q) + (D_NSA_KV,) * 6 + (3 * NSA_HEADS,)
D_IN = sum(PROJ_SPLITS)

kernel_name = 'hybrid_moba_nsa_macaron_adaln_step'


def rms_norm(x, g):
    xf = x.astype(jnp.float32)
    y = xf * lax.rsqrt(jnp.mean(xf * xf, axis=-1, keepdims=True) + EPS)
    return (y * g.astype(jnp.float32)).astype(x.dtype)


def modulate(x, g, shift, scale):
    return rms_norm(x, g) * (1.0 + scale) + shift


def swiglu(h, w_in, w_out):
    a, b = jnp.split(h @ w_in, 2, axis=-1)
    return (jax.nn.silu(a) * b) @ w_out


def rope_partial(x, pos):
    half = ROPE_DIM // 2
    inv_freq = ROPE_THETA ** (-jnp.arange(half, dtype=jnp.float32) / half)
    ang = pos.astype(jnp.float32)[..., None, None] * inv_freq
    cos, sin = jnp.cos(ang), jnp.sin(ang)
    xf = x.astype(jnp.float32)
    x1, x2, rest = xf[..., :half], xf[..., half:ROPE_DIM], xf[..., ROPE_DIM:]
    out = jnp.concatenate([x1 * cos - x2 * sin, x1 * sin + x2 * cos, rest], axis=-1)
    return out.astype(x.dtype)


def masked_softmax(s, mask):
    s = jnp.where(mask, s, -jnp.inf)
    m = jnp.max(s, axis=-1, keepdims=True)
    m = jnp.where(jnp.isfinite(m), m, 0.0)
    e = jnp.where(mask, jnp.exp(s - m), 0.0)
    return e / jnp.maximum(jnp.sum(e, axis=-1, keepdims=True), 1e-30)


def pad_rows(x, total):
    return jnp.pad(x, ((0, 0), (0, total - x.shape[1]), (0, 0), (0, 0)))


def to_blocks(x, blk):
    B, T, H, D = x.shape
    return x.reshape(B, T // blk, blk, H, D).transpose(0, 3, 1, 2, 4)


def gather_rows(pool, new_rows, page_table, pos, head_idx):
    n_past = page_table.shape[1] * PAGE_SIZE
    b = jnp.arange(pos.shape[0]).reshape((-1,) + (1,) * (pos.ndim - 1))
    p_old = jnp.clip(pos, 0, n_past - 1)
    phys = page_table[b, p_old // PAGE_SIZE]
    old = pool[phys, p_old % PAGE_SIZE, head_idx]
    new = new_rows[b, jnp.clip(pos - n_past, 0, new_rows.shape[1] - 1), head_idx]
    return jnp.where((pos < n_past)[..., None], old, new)


def moba_block_means(k):
    B, T = k.shape[:2]
    nb = -(-T // MOBA_BLOCK)
    kb = pad_rows(k, nb * MOBA_BLOCK).reshape(B, nb, MOBA_BLOCK, MOBA_KV_HEADS, HEAD_DIM)
    return jnp.mean(kb.astype(jnp.float32), axis=2).astype(k.dtype)


def moba_select(q, kmean, qpos):
    B, Q = q.shape[:2]
    nb = kmean.shape[1]
    qg = q.reshape(B, Q, MOBA_KV_HEADS, MOBA_GROUP, HEAD_DIM)
    s = jnp.einsum('bqkgd,bnkd->bqkgn', qg, kmean, preferred_element_type=jnp.float32).reshape(B, Q, MOBA_HEADS, nb)
    own = (qpos // MOBA_BLOCK)[:, :, None, None]
    s = jnp.where(jnp.arange(nb) < own, s, -jnp.inf)
    _, idx = lax.top_k(s, min(MOBA_TOPK, nb))
    return idx, idx < own


def moba_attend(q, k_rows, v_rows, mask):
    s = jnp.einsum('bqhd,bqhld->bqhl', q, k_rows, preferred_element_type=jnp.float32) * SCALE
    p = masked_softmax(s, mask).astype(v_rows.dtype)
    return jnp.einsum('bqhl,bqhld->bqhd', p, v_rows)


def compress(rows, pe, w1, w2):
    B, T = rows.shape[:2]
    n_chunks = T // CMP_STRIDE
    nc = n_chunks - CMP_RATIO + 1
    chunks = rows[:, :n_chunks * CMP_STRIDE].reshape(B, n_chunks, CMP_STRIDE, NSA_KV_HEADS, HEAD_DIM)
    pe_r = pe.reshape(CMP_RATIO, CMP_STRIDE, HEAD_DIM)
    w1_r = w1.reshape(CMP_RATIO, CMP_STRIDE, HEAD_DIM, CMP_HIDDEN)
    h = jnp.einsum('bnlkd,lde->bnke', chunks[:, 0:nc] + pe_r[0][:, None, :], w1_r[0])
    for r in range(1, CMP_RATIO):
        h = h + jnp.einsum('bnlkd,lde->bnke', chunks[:, r:r + nc] + pe_r[r][:, None, :], w1_r[r])
    return jnp.einsum('bnke,ed->bnkd', jax.nn.silu(h), w2)


def cmp_to_sel_matrix(nc, nsb):
    i = np.arange(nc)[:, None]
    j = np.arange(nsb)[None, :]
    start = i * CMP_STRIDE
    m = (start < (j + 1) * SEL_LEN) & (start + CMP_LEN > j * SEL_LEN)
    return jnp.asarray(m.astype(np.float32))


def nsa_compressed(q_nope, ck_c, cv_c, qpos):
    B, Q = q_nope.shape[:2]
    nc = ck_c.shape[1]
    qg = q_nope.reshape(B, Q, NSA_KV_HEADS, NSA_GROUP, HEAD_DIM)
    s = jnp.einsum('bqkgd,bnkd->bqkgn', qg, ck_c, preferred_element_type=jnp.float32) * SCALE
    cmp_end = jnp.arange(nc) * CMP_STRIDE + CMP_LEN - 1
    mask = (cmp_end[None, None, :] <= qpos[:, :, None])[:, :, None, None, :]
    p = masked_softmax(s, mask)
    o = jnp.einsum('bqkgn,bnkd->bqkgd', p.astype(cv_c.dtype), cv_c).reshape(B, Q, NSA_HEADS, HEAD_DIM)
    return o, jnp.sum(p, axis=3)


def nsa_select(p_kv, qpos, nsb):
    score = jnp.einsum('bqkn,nj->bqkj', p_kv, cmp_to_sel_matrix(p_kv.shape[-1], nsb))
    j = jnp.arange(nsb)
    t = qpos[:, :, None, None]
    cur = t // SEL_LEN
    valid = j * SEL_LEN <= t
    forced = (j == 0) | (j == cur) | (j == cur - 1)
    score = jnp.where(valid, jnp.where(forced, jnp.inf, score), -jnp.inf)
    _, idx = lax.top_k(score, min(SEL_TOPN, nsb))
    return idx, idx * SEL_LEN <= t


def nsa_selected_attend(q, k_rows, v_rows, mask):
    B, Q = q.shape[:2]
    qg = q.reshape(B, Q, NSA_KV_HEADS, NSA_GROUP, HEAD_DIM)
    s = jnp.einsum('bqkgd,bqkld->bqkgl', qg, k_rows, preferred_element_type=jnp.float32) * SCALE
    p = masked_softmax(s, mask[:, :, :, None, :]).astype(v_rows.dtype)
    return jnp.einsum('bqkgl,bqkld->bqkgd', p, v_rows).reshape(B, Q, NSA_HEADS, HEAD_DIM)


def nsa_window_attend(q, k, v, mask):
    B, Q = q.shape[:2]
    qg = q.reshape(B, Q, NSA_KV_HEADS, NSA_GROUP, HEAD_DIM)
    s = jnp.einsum('bqkgd,blkd->bqkgl', qg, k, preferred_element_type=jnp.float32) * SCALE
    p = masked_softmax(s, mask[:, :, None, None, :]).astype(v.dtype)
    return jnp.einsum('bqkgl,blkd->bqkgd', p, v).reshape(B, Q, NSA_HEADS, HEAD_DIM)


def nsa_combine(g, o_c, o_s, o_w):
    return (g[..., 0:1] * o_c + g[..., 1:2] * o_s + g[..., 2:3] * o_w).astype(o_c.dtype)


def split_heads(proj, pos):
    B, T = proj.shape[:2]
    offs = [int(o) for o in np.cumsum(PROJ_SPLITS)[:-1]]
    p = jnp.split(proj, offs, axis=-1)

    def heads(a, n):
        return a.reshape(B, T, n, HEAD_DIM)

    mq = rope_partial(heads(p[0], MOBA_HEADS), pos)
    mk = rope_partial(heads(p[1], MOBA_KV_HEADS), pos)
    mv = heads(p[2], MOBA_KV_HEADS)
    nq_nope = heads(p[3], NSA_HEADS)
    nq = rope_partial(nq_nope, pos)
    ck = heads(p[4], NSA_KV_HEADS)
    cv = heads(p[5], NSA_KV_HEADS)
    sk = rope_partial(heads(p[6], NSA_KV_HEADS), pos)
    sv = heads(p[7], NSA_KV_HEADS)
    wk = rope_partial(heads(p[8], NSA_KV_HEADS), pos)
    wv = heads(p[9], NSA_KV_HEADS)
    gates = jax.nn.sigmoid(p[10].astype(jnp.float32)).reshape(B, T, NSA_HEADS, 3)
    return mq, mk, mv, nq_nope, nq, ck, cv, sk, sv, wk, wv, gates


def mixer_prompt(proj, cmp_k_pe, cmp_k_w1, cmp_k_w2, cmp_v_pe, cmp_v_w1, cmp_v_w2):
    B, S = proj.shape[:2]
    pos = jnp.broadcast_to(jnp.arange(S, dtype=jnp.int32), (B, S))
    mq, mk, mv, nq_nope, nq, ck, cv, sk, sv, wk, wv, gates = split_heads(proj, pos)
    kmean = moba_block_means(mk)
    nb = kmean.shape[1]
    mk_pad, mv_pad = pad_rows(mk, nb * MOBA_BLOCK), pad_rows(mv, nb * MOBA_BLOCK)
    mk_blk, mv_blk = to_blocks(mk_pad, MOBA_BLOCK), to_blocks(mv_pad, MOBA_BLOCK)
    ck_c = compress(ck, cmp_k_pe, cmp_k_w1, cmp_k_w2)
    cv_c = compress(cv, cmp_v_pe, cmp_v_w1, cmp_v_w2)
    nsb = -(-S // SEL_LEN)
    sk_blk = to_blocks(pad_rows(sk, nsb * SEL_LEN), SEL_LEN)
    sv_blk = to_blocks(pad_rows(sv, nsb * SEL_LEN), SEL_LEN)
    band = ((0, 0), (WINDOW, 0), (0, 0), (0, 0))
    wk_pad, wv_pad = jnp.pad(wk, band), jnp.pad(wv, band)
    b_idx = jnp.arange(B)[:, None, None, None]
    moba_kv = (jnp.arange(MOBA_HEADS) // MOBA_GROUP)[None, None, :, None]
    nsa_kv = jnp.arange(NSA_KV_HEADS)[None, None, :, None]
    blk_off = jnp.arange(MOBA_BLOCK)
    sel_off = jnp.arange(SEL_LEN)
    band_off = jnp.arange(WINDOW + Q_BLOCK)

    def query_block(i):
        qs = i * Q_BLOCK
        qpos = jnp.broadcast_to(qs + jnp.arange(Q_BLOCK, dtype=jnp.int32), (B, Q_BLOCK))

        def take(a):
            return lax.dynamic_slice_in_dim(a, qs, Q_BLOCK, axis=1)

        q_m = take(mq)
        idx, valid = moba_select(q_m, kmean, qpos)
        n_sel = idx.shape[-1] * MOBA_BLOCK
        k_sel = mk_blk[b_idx, moba_kv, idx].reshape(B, Q_BLOCK, MOBA_HEADS, n_sel, HEAD_DIM)
        v_sel = mv_blk[b_idx, moba_kv, idx].reshape(B, Q_BLOCK, MOBA_HEADS, n_sel, HEAD_DIM)
        own_start = (qs // MOBA_BLOCK) * MOBA_BLOCK
        k_own = lax.dynamic_slice_in_dim(mk_pad, own_start, MOBA_BLOCK, axis=1)
        v_own = lax.dynamic_slice_in_dim(mv_pad, own_start, MOBA_BLOCK, axis=1)
        mask_own = jnp.broadcast_to((own_start + blk_off)[None, None, None, :] <= qpos[:, :, None, None],
                                    (B, Q_BLOCK, MOBA_HEADS, MOBA_BLOCK))
        qg = q_m.reshape(B, Q_BLOCK, MOBA_KV_HEADS, MOBA_GROUP, HEAD_DIM)
        s_sel = jnp.einsum('bqhd,bqhld->bqhl', q_m, k_sel, preferred_element_type=jnp.float32)
        s_own = jnp.einsum('bqkgd,bskd->bqkgs', qg, k_own,
                           preferred_element_type=jnp.float32).reshape(B, Q_BLOCK, MOBA_HEADS, MOBA_BLOCK)
        p = masked_softmax(jnp.concatenate([s_sel, s_own], axis=-1) * SCALE,
                           jnp.concatenate([jnp.repeat(valid, MOBA_BLOCK, axis=-1), mask_own], axis=-1)).astype(mv.dtype)
        p_own = p[..., n_sel:].reshape(B, Q_BLOCK, MOBA_KV_HEADS, MOBA_GROUP, MOBA_BLOCK)
        o_m = (jnp.einsum('bqhl,bqhld->bqhd', p[..., :n_sel], v_sel)
               + jnp.einsum('bqkgs,bskd->bqkgd', p_own, v_own).reshape(B, Q_BLOCK, MOBA_HEADS, HEAD_DIM))
        qn, qr, g = take(nq_nope), take(nq), take(gates)
        o_c, p_kv = nsa_compressed(qn, ck_c, cv_c, qpos)
        idx_s, valid_s = nsa_select(p_kv, qpos, nsb)
        n_rows = idx_s.shape[-1] * SEL_LEN
        k_rows = sk_blk[b_idx, nsa_kv, idx_s].reshape(B, Q_BLOCK, NSA_KV_HEADS, n_rows, HEAD_DIM)
        v_rows = sv_blk[b_idx, nsa_kv, idx_s].reshape(B, Q_BLOCK, NSA_KV_HEADS, n_rows, HEAD_DIM)
        row_pos = idx_s[..., None] * SEL_LEN + sel_off
        mask_s = (valid_s[..., None] & (row_pos <= qpos[:, :, None, None, None])).reshape(B, Q_BLOCK, NSA_KV_HEADS, n_rows)
        o_s = nsa_selected_attend(qr, k_rows, v_rows, mask_s)
        k_band = lax.dynamic_slice_in_dim(wk_pad, qs, WINDOW + Q_BLOCK, axis=1)
        v_band = lax.dynamic_slice_in_dim(wv_pad, qs, WINDOW + Q_BLOCK, axis=1)
        band_pos = qs - WINDOW + band_off
        dist = qpos[:, :, None] - band_pos[None, None, :]
        mask_w = (band_pos >= 0)[None, None, :] & (dist >= 0) & (dist < WINDOW)
        o_w = nsa_window_attend(qr, k_band, v_band, mask_w)
        o_n = nsa_combine(g, o_c, o_s, o_w)
        return jnp.concatenate([o_m.reshape(B, Q_BLOCK, D_MOBA_Q), o_n.reshape(B, Q_BLOCK, D_NSA_Q)], axis=-1)

    out = lax.map(query_block, jnp.arange(S // Q_BLOCK))
    out = jnp.moveaxis(out, 0, 1).reshape(B, S, MIX_WIDTH)
    w_keep = min(WINDOW, S)
    return out, (mk, mv, ck, cv, sk, sv, wk[:, S - w_keep:], wv[:, S - w_keep:])


def mixer_sample(proj, page_table, cache_mk, cache_mv, cache_ck, cache_cv, cache_sk, cache_sv,
                 state_wk, state_wv, cmp_k_pe, cmp_k_w1, cmp_k_w2, cmp_v_pe, cmp_v_w1, cmp_v_w2):
    DB, DS = proj.shape[:2]
    n_past = page_table.shape[1] * PAGE_SIZE
    T = n_past + DS
    pos = jnp.broadcast_to(n_past + jnp.arange(DS, dtype=jnp.int32), (DB, DS))
    mq, mk, mv, nq_nope, nq, ck, cv, sk, sv, wk, wv, gates = split_heads(proj, pos)
    t4 = pos[:, :, None, None]

    def all_rows(pool, new):
        past = pool[page_table].reshape(DB, n_past, pool.shape[2], HEAD_DIM)
        return jnp.concatenate([past, new], axis=1)

    kmean = moba_block_means(all_rows(cache_mk, mk))
    idx, valid = moba_select(mq, kmean, pos)
    blk_off = jnp.arange(MOBA_BLOCK)
    n_sel = idx.shape[-1] * MOBA_BLOCK
    sel_pos = (idx[..., None] * MOBA_BLOCK + blk_off).reshape(DB, DS, MOBA_HEADS, n_sel)
    own_pos = jnp.broadcast_to((t4 // MOBA_BLOCK) * MOBA_BLOCK + blk_off, (DB, DS, MOBA_HEADS, MOBA_BLOCK))
    rows_pos = jnp.concatenate([sel_pos, own_pos], axis=-1)
    mask_m = jnp.concatenate([jnp.repeat(valid, MOBA_BLOCK, axis=-1), own_pos <= t4], axis=-1)
    moba_kv = (jnp.arange(MOBA_HEADS) // MOBA_GROUP)[None, None, :, None]
    k_rows = gather_rows(cache_mk, mk, page_table, rows_pos, moba_kv)
    v_rows = gather_rows(cache_mv, mv, page_table, rows_pos, moba_kv)
    o_m = moba_attend(mq, k_rows, v_rows, mask_m)
    ck_c = compress(all_rows(cache_ck, ck), cmp_k_pe, cmp_k_w1, cmp_k_w2)
    cv_c = compress(all_rows(cache_cv, cv), cmp_v_pe, cmp_v_w1, cmp_v_w2)
    o_c, p_kv = nsa_compressed(nq_nope, ck_c, cv_c, pos)
    nsb = -(-T // SEL_LEN)
    idx_s, valid_s = nsa_select(p_kv, pos, nsb)
    n_rows = idx_s.shape[-1] * SEL_LEN
    row_pos = idx_s[..., None] * SEL_LEN + jnp.arange(SEL_LEN)
    mask_s = (valid_s[..., None] & (row_pos <= pos[:, :, None, None, None])).reshape(DB, DS, NSA_KV_HEADS, n_rows)
    nsa_kv = jnp.arange(NSA_KV_HEADS)[None, None, :, None, None]
    k_sel = gather_rows(cache_sk, sk, page_table, row_pos, nsa_kv).reshape(DB, DS, NSA_KV_HEADS, n_rows, HEAD_DIM)
    v_sel = gather_rows(cache_sv, sv, page_table, row_pos, nsa_kv).reshape(DB, DS, NSA_KV_HEADS, n_rows, HEAD_DIM)
    o_s = nsa_selected_attend(nq, k_sel, v_sel, mask_s)
    wk_all = jnp.concatenate([state_wk, wk], axis=1)
    wv_all = jnp.concatenate([state_wv, wv], axis=1)
    w_buf = state_wk.shape[1]
    w_pos = n_past - w_buf + jnp.arange(w_buf + DS)
    dist = pos[:, :, None] - w_pos[None, None, :]
    o_w = nsa_window_attend(nq, wk_all, wv_all, (dist >= 0) & (dist < WINDOW))
    o_n = nsa_combine(gates, o_c, o_s, o_w)
    out = jnp.concatenate([o_m.reshape(DB, DS, D_MOBA_Q), o_n.reshape(DB, DS, D_NSA_Q)], axis=-1)
    return out, (mk, mv, ck, cv, sk, sv, wk_all[:, DS:], wv_all[:, DS:])


def layer_forward(x, c, mixer, w_ada, b_ada, norm_ffn1, w_ffn1_in, w_ffn1_out, norm_mix,
                  w_mix_in, w_mix_out, norm_ffn2, w_ffn2_in, w_ffn2_out):
    mods = jnp.split((jax.nn.silu(c) @ w_ada + b_ada)[:, None, :], N_MOD, axis=-1)
    sh1, sc1, gt1, sh2, sc2, gt2, sh3, sc3, gt3 = mods
    x = x + MACARON_WEIGHT * gt1 * swiglu(modulate(x, norm_ffn1, sh1, sc1), w_ffn1_in, w_ffn1_out)
    mix_out, new_state = mixer(modulate(x, norm_mix, sh2, sc2) @ w_mix_in)
    x = x + gt2 * (mix_out @ w_mix_out)
    x = x + MACARON_WEIGHT * gt3 * swiglu(modulate(x, norm_ffn2, sh3, sc3), w_ffn2_in, w_ffn2_out)
    return x, new_state


def setup_inputs(seed: int = 0) -> dict:
    key = jax.random.key(seed)
    ks = iter(jax.random.split(key, 48))

    def nrm(shape, scale):
        return jax.random.normal(next(ks), shape, jnp.float32) * scale

    n_pages = PAST_LEN // PAGE_SIZE
    n_used = DEC_BATCH * n_pages
    n_pool = n_used + n_used // 4
    w_buf = min(WINDOW, PAST_LEN)
    page_table = jax.random.permutation(next(ks), n_pool)[:n_used].reshape(DEC_BATCH, n_pages).astype(jnp.int32)
    moba_pool = (DEPTH, n_pool, PAGE_SIZE, MOBA_KV_HEADS, HEAD_DIM)
    nsa_pool = (DEPTH, n_pool, PAGE_SIZE, NSA_KV_HEADS, HEAD_DIM)
    win = (DEPTH, DEC_BATCH, w_buf, NSA_KV_HEADS, HEAD_DIM)
    return {
        'x_prompt': nrm((BATCH, SEQ, D_MODEL), 1.0),
        'x_sample': nrm((DEC_BATCH, DEC_SEQ, D_MODEL), 1.0),
        'cache_moba_k': nrm(moba_pool, 1.0),
        'cache_moba_v': nrm(moba_pool, 1.0),
        'cache_nsa_cmp_k': nrm(nsa_pool, 1.0),
        'cache_nsa_cmp_v': nrm(nsa_pool, 1.0),
        'cache_nsa_sel_k': nrm(nsa_pool, 1.0),
        'cache_nsa_sel_v': nrm(nsa_pool, 1.0),
        'state_nsa_win_k': nrm(win, 1.0),
        'state_nsa_win_v': nrm(win, 1.0),
        'page_table': page_table,
        'c_prompt': nrm((BATCH, D_MODEL), 1.0),
        'c_sample': nrm((DEC_BATCH, D_MODEL), 1.0),
        'w_ada': nrm((DEPTH, D_MODEL, N_MOD * D_MODEL), D_MODEL ** -0.5),
        'b_ada': nrm((DEPTH, N_MOD * D_MODEL), 0.02),
        'norm_ffn1': 1.0 + nrm((DEPTH, D_MODEL), 0.05),
        'w_ffn1_in': nrm((DEPTH, D_MODEL, 2 * D_FF), D_MODEL ** -0.5),
        'w_ffn1_out': nrm((DEPTH, D_FF, D_MODEL), D_FF ** -0.5),
        'norm_mix': 1.0 + nrm((DEPTH, D_MODEL), 0.05),
        'w_mix_in': nrm((DEPTH, D_MODEL, D_IN), D_MODEL ** -0.5),
        'w_mix_out': nrm((DEPTH, MIX_WIDTH, D_MODEL), MIX_WIDTH ** -0.5),
        'norm_ffn2': 1.0 + nrm((DEPTH, D_MODEL), 0.05),
        'w_ffn2_in': nrm((DEPTH, D_MODEL, 2 * D_FF), D_MODEL ** -0.5),
        'w_ffn2_out': nrm((DEPTH, D_FF, D_MODEL), D_FF ** -0.5),
        'cmp_k_pe': nrm((DEPTH, CMP_LEN, HEAD_DIM), 0.1),
        'cmp_k_w1': nrm((DEPTH, CMP_LEN, HEAD_DIM, CMP_HIDDEN), (CMP_LEN * HEAD_DIM) ** -0.5),
        'cmp_k_w2': nrm((DEPTH, CMP_HIDDEN, HEAD_DIM), CMP_HIDDEN ** -0.5),
        'cmp_v_pe': nrm((DEPTH, CMP_LEN, HEAD_DIM), 0.1),
        'cmp_v_w1': nrm((DEPTH, CMP_LEN, HEAD_DIM, CMP_HIDDEN), (CMP_LEN * HEAD_DIM) ** -0.5),
        'cmp_v_w2': nrm((DEPTH, CMP_HIDDEN, HEAD_DIM), CMP_HIDDEN ** -0.5),
        'norm_final': 1.0 + nrm((D_MODEL,), 0.05),
    }


def reference(x_prompt, x_sample, cache_moba_k, cache_moba_v, cache_nsa_cmp_k, cache_nsa_cmp_v,
              cache_nsa_sel_k, cache_nsa_sel_v, state_nsa_win_k, state_nsa_win_v, page_table,
              c_prompt, c_sample, w_ada, b_ada, norm_ffn1, w_ffn1_in, w_ffn1_out, norm_mix,
              w_mix_in, w_mix_out, norm_ffn2, w_ffn2_in, w_ffn2_out, cmp_k_pe, cmp_k_w1, cmp_k_w2,
              cmp_v_pe, cmp_v_w1, cmp_v_w2, norm_final):
    xp, xs = x_prompt, x_sample
    states_p, states_s = [], []
    for l in range(DEPTH):
        wl = (w_ada[l], b_ada[l], norm_ffn1[l], w_ffn1_in[l], w_ffn1_out[l], norm_mix[l],
              w_mix_in[l], w_mix_out[l], norm_ffn2[l], w_ffn2_in[l], w_ffn2_out[l])
        cw = (cmp_k_pe[l], cmp_k_w1[l], cmp_k_w2[l], cmp_v_pe[l], cmp_v_w1[l], cmp_v_w2[l])
        xp, sp = layer_forward(xp, c_prompt, lambda h: mixer_prompt(h, *cw), *wl)
        xs, ss = layer_forward(
            xs, c_sample,
            lambda h: mixer_sample(h, page_table, cache_moba_k[l], cache_moba_v[l], cache_nsa_cmp_k[l],
                                   cache_nsa_cmp_v[l], cache_nsa_sel_k[l], cache_nsa_sel_v[l],
                                   state_nsa_win_k[l], state_nsa_win_v[l], *cw),
            *wl)
        states_p.append(sp)
        states_s.append(ss)

    def stack(states, i):
        return jnp.stack([s[i] for s in states])

    y_prompt = rms_norm(xp, norm_final)
    y_sample = rms_norm(xs, norm_final)
    moba_k_prompt, moba_k_sample = stack(states_p, 0), stack(states_s, 0)
    moba_v_prompt, moba_v_sample = stack(states_p, 1), stack(states_s, 1)
    cmp_k_prompt, cmp_k_sample = stack(states_p, 2), stack(states_s, 2)
    cmp_v_prompt, cmp_v_sample = stack(states_p, 3), stack(states_s, 3)
    sel_k_prompt, sel_k_sample = stack(states_p, 4), stack(states_s, 4)
    sel_v_prompt, sel_v_sample = stack(states_p, 5), stack(states_s, 5)
    win_k_prompt, win_k_sample = stack(states_p, 6), stack(states_s, 6)
    win_v_prompt, win_v_sample = stack(states_p, 7), stack(states_s, 7)
    return (y_prompt, y_sample,
            moba_k_prompt, moba_k_sample, moba_v_prompt, moba_v_sample,
            cmp_k_prompt, cmp_k_sample, cmp_v_prompt, cmp_v_sample,
            sel_k_prompt, sel_k_sample, sel_v_prompt, sel_v_sample,
            win_k_prompt, win_k_sample, win_v_prompt, win_v_sample)
```

```python
import functools

import numpy as np
import jax
import jax.numpy as jnp
from jax import lax
from jax.experimental import pallas as pl
from jax.experimental.pallas import tpu as pltpu

F32 = jnp.float32
BF16 = jnp.bfloat16
NEG_INF = float("-inf")

HEAD_DIM = 64
MOBA_HEADS = 8
MOBA_KV_HEADS = 4
MOBA_GROUP = MOBA_HEADS // MOBA_KV_HEADS
NSA_HEADS = 8
NSA_KV_HEADS = 2
NSA_GROUP = NSA_HEADS // NSA_KV_HEADS
ROPE_DIM = HEAD_DIM // 4
ROPE_THETA = 500000.0
MOBA_BLOCK = 256
MOBA_TOPK = 3
CMP_LEN = 32
CMP_STRIDE = 16
CMP_HIDDEN = 2 * HEAD_DIM
SEL_LEN = 64
SEL_TOPN = 16
WINDOW = 512
MACARON_WEIGHT = 0.5
N_MOD = 9
EPS = 1e-6
SCALE = HEAD_DIM ** -0.5
D_MOBA_Q = MOBA_HEADS * HEAD_DIM
D_MOBA_KV = MOBA_KV_HEADS * HEAD_DIM
D_NSA_Q = NSA_HEADS * HEAD_DIM
D_NSA_KV = NSA_KV_HEADS * HEAD_DIM
N_GATES = 3 * NSA_HEADS
LANE = 128
PROJ_PAD = 2432
VMEM_LIMIT = 56 * 1024 * 1024
Q_TILE = 256
FFN_CHUNK = 256

OFF_MQ, OFF_MK, OFF_MV, OFF_NQ = 0, 512, 768, 1024
OFF_CK, OFF_CV, OFF_SK, OFF_SV, OFF_WK, OFF_WV, OFF_G = 1536, 1664, 1792, 1920, 2048, 2176, 2304


def _silu(x):
    return x / (1.0 + jnp.exp(-x))


def _sigmoid(x):
    return 1.0 / (1.0 + jnp.exp(-x))


def _dot(a, b):
    return jnp.dot(a, b, preferred_element_type=F32)


def _dot_nt(a, b, precision=None):
    return lax.dot_general(a, b, (((1,), (1,)), ((), ())), preferred_element_type=F32, precision=precision)


def _modulated(x, g, shift, scale):
    ms = jnp.mean(x * x, axis=-1, keepdims=True)
    return (x * lax.rsqrt(ms + EPS) * g) * (1.0 + scale) + shift


def _rank(sc, ncols):
    lane = lax.broadcasted_iota(jnp.int32, sc.shape, 1)
    rank = jnp.zeros(sc.shape, F32)
    for i in range(ncols):
        ci = sc[:, i:i + 1]
        beats = (ci > sc) | ((ci == sc) & (lane > i))
        rank = rank + jnp.where(beats, 1.0, 0.0)
    return rank


def _ada_kernel(c_ref, w_ref, b_ref, o_ref):
    sc = _silu(c_ref[...]).astype(BF16)
    o_ref[0] = _dot(sc, w_ref[...].astype(BF16)) + b_ref[0]


def _ada_mods(c_all, w_ada, b_ada):
    rows, d = c_all.shape
    return pl.pallas_call(
        _ada_kernel,
        grid=(N_MOD,),
        in_specs=[pl.BlockSpec((rows, d), lambda j: (0, 0)),
                  pl.BlockSpec((d, d), lambda j: (0, j)),
                  pl.BlockSpec((1, 1, d), lambda j: (j, 0, 0))],
        out_specs=pl.BlockSpec((1, rows, d), lambda j: (j, 0, 0)),
        out_shape=jax.ShapeDtypeStruct((N_MOD, rows, d), F32),
        compiler_params=pltpu.CompilerParams(dimension_semantics=("arbitrary",), vmem_limit_bytes=VMEM_LIMIT),
        name="ada_mods",
    )(c_all, w_ada, b_ada.reshape(N_MOD, 1, d))


class _Mods:
    def __init__(self, mods, n_sample, per_row):
        self.per_row = per_row
        self.n_sample = n_sample
        self.rows = mods.shape[1]
        d = mods.shape[2]
        self.d = d
        self.arr = mods if per_row else mods.reshape(N_MOD * self.rows, 1, d)

    def spec(self, j):
        if self.per_row:
            return pl.BlockSpec((1, self.n_sample, self.d), lambda g, i: (j, 0, 0))
        base = j * self.rows + self.n_sample
        return pl.BlockSpec((1, 1, self.d), lambda g, i: (base + g, 0, 0))


def _ffn_kernel(*refs, has_mix, final_norm, n_chunks):
    it = iter(refs)
    x_ref = next(it)
    if has_mix:
        mm_ref, mn_ref, wmm_ref, wmn_ref, gmix_ref = next(it), next(it), next(it), next(it), next(it)
    sh_ref, sc_ref, gt_ref, g_ref, wa_ref, wb_ref, wo_ref = (next(it) for _ in range(7))
    gf_ref = next(it) if final_norm else None
    o_ref, h_ref = next(it), next(it)

    x = x_ref[0]
    if has_mix:
        x = x + gmix_ref[0] * (_dot(mm_ref[0], wmm_ref[...]) + _dot(mn_ref[0], wmn_ref[...]))
    xm = _modulated(x, g_ref[...], sh_ref[0], sc_ref[0]).astype(BF16)
    tf = wa_ref.shape[2]
    for j in range(n_chunks):
        a = _dot(xm, wa_ref[j])
        b = _dot(xm, wb_ref[j])
        h_ref[:, j * tf:(j + 1) * tf] = (_silu(a) * b).astype(BF16)
    y = x + (MACARON_WEIGHT * gt_ref[0]) * _dot(h_ref[...], wo_ref[...])
    if final_norm:
        ms = jnp.mean(y * y, axis=-1, keepdims=True)
        y = y * lax.rsqrt(ms + EPS) * gf_ref[...]
    o_ref[0] = y


def _ffn(x, mods, jmods, norm_g, wa, wb, wo, tm, mix=None, final_g=None):
    g_, r_, d = x.shape
    n_chunks, _, tf = wa.shape
    f = n_chunks * tf
    const2 = lambda g, i: (0, 0)
    const3 = lambda g, i: (0, 0, 0)
    row_spec = lambda w: pl.BlockSpec((1, tm, w), lambda g, i: (g, i, 0))
    args, specs = [x], [row_spec(d)]
    if mix is not None:
        mm, mn, wmm, wmn, jmix = mix
        args += [mm, mn, wmm, wmn, mods.arr]
        specs += [row_spec(mm.shape[2]), row_spec(mn.shape[2]),
                  pl.BlockSpec(wmm.shape, const2), pl.BlockSpec(wmn.shape, const2), mods.spec(jmix)]
    args += [mods.arr, mods.arr, mods.arr, norm_g.reshape(1, d), wa, wb, wo]
    specs += [mods.spec(jmods[0]), mods.spec(jmods[1]), mods.spec(jmods[2]),
              pl.BlockSpec((1, d), const2),
              pl.BlockSpec(wa.shape, const3), pl.BlockSpec(wb.shape, const3), pl.BlockSpec(wo.shape, const2)]
    if final_g is not None:
        args.append(final_g.reshape(1, d))
        specs.append(pl.BlockSpec((1, d), const2))
    kern = functools.partial(_ffn_kernel, has_mix=mix is not None, final_norm=final_g is not None, n_chunks=n_chunks)
    return pl.pallas_call(
        kern,
        grid=(g_, r_ // tm),
        in_specs=specs,
        out_specs=row_spec(d),
        out_shape=jax.ShapeDtypeStruct((g_, r_, d), F32),
        scratch_shapes=[pltpu.VMEM((tm, f), BF16)],
        compiler_params=pltpu.CompilerParams(dimension_semantics=("arbitrary", "arbitrary"),
                                             vmem_limit_bytes=VMEM_LIMIT),
        name="ffn",
    )(*args)


def _rope_tables(pos):
    half = ROPE_DIM // 2
    inv_freq = ROPE_THETA ** (-jnp.arange(half, dtype=F32) / half)
    ang = pos.astype(F32)[:, None] * inv_freq
    cos, sin = jnp.cos(ang), jnp.sin(ang)
    rows = pos.shape[0]
    rest = HEAD_DIM - ROPE_DIM
    one, zero = jnp.ones((rows, rest), F32), jnp.zeros((rows, rest), F32)
    zh = jnp.zeros((rows, half), F32)
    c = jnp.concatenate([cos, cos, one], axis=1)
    sa = jnp.concatenate([-sin, zh, zero], axis=1)
    sb = jnp.concatenate([zh, sin, zero], axis=1)
    tile = lambda t: jnp.concatenate([t, t], axis=1)
    return tile(c), tile(sa), tile(sb)


def _proj_kernel(*refs, head_major):
    (x_ref, sh_ref, sc_ref, g_ref, w_ref, cos_ref, sa_ref, sb_ref) = refs[:8]
    outs = refs[8:]
    (mq_ref, mk_ref, mv_ref, ck_ref, cv_ref, sk_ref, sv_ref, wk_ref, wv_ref, gate_ref) = outs[:10]
    xm = _modulated(x_ref[0], g_ref[...], sh_ref[0], sc_ref[0]).astype(BF16)
    cos, sa, sb = cos_ref[...], sa_ref[...], sb_ref[...]
    half = ROPE_DIM // 2

    def seg(lo, width):
        return _dot(xm, w_ref[:, lo:lo + width])

    def rope(y):
        parts = []
        for c in range(y.shape[1] // LANE):
            s = y[:, c * LANE:(c + 1) * LANE]
            parts.append(s * cos + pltpu.roll(s, LANE - half, 1) * sa + pltpu.roll(s, half, 1) * sb)
        return parts[0] if len(parts) == 1 else jnp.concatenate(parts, axis=1)

    mq = rope(seg(OFF_MQ, D_MOBA_Q))
    mk = rope(seg(OFF_MK, D_MOBA_KV))
    mv = seg(OFF_MV, D_MOBA_KV)
    nqn = seg(OFF_NQ, D_NSA_Q)
    nq = rope(nqn)
    ck = seg(OFF_CK, D_NSA_KV)
    cv = seg(OFF_CV, D_NSA_KV)
    sk = rope(seg(OFF_SK, D_NSA_KV))
    sv = seg(OFF_SV, D_NSA_KV)
    wk = rope(seg(OFF_WK, D_NSA_KV))
    wv = seg(OFF_WV, D_NSA_KV)
    gate_ref[0] = _sigmoid(seg(OFF_G, LANE))
    mq_ref[0], mk_ref[0], mv_ref[0] = mq, mk, mv
    ck_ref[0], cv_ref[0], sk_ref[0], sv_ref[0], wk_ref[0], wv_ref[0] = ck, cv, sk, sv, wk, wv
    if not head_major:
        nqn_ref, nq_ref = outs[10:]
        nqn_ref[0], nq_ref[0] = nqn, nq
    else:
        (mqh_ref, mkh_ref, mvh_ref, nqnh_ref, nqh_ref, skh_ref, svh_ref, wkh_ref, wvh_ref) = outs[10:]

        def put(ref, val, n, scale):
            for h in range(n):
                piece = val[:, h * HEAD_DIM:(h + 1) * HEAD_DIM]
                ref[0, h] = (piece * scale if scale != 1.0 else piece).astype(BF16)

        put(mqh_ref, mq, MOBA_HEADS, SCALE)
        put(mkh_ref, mk, MOBA_KV_HEADS, 1.0)
        put(mvh_ref, mv, MOBA_KV_HEADS, 1.0)
        put(nqnh_ref, nqn, NSA_HEADS, SCALE)
        put(nqh_ref, nq, NSA_HEADS, SCALE)
        put(skh_ref, sk, NSA_KV_HEADS, 1.0)
        put(svh_ref, sv, NSA_KV_HEADS, 1.0)
        put(wkh_ref, wk, NSA_KV_HEADS, 1.0)
        put(wvh_ref, wv, NSA_KV_HEADS, 1.0)


def _proj(x, mods, jmods, norm_g, w_pad, tables, tm, head_major):
    g_, r_, d = x.shape
    const2 = lambda g, i: (0, 0)
    row_spec = lambda w: pl.BlockSpec((1, tm, w), lambda g, i: (g, i, 0))
    tab_spec = pl.BlockSpec((tm, LANE), lambda g, i: (i, 0))
    in_specs = [row_spec(d), mods.spec(jmods[0]), mods.spec(jmods[1]), pl.BlockSpec((1, d), const2),
                pl.BlockSpec(w_pad.shape, const2), tab_spec, tab_spec, tab_spec]
    widths = [D_MOBA_Q, D_MOBA_KV, D_MOBA_KV] + [D_NSA_KV] * 6 + [LANE]
    if not head_major:
        widths += [D_NSA_Q, D_NSA_Q]
    out_specs = [row_spec(w) for w in widths]
    out_shape = [jax.ShapeDtypeStruct((g_, r_, w), F32) for w in widths]
    if head_major:
        for n in (MOBA_HEADS, MOBA_KV_HEADS, MOBA_KV_HEADS, NSA_HEADS, NSA_HEADS) + (NSA_KV_HEADS,) * 4:
            out_specs.append(pl.BlockSpec((1, n, tm, HEAD_DIM), lambda g, i: (g, 0, i, 0)))
            out_shape.append(jax.ShapeDtypeStruct((g_, n, r_, HEAD_DIM), BF16))
    return pl.pallas_call(
        functools.partial(_proj_kernel, head_major=head_major),
        grid=(g_, r_ // tm),
        in_specs=in_specs,
        out_specs=out_specs,
        out_shape=out_shape,
        compiler_params=pltpu.CompilerParams(dimension_semantics=("arbitrary", "arbitrary"),
                                             vmem_limit_bytes=VMEM_LIMIT),
        name="proj",
    )(x, mods.arr, mods.arr, norm_g.reshape(1, d), w_pad, *tables)


def _compress_rows(x, pe_ref, w1_ref, w2_ref):
    n = x.shape[0]
    p0 = _dot((x + pe_ref[0]).astype(BF16), w1_ref[0])
    p1 = _dot((x + pe_ref[1]).astype(BF16), w1_ref[1])
    h = p0 + pltpu.roll(p1, n - 1, 0)
    return _dot(_silu(h).astype(BF16), w2_ref[...])


def _compress_kernel(ck_ref, cv_ref, kpe_ref, kw1_ref, kw2_ref, vpe_ref, vw1_ref, vw2_ref, okc_ref, ovc_ref):
    okc_ref[0] = _compress_rows(ck_ref[0], kpe_ref, kw1_ref, kw2_ref)
    ovc_ref[0] = _compress_rows(cv_ref[0], vpe_ref, vw1_ref, vw2_ref)


def _compress_weights(pe, w1, w2):
    ratio = CMP_LEN // CMP_STRIDE
    eye = jnp.eye(NSA_KV_HEADS, dtype=F32)
    w1r = w1.reshape(ratio, CMP_STRIDE, HEAD_DIM, CMP_HIDDEN)
    w1f = jnp.einsum("rlde,kK->rlkdKe", w1r, eye).reshape(ratio, CMP_STRIDE * D_NSA_KV, NSA_KV_HEADS * CMP_HIDDEN)
    pef = jnp.broadcast_to(pe.reshape(ratio, CMP_STRIDE, 1, HEAD_DIM),
                           (ratio, CMP_STRIDE, NSA_KV_HEADS, HEAD_DIM)).reshape(ratio, 1, CMP_STRIDE * D_NSA_KV)
    w2f = jnp.einsum("ed,kK->keKd", w2, eye).reshape(NSA_KV_HEADS * CMP_HIDDEN, D_NSA_KV)
    return pef, w1f.astype(BF16), w2f.astype(BF16)


def _compress_prompt(ck, cv, kw, vw):
    b, s, _ = ck.shape
    n = s // CMP_STRIDE
    width = CMP_STRIDE * D_NSA_KV
    ck2, cv2 = ck.reshape(b, n, width), cv.reshape(b, n, width)
    row = pl.BlockSpec((1, n, width), lambda i: (i, 0, 0))
    wspecs = []
    for w in kw + vw:
        wspecs.append(pl.BlockSpec(w.shape, (lambda i: (0, 0, 0)) if w.ndim == 3 else (lambda i: (0, 0))))
    out = pl.BlockSpec((1, n, D_NSA_KV), lambda i: (i, 0, 0))
    return pl.pallas_call(
        _compress_kernel,
        grid=(b,),
        in_specs=[row, row] + wspecs,
        out_specs=[out, out],
        out_shape=[jax.ShapeDtypeStruct((b, n, D_NSA_KV), F32)] * 2,
        compiler_params=pltpu.CompilerParams(dimension_semantics=("arbitrary",), vmem_limit_bytes=VMEM_LIMIT),
        name="compress_prompt",
    )(ck2, cv2, *kw, *vw)


def _flash_init(m_ref, l_ref, acc_ref):
    m_ref[...] = jnp.full(m_ref.shape, NEG_INF, F32)
    l_ref[...] = jnp.zeros(l_ref.shape, F32)
    acc_ref[...] = jnp.zeros(acc_ref.shape, F32)


def _flash_step(q, k, v, mask, m_ref, l_ref, acc_ref):
    s = jnp.where(mask, _dot_nt(q, k), NEG_INF)
    m_prev = m_ref[...]
    m_new = jnp.maximum(m_prev, jnp.max(s, axis=1, keepdims=True))
    m_safe = jnp.where(m_new == NEG_INF, 0.0, m_new)
    alpha = jnp.exp(m_prev - m_safe)
    p = jnp.exp(s - m_safe)
    l_ref[...] = alpha * l_ref[...] + jnp.sum(p, axis=1, keepdims=True)
    acc_ref[...] = alpha * acc_ref[...] + _dot(p.astype(BF16), v)
    m_ref[...] = m_new


def _flash_out(l_ref, acc_ref):
    return acc_ref[...] / jnp.maximum(l_ref[...], 1e-30)


def _moba_prompt_kernel(mq_ref, mqh_ref, mk_ref, mkh_ref, mvh_ref, o_ref, kmean_ref, q_ref, sel_ref, m_ref, l_ref, acc_ref,
                        *, n_blocks):
    qi = pl.program_id(1)
    tq = mq_ref.shape[1]

    @pl.when(qi == 0)
    def _():
        for j in range(n_blocks):
            blk = mk_ref[0, j * MOBA_BLOCK:(j + 1) * MOBA_BLOCK, :]
            kmean_ref[j:j + 1, :] = jnp.sum(blk, axis=0, keepdims=True) * (1.0 / MOBA_BLOCK)

    col = lax.broadcasted_iota(jnp.int32, (tq, n_blocks), 1)
    rows2 = MOBA_GROUP * tq
    r_loc = lax.broadcasted_iota(jnp.int32, (rows2, MOBA_BLOCK), 0) % tq
    c_loc = lax.broadcasted_iota(jnp.int32, (rows2, MOBA_BLOCK), 1)
    col2 = lax.broadcasted_iota(jnp.int32, (rows2, n_blocks), 1)

    pieces = []
    for k in range(MOBA_KV_HEADS):
        km = kmean_ref[:, k * HEAD_DIM:(k + 1) * HEAD_DIM]
        for g in range(MOBA_GROUP):
            h = k * MOBA_GROUP + g
            qf = mq_ref[0, :, h * HEAD_DIM:(h + 1) * HEAD_DIM]
            sb = _dot_nt(qf, km, precision=lax.Precision.HIGHEST)
            sb = jnp.where(col < qi, sb, NEG_INF)
            sel = (_rank(sb, n_blocks) < MOBA_TOPK) & (col < qi)
            sel_ref[g * tq:(g + 1) * tq, :] = jnp.where(sel, 1.0, 0.0)
            q_ref[g * tq:(g + 1) * tq, :] = mqh_ref[0, h]
        _flash_init(m_ref, l_ref, acc_ref)

        def body(j, carry):
            start = pl.multiple_of(j * MOBA_BLOCK, MOBA_BLOCK)
            kc = mkh_ref[0, k, pl.ds(start, MOBA_BLOCK), :]
            vc = mvh_ref[0, k, pl.ds(start, MOBA_BLOCK), :]
            selcol = jnp.sum(jnp.where(col2 == j, sel_ref[...], 0.0), axis=1, keepdims=True)
            limit = jnp.where(j == qi, r_loc, jnp.where(selcol > 0.5, MOBA_BLOCK, -1))
            mask = c_loc <= limit
            _flash_step(q_ref[...], kc, vc, mask, m_ref, l_ref, acc_ref)
            return carry

        lax.fori_loop(0, qi + 1, body, 0)
        o = _flash_out(l_ref, acc_ref)
        pieces += [o[g * tq:(g + 1) * tq, :] for g in range(MOBA_GROUP)]
    o_ref[0] = jnp.concatenate(pieces, axis=1).astype(BF16)


def _moba_prompt(mq, mqh, mk, mkh, mvh):
    b, s, _ = mq.shape
    tq = Q_TILE
    n_blocks = s // MOBA_BLOCK
    full = lambda n: pl.BlockSpec((1, n, s, HEAD_DIM), lambda i, j: (i, 0, 0, 0))
    return pl.pallas_call(
        functools.partial(_moba_prompt_kernel, n_blocks=n_blocks),
        grid=(b, s // tq),
        in_specs=[pl.BlockSpec((1, tq, D_MOBA_Q), lambda i, j: (i, j, 0)),
                  pl.BlockSpec((1, MOBA_HEADS, tq, HEAD_DIM), lambda i, j: (i, 0, j, 0)),
                  pl.BlockSpec((1, s, D_MOBA_KV), lambda i, j: (i, 0, 0)),
                  full(MOBA_KV_HEADS), full(MOBA_KV_HEADS)],
        out_specs=pl.BlockSpec((1, tq, D_MOBA_Q), lambda i, j: (i, j, 0)),
        out_shape=jax.ShapeDtypeStruct((b, s, D_MOBA_Q), BF16),
        scratch_shapes=[pltpu.VMEM((n_blocks, D_MOBA_KV), F32),
                        pltpu.VMEM((MOBA_GROUP * tq, HEAD_DIM), BF16),
                        pltpu.VMEM((MOBA_GROUP * tq, n_blocks), F32),
                        pltpu.VMEM((MOBA_GROUP * tq, 1), F32),
                        pltpu.VMEM((MOBA_GROUP * tq, 1), F32),
                        pltpu.VMEM((MOBA_GROUP * tq, HEAD_DIM), F32)],
        compiler_params=pltpu.CompilerParams(dimension_semantics=("arbitrary", "arbitrary"),
                                             vmem_limit_bytes=VMEM_LIMIT),
        name="moba_prompt",
    )(mq, mqh, mk, mkh, mvh)


def _cmp_to_sel(nc_pad, nc, nsb_pad, nsb):
    i = np.arange(nc_pad)[:, None]
    j = np.arange(nsb_pad)[None, :]
    start = i * CMP_STRIDE
    m = (start < (j + 1) * SEL_LEN) & (start + CMP_LEN > j * SEL_LEN) & (i < nc) & (j < nsb)
    return m.astype(np.float32)


def _block_expand(n_blocks_pad, n_blocks, block_len, chunk):
    n_chunks = n_blocks * block_len // chunk
    pos = np.arange(n_chunks * chunk).reshape(n_chunks, 1, chunk)
    j = np.arange(n_blocks_pad).reshape(1, n_blocks_pad, 1)
    return (pos // block_len == j).astype(np.float32)


def _nsa_prompt_kernel(nqnh_ref, nqh_ref, gate_ref, ckc_ref, cvc_ref, skh_ref, svh_ref, wkh_ref, wvh_ref, msel_ref, exp_ref,
                       o_ref, q_ref, sel_ref, m_ref, l_ref, acc_ref, *, n_cmp, n_sel):
    qi = pl.program_id(1)
    tq = nqnh_ref.shape[2]
    rows = NSA_GROUP * tq
    nc_pad = ckc_ref.shape[1]
    nsb_pad = msel_ref.shape[1]
    chunk = exp_ref.shape[2]
    q0 = qi * tq
    gates = gate_ref[0]

    qpos_c = q0 + lax.broadcasted_iota(jnp.int32, (rows, nc_pad), 0) % tq
    n_idx = lax.broadcasted_iota(jnp.int32, (rows, nc_pad), 1)
    cmp_mask = (n_idx * CMP_STRIDE + (CMP_LEN - 1) <= qpos_c) & (n_idx < n_cmp)

    t = q0 + lax.broadcasted_iota(jnp.int32, (tq, nsb_pad), 0)
    jb = lax.broadcasted_iota(jnp.int32, (tq, nsb_pad), 1)
    cur = t // SEL_LEN
    valid = (jb * SEL_LEN <= t) & (jb < n_sel)
    forced = (jb == 0) | (jb == cur) | (jb == cur - 1)

    r_loc = q0 + lax.broadcasted_iota(jnp.int32, (rows, chunk), 0) % tq
    c_loc = lax.broadcasted_iota(jnp.int32, (rows, chunk), 1)

    pieces = []
    for k in range(NSA_KV_HEADS):
        lo = k * HEAD_DIM
        for g in range(NSA_GROUP):
            q_ref[g * tq:(g + 1) * tq, :] = nqnh_ref[0, k * NSA_GROUP + g]
        kc = ckc_ref[0, :, lo:lo + HEAD_DIM].astype(BF16)
        vc = cvc_ref[0, :, lo:lo + HEAD_DIM].astype(BF16)
        s = jnp.where(cmp_mask, _dot_nt(q_ref[...], kc), NEG_INF)
        m = jnp.max(s, axis=1, keepdims=True)
        m = jnp.where(m == NEG_INF, 0.0, m)
        e = jnp.where(cmp_mask, jnp.exp(s - m), 0.0)
        p = e / jnp.maximum(jnp.sum(e, axis=1, keepdims=True), 1e-30)
        o_c = _dot(p.astype(BF16), vc)
        p_kv = p[0:tq]
        for g in range(1, NSA_GROUP):
            p_kv = p_kv + p[g * tq:(g + 1) * tq]
        score = jnp.dot(p_kv, msel_ref[...], preferred_element_type=F32, precision=lax.Precision.HIGHEST)
        score = jnp.where(valid, jnp.where(forced, jnp.inf, score), NEG_INF)
        sel = (_rank(score, n_sel) < SEL_TOPN) & valid
        sel_ref[...] = jnp.where(sel, 1.0, 0.0).astype(BF16)
        for g in range(NSA_GROUP):
            q_ref[g * tq:(g + 1) * tq, :] = nqh_ref[0, k * NSA_GROUP + g]
        _flash_init(m_ref, l_ref, acc_ref)

        def sel_body(j, carry):
            start = pl.multiple_of(j * chunk, chunk)
            kj = skh_ref[0, k, pl.ds(start, chunk), :]
            vj = svh_ref[0, k, pl.ds(start, chunk), :]
            selx = _dot(sel_ref[...], exp_ref[j])
            selx = jnp.concatenate([selx] * NSA_GROUP, axis=0)
            mask = (selx > 0.5) & (start + c_loc <= r_loc)
            _flash_step(q_ref[...], kj, vj, mask, m_ref, l_ref, acc_ref)
            return carry

        lax.fori_loop(0, qi + 1, sel_body, 0)
        o_s = _flash_out(l_ref, acc_ref)
        _flash_init(m_ref, l_ref, acc_ref)

        def win_body(j, carry):
            start = pl.multiple_of(j * chunk, chunk)
            kj = wkh_ref[0, k, pl.ds(start, chunk), :]
            vj = wvh_ref[0, k, pl.ds(start, chunk), :]
            dist = r_loc - (start + c_loc)
            mask = (dist >= 0) & (dist < WINDOW)
            _flash_step(q_ref[...], kj, vj, mask, m_ref, l_ref, acc_ref)
            return carry

        lax.fori_loop(jnp.maximum(qi - WINDOW // chunk, 0), qi + 1, win_body, 0)
        o_w = _flash_out(l_ref, acc_ref)
        for g in range(NSA_GROUP):
            h = k * NSA_GROUP + g
            rs = slice(g * tq, (g + 1) * tq)
            pieces.append(gates[:, 3 * h:3 * h + 1] * o_c[rs] + gates[:, 3 * h + 1:3 * h + 2] * o_s[rs]
                          + gates[:, 3 * h + 2:3 * h + 3] * o_w[rs])
    o_ref[0] = jnp.concatenate(pieces, axis=1).astype(BF16)


def _nsa_prompt(nqnh, nqh, gates, ckc, cvc, skh, svh, wkh, wvh):
    b, _, s, _ = nqnh.shape
    tq = Q_TILE
    chunk = Q_TILE
    n_chunks = s // CMP_STRIDE
    n_cmp = n_chunks - CMP_LEN // CMP_STRIDE + 1
    n_sel = -(-s // SEL_LEN)
    nsb_pad = max(n_sel, 8)
    msel = jnp.asarray(_cmp_to_sel(n_chunks, n_cmp, nsb_pad, n_sel))
    expand = jnp.asarray(_block_expand(nsb_pad, n_sel, SEL_LEN, chunk)).astype(BF16)
    full = lambda n: pl.BlockSpec((1, n, s, HEAD_DIM), lambda i, j: (i, 0, 0, 0))
    qspec = pl.BlockSpec((1, NSA_HEADS, tq, HEAD_DIM), lambda i, j: (i, 0, j, 0))
    cspec = pl.BlockSpec((1, n_chunks, D_NSA_KV), lambda i, j: (i, 0, 0))
    rows = NSA_GROUP * tq
    return pl.pallas_call(
        functools.partial(_nsa_prompt_kernel, n_cmp=n_cmp, n_sel=n_sel),
        grid=(b, s // tq),
        in_specs=[qspec, qspec, pl.BlockSpec((1, tq, LANE), lambda i, j: (i, j, 0)), cspec, cspec,
                  full(NSA_KV_HEADS), full(NSA_KV_HEADS), full(NSA_KV_HEADS), full(NSA_KV_HEADS),
                  pl.BlockSpec(msel.shape, lambda i, j: (0, 0)),
                  pl.BlockSpec(expand.shape, lambda i, j: (0, 0, 0))],
        out_specs=pl.BlockSpec((1, tq, D_NSA_Q), lambda i, j: (i, j, 0)),
        out_shape=jax.ShapeDtypeStruct((b, s, D_NSA_Q), BF16),
        scratch_shapes=[pltpu.VMEM((rows, HEAD_DIM), BF16),
                        pltpu.VMEM((tq, nsb_pad), BF16),
                        pltpu.VMEM((rows, 1), F32),
                        pltpu.VMEM((rows, 1), F32),
                        pltpu.VMEM((rows, HEAD_DIM), F32)],
        compiler_params=pltpu.CompilerParams(dimension_semantics=("arbitrary", "arbitrary"),
                                             vmem_limit_bytes=VMEM_LIMIT),
        name="nsa_prompt",
    )(nqnh, nqh, gates, ckc, cvc, skh, svh, wkh, wvh, msel, expand)


def _page_copy(cache_ref, buf_ref, sem_ref, pt_ref, b, slot, p):
    return pltpu.make_async_copy(cache_ref.at[pt_ref[b, p]], buf_ref.at[slot, p], sem_ref.at[slot])


def _pages_start(caches, bufs, sems, pt_ref, b, slot, n_pages):
    def body(p, carry):
        for c, bf, sm in zip(caches, bufs, sems):
            _page_copy(c, bf, sm, pt_ref, b, slot, p).start()
        return carry
    lax.fori_loop(0, n_pages, body, 0)


def _pages_wait(caches, bufs, sems, pt_ref, b, slot, n_pages):
    def body(p, carry):
        for c, bf, sm in zip(caches, bufs, sems):
            _page_copy(c, bf, sm, pt_ref, b, slot, p).wait()
        return carry
    lax.fori_loop(0, n_pages, body, 0)


def _paged_step(caches, bufs, sems, pt_ref, n_pages):
    b = pl.program_id(0)
    nb = pl.num_programs(0)
    slot = b % 2

    @pl.when(b == 0)
    def _():
        _pages_start(caches, bufs, sems, pt_ref, b, slot, n_pages)

    @pl.when(b + 1 < nb)
    def _():
        _pages_start(caches, bufs, sems, pt_ref, b + 1, 1 - slot, n_pages)

    _pages_wait(caches, bufs, sems, pt_ref, b, slot, n_pages)
    return slot


def _pick_heads(o, n_heads, n_kv):
    group = n_heads // n_kv
    row = lax.broadcasted_iota(jnp.int32, (n_heads, HEAD_DIM), 0)
    out = jnp.zeros((n_heads, HEAD_DIM), F32)
    for k in range(n_kv):
        out = jnp.where(row // group == k, o[:, k * HEAD_DIM:(k + 1) * HEAD_DIM], out)
    return out


def _decode_attend(qe, kbuf, vbuf, slot, selx, k_new, v_new, sc_ref, rows_per_page):
    n_pages = kbuf.shape[1]
    qb = qe.astype(BF16)
    for p in range(n_pages):
        kp = kbuf[slot, p].astype(BF16)
        sc_ref[:, p * rows_per_page:(p + 1) * rows_per_page] = _dot_nt(qb, kp)
    s = jnp.where(selx, sc_ref[...], NEG_INF)
    s_new = jnp.sum(qe * k_new, axis=1, keepdims=True)
    m = jnp.maximum(jnp.max(s, axis=1, keepdims=True), s_new)
    e = jnp.exp(s - m)
    e_new = jnp.exp(s_new - m)
    inv = 1.0 / jnp.maximum(jnp.sum(e, axis=1, keepdims=True) + e_new, 1e-30)
    sc_ref[...] = e * inv
    o = (e_new * inv) * v_new
    for p in range(n_pages):
        pp = sc_ref[:, p * rows_per_page:(p + 1) * rows_per_page].astype(BF16)
        o = o + _dot(pp, vbuf[slot, p].astype(BF16))
    return o


def _moba_sample_kernel(pt_ref, qe_ref, knew_ref, vnew_ref, exp_ref, ck_hbm, cv_hbm, o_ref,
                        kbuf, vbuf, sc_ref, sem_k, sem_v, *, n_blocks):
    n_pages = kbuf.shape[1]
    page = kbuf.shape[2]
    slot = _paged_step((ck_hbm, cv_hbm), (kbuf, vbuf), (sem_k, sem_v), pt_ref, n_pages)
    ppb = MOBA_BLOCK // page
    qe = qe_ref[0]
    means = []
    for j in range(n_blocks):
        acc = jnp.sum(kbuf[slot, j * ppb], axis=0, keepdims=True)
        for r in range(1, ppb):
            acc = acc + jnp.sum(kbuf[slot, j * ppb + r], axis=0, keepdims=True)
        means.append(acc * (1.0 / MOBA_BLOCK))
    kmean = jnp.concatenate(means, axis=0)
    sb = _dot_nt(qe, kmean, precision=lax.Precision.HIGHEST)
    sel = _rank(sb, n_blocks) < MOBA_TOPK
    selx = _dot(jnp.where(sel, 1.0, 0.0).astype(BF16), exp_ref[...]) > 0.5
    o = _decode_attend(qe * SCALE, kbuf, vbuf, slot, selx, knew_ref[0], vnew_ref[0], sc_ref, page)
    o_ref[0] = _pick_heads(o, MOBA_HEADS, MOBA_KV_HEADS)


def _moba_sample(page_table, qe, k_new, v_new, cache_k, cache_v):
    db, n_pages = page_table.shape
    n_pool, page = cache_k.shape[0], cache_k.shape[1]
    n_past = n_pages * page
    n_blocks = n_past // MOBA_BLOCK
    ck = cache_k.reshape(n_pool, page, D_MOBA_KV)
    cv = cache_v.reshape(n_pool, page, D_MOBA_KV)
    expand = jnp.asarray(_block_expand(n_blocks, n_blocks, MOBA_BLOCK, n_past)[0]).astype(BF16)
    grid_spec = pltpu.PrefetchScalarGridSpec(
        num_scalar_prefetch=1,
        grid=(db,),
        in_specs=[pl.BlockSpec((1, MOBA_HEADS, D_MOBA_KV), lambda b, pt: (b, 0, 0)),
                  pl.BlockSpec((1, 1, D_MOBA_KV), lambda b, pt: (b, 0, 0)),
                  pl.BlockSpec((1, 1, D_MOBA_KV), lambda b, pt: (b, 0, 0)),
                  pl.BlockSpec(expand.shape, lambda b, pt: (0, 0)),
                  pl.BlockSpec(memory_space=pl.ANY),
                  pl.BlockSpec(memory_space=pl.ANY)],
        out_specs=pl.BlockSpec((1, MOBA_HEADS, HEAD_DIM), lambda b, pt: (b, 0, 0)),
        scratch_shapes=[pltpu.VMEM((2, n_pages, page, D_MOBA_KV), F32),
                        pltpu.VMEM((2, n_pages, page, D_MOBA_KV), F32),
                        pltpu.VMEM((MOBA_HEADS, n_past), F32),
                        pltpu.SemaphoreType.DMA((2,)),
                        pltpu.SemaphoreType.DMA((2,))])
    return pl.pallas_call(
        functools.partial(_moba_sample_kernel, n_blocks=n_blocks),
        grid_spec=grid_spec,
        out_shape=jax.ShapeDtypeStruct((db, MOBA_HEADS, HEAD_DIM), F32),
        compiler_params=pltpu.CompilerParams(dimension_semantics=("arbitrary",), vmem_limit_bytes=VMEM_LIMIT),
        name="moba_sample",
    )(page_table, qe, k_new, v_new, expand, ck, cv)


def _nsa_sample_kernel(pt_ref, qn_ref, qr_ref, gate_ref, sknew_ref, svnew_ref, wknew_ref, wvnew_ref, wk_ref, wv_ref,
                       kpe_ref, kw1_ref, kw2_ref, vpe_ref, vw1_ref, vw2_ref, msel_ref, exp_ref,
                       ck_hbm, cv_hbm, sk_hbm, sv_hbm, o_ref, wko_ref, wvo_ref,
                       ckbuf, cvbuf, skbuf, svbuf, sc_ref, sem_ck, sem_cv, sem_sk, sem_sv, *, n_cmp, n_sel):
    n_pages = skbuf.shape[1]
    page = skbuf.shape[2]
    slot = _paged_step((ck_hbm, cv_hbm, sk_hbm, sv_hbm), (ckbuf, cvbuf, skbuf, svbuf),
                       (sem_ck, sem_cv, sem_sk, sem_sv), pt_ref, n_pages)
    cpp = ckbuf.shape[2]
    n_chunks = n_pages * cpp
    width = ckbuf.shape[3]
    qn = qn_ref[0] * SCALE
    qr = qr_ref[0] * SCALE
    ckc = _compress_rows(ckbuf[slot].reshape(n_chunks, width), kpe_ref, kw1_ref, kw2_ref)
    cvc = _compress_rows(cvbuf[slot].reshape(n_chunks, width), vpe_ref, vw1_ref, vw2_ref)
    n_idx = lax.broadcasted_iota(jnp.int32, (NSA_HEADS, n_chunks), 1)
    cmask = n_idx < n_cmp
    s = jnp.where(cmask, _dot_nt(qn.astype(BF16), ckc.astype(BF16)), NEG_INF)
    m = jnp.max(s, axis=1, keepdims=True)
    e = jnp.where(cmask, jnp.exp(s - m), 0.0)
    p = e / jnp.maximum(jnp.sum(e, axis=1, keepdims=True), 1e-30)
    o_c = _dot(p.astype(BF16), cvc.astype(BF16))
    p_kv = jnp.concatenate([jnp.sum(p[k * NSA_GROUP:(k + 1) * NSA_GROUP], axis=0, keepdims=True)
                            for k in range(NSA_KV_HEADS)], axis=0)
    nsb_pad = msel_ref.shape[1]
    score = jnp.dot(p_kv, msel_ref[...], preferred_element_type=F32, precision=lax.Precision.HIGHEST)
    jb = lax.broadcasted_iota(jnp.int32, (NSA_KV_HEADS, nsb_pad), 1)
    cur = n_sel - 1
    valid = jb < n_sel
    forced = (jb == 0) | (jb == cur) | (jb == cur - 1)
    score = jnp.where(valid, jnp.where(forced, jnp.inf, score), NEG_INF)
    sel = (_rank(score, n_sel) < SEL_TOPN) & valid
    sel_f = jnp.where(sel, 1.0, 0.0)
    row = lax.broadcasted_iota(jnp.int32, (NSA_HEADS, nsb_pad), 0)
    sel8 = jnp.where(row // NSA_GROUP == 0, sel_f[0:1], sel_f[1:2]).astype(BF16)
    selx = _dot(sel8[:, :exp_ref.shape[0]], exp_ref[...]) > 0.5
    o_s = _decode_attend(qr, skbuf, svbuf, slot, selx, sknew_ref[0], svnew_ref[0], sc_ref, page)
    wk, wv = wk_ref[0], wv_ref[0]
    w_buf = wk.shape[0]
    qb = qr.astype(BF16)
    widx = lax.broadcasted_iota(jnp.int32, (NSA_HEADS, w_buf), 1)
    sw = jnp.where(widx > w_buf - WINDOW, _dot_nt(qb, wk.astype(BF16)), NEG_INF)
    wk_new, wv_new = wknew_ref[0], wvnew_ref[0]
    sw_new = jnp.sum(qr * wk_new, axis=1, keepdims=True)
    mw = jnp.maximum(jnp.max(sw, axis=1, keepdims=True), sw_new)
    ew = jnp.exp(sw - mw)
    ew_new = jnp.exp(sw_new - mw)
    invw = 1.0 / jnp.maximum(jnp.sum(ew, axis=1, keepdims=True) + ew_new, 1e-30)
    o_w = _dot((ew * invw).astype(BF16), wv.astype(BF16)) + (ew_new * invw) * wv_new
    g = gate_ref[0]
    o = g[:, 0:1] * o_c + g[:, 1:2] * o_s + g[:, 2:3] * o_w
    o_ref[0] = _pick_heads(o, NSA_HEADS, NSA_KV_HEADS)
    ridx = lax.broadcasted_iota(jnp.int32, wk.shape, 0)
    wko_ref[0] = jnp.where(ridx == w_buf - 1, wk_new, pltpu.roll(wk, w_buf - 1, 0))
    wvo_ref[0] = jnp.where(ridx == w_buf - 1, wv_new, pltpu.roll(wv, w_buf - 1, 0))


def _nsa_sample(page_table, qn, qr, gates, sk_new, sv_new, wk_new, wv_new, state_wk, state_wv, kw, vw,
                cache_ck, cache_cv, cache_sk, cache_sv):
    db, n_pages = page_table.shape
    n_pool, page = cache_ck.shape[0], cache_ck.shape[1]
    n_past = n_pages * page
    cpp = page // CMP_STRIDE
    width = CMP_STRIDE * D_NSA_KV
    n_chunks = n_past // CMP_STRIDE
    n_cmp = n_chunks - CMP_LEN // CMP_STRIDE + 1
    n_sel = -(-(n_past + 1) // SEL_LEN)
    n_sel_past = n_past // SEL_LEN
    nsb_pad = -(-n_sel // LANE) * LANE
    w_buf = state_wk.shape[1]
    ck = cache_ck.reshape(n_pool, cpp, width)
    cv = cache_cv.reshape(n_pool, cpp, width)
    sk = cache_sk.reshape(n_pool, page, D_NSA_KV)
    sv = cache_sv.reshape(n_pool, page, D_NSA_KV)
    wk = state_wk.reshape(db, w_buf, D_NSA_KV)
    wv = state_wv.reshape(db, w_buf, D_NSA_KV)
    msel = jnp.asarray(_cmp_to_sel(n_chunks, n_cmp, nsb_pad, n_sel))
    expand = jnp.asarray(_block_expand(n_sel_past, n_sel_past, SEL_LEN, n_past)[0]).astype(BF16)
    per_b = lambda shape: pl.BlockSpec((1,) + shape, lambda b, pt: (b, 0, 0))
    const = lambda a: pl.BlockSpec(a.shape, (lambda b, pt: (0, 0, 0)) if a.ndim == 3 else (lambda b, pt: (0, 0)))
    any_spec = pl.BlockSpec(memory_space=pl.ANY)
    grid_spec = pltpu.PrefetchScalarGridSpec(
        num_scalar_prefetch=1,
        grid=(db,),
        in_specs=[per_b((NSA_HEADS, D_NSA_KV)), per_b((NSA_HEADS, D_NSA_KV)), per_b((NSA_HEADS, 3)),
                  per_b((1, D_NSA_KV)), per_b((1, D_NSA_KV)), per_b((1, D_NSA_KV)), per_b((1, D_NSA_KV)),
                  per_b((w_buf, D_NSA_KV)), per_b((w_buf, D_NSA_KV))]
        + [const(a) for a in kw + vw] + [const(msel), const(expand)] + [any_spec] * 4,
        out_specs=[per_b((NSA_HEADS, HEAD_DIM)), per_b((w_buf, D_NSA_KV)), per_b((w_buf, D_NSA_KV))],
        scratch_shapes=[pltpu.VMEM((2, n_pages, cpp, width), F32),
                        pltpu.VMEM((2, n_pages, cpp, width), F32),
                        pltpu.VMEM((2, n_pages, page, D_NSA_KV), F32),
                        pltpu.VMEM((2, n_pages, page, D_NSA_KV), F32),
                        pltpu.VMEM((NSA_HEADS, n_past), F32)] + [pltpu.SemaphoreType.DMA((2,))] * 4)
    return pl.pallas_call(
        functools.partial(_nsa_sample_kernel, n_cmp=n_cmp, n_sel=n_sel),
        grid_spec=grid_spec,
        out_shape=[jax.ShapeDtypeStruct((db, NSA_HEADS, HEAD_DIM), F32),
                   jax.ShapeDtypeStruct((db, w_buf, D_NSA_KV), F32),
                   jax.ShapeDtypeStruct((db, w_buf, D_NSA_KV), F32)],
        compiler_params=pltpu.CompilerParams(dimension_semantics=("arbitrary",), vmem_limit_bytes=VMEM_LIMIT),
        name="nsa_sample",
    )(page_table, qn, qr, gates, sk_new, sv_new, wk_new, wv_new, wk, wv, *kw, *vw, msel, expand, ck, cv, sk, sv)


def _expand_heads(q, n_heads, n_kv):
    b = q.shape[0]
    group = n_heads // n_kv
    place = jnp.asarray((np.arange(n_heads)[:, None] // group == np.arange(n_kv)[None, :]).astype(np.float32))
    q4 = q.reshape(b, n_heads, 1, HEAD_DIM) * place[None, :, :, None]
    return q4.reshape(b, n_heads, n_kv * HEAD_DIM)


def _ffn_weights(w_in, w_out):
    d, f2 = w_in.shape
    f = f2 // 2
    n_chunks = f // FFN_CHUNK
    wa = w_in[:, :f].reshape(d, n_chunks, FFN_CHUNK).transpose(1, 0, 2).astype(BF16)
    wb = w_in[:, f:].reshape(d, n_chunks, FFN_CHUNK).transpose(1, 0, 2).astype(BF16)
    return wa, wb, w_out.astype(BF16)


def kernel(x_prompt, x_sample, cache_moba_k, cache_moba_v, cache_nsa_cmp_k, cache_nsa_cmp_v, cache_nsa_sel_k, cache_nsa_sel_v, state_nsa_win_k, state_nsa_win_v, page_table, c_prompt, c_sample, w_ada, b_ada, norm_ffn1, w_ffn1_in, w_ffn1_out, norm_mix, w_mix_in, w_mix_out, norm_ffn2, w_ffn2_in, w_ffn2_out, cmp_k_pe, cmp_k_w1, cmp_k_w2, cmp_v_pe, cmp_v_w1, cmp_v_w2, norm_final):
    depth = w_ada.shape[0]
    assert depth == 1, "single-layer step"
    b, s, d = x_prompt.shape
    db = x_sample.shape[0]
    assert x_sample.shape[1] == 1
    n_pages = page_table.shape[1]
    page = cache_moba_k.shape[2]
    n_past = n_pages * page
    assert s % Q_TILE == 0 and n_past % MOBA_BLOCK == 0 and db % 8 == 0
    l = 0

    ffn1_w = _ffn_weights(w_ffn1_in[l], w_ffn1_out[l])
    ffn2_w = _ffn_weights(w_ffn2_in[l], w_ffn2_out[l])
    w_proj = jnp.pad(w_mix_in[l], ((0, 0), (0, PROJ_PAD - w_mix_in.shape[2]))).astype(BF16)
    w_mo = w_mix_out[l].astype(BF16)
    w_mo_m, w_mo_n = w_mo[:D_MOBA_Q], w_mo[D_MOBA_Q:]
    kw = _compress_weights(cmp_k_pe[l], cmp_k_w1[l], cmp_k_w2[l])
    vw = _compress_weights(cmp_v_pe[l], cmp_v_w1[l], cmp_v_w2[l])

    mods = _ada_mods(jnp.concatenate([c_sample, c_prompt], axis=0), w_ada[l], b_ada[l])
    mods_p = _Mods(mods, db, per_row=False)
    mods_s = _Mods(mods, db, per_row=True)
    xs = x_sample.reshape(1, db, d)
    tm_p = 512 if s % 512 == 0 else Q_TILE

    xp1 = _ffn(x_prompt, mods_p, (0, 1, 2), norm_ffn1[l], *ffn1_w, tm=tm_p)
    xs1 = _ffn(xs, mods_s, (0, 1, 2), norm_ffn1[l], *ffn1_w, tm=db)

    tabs_p = _rope_tables(jnp.arange(s, dtype=jnp.int32))
    pp = _proj(xp1, mods_p, (3, 4), norm_mix[l], w_proj, tabs_p, tm=tm_p, head_major=True)
    (mq, mk, mv, ck, cv, sk, sv, wk, wv, gates,
     mqh, mkh, mvh, nqnh, nqh, skh, svh, wkh, wvh) = pp
    om = _moba_prompt(mq, mqh, mk, mkh, mvh)
    ckc, cvc = _compress_prompt(ck, cv, kw, vw)
    on = _nsa_prompt(nqnh, nqh, gates, ckc, cvc, skh, svh, wkh, wvh)

    tabs_s = _rope_tables(jnp.full((db,), n_past, dtype=jnp.int32))
    ps = _proj(xs1, mods_s, (3, 4), norm_mix[l], w_proj, tabs_s, tm=db, head_major=False)
    (smq, smk, smv, sck, scv, ssk, ssv, swk, swv, sgates, snqn, snq) = [a[0] for a in ps]
    row3 = lambda a: a.reshape(db, 1, a.shape[-1])
    o_m_s = _moba_sample(page_table, _expand_heads(smq, MOBA_HEADS, MOBA_KV_HEADS), row3(smk), row3(smv),
                         cache_moba_k[l], cache_moba_v[l])
    o_n_s, win_k_s, win_v_s = _nsa_sample(
        page_table, _expand_heads(snqn, NSA_HEADS, NSA_KV_HEADS), _expand_heads(snq, NSA_HEADS, NSA_KV_HEADS),
        sgates[:, :N_GATES].reshape(db, NSA_HEADS, 3), row3(ssk), row3(ssv), row3(swk), row3(swv),
        state_nsa_win_k[l], state_nsa_win_v[l], kw, vw,
        cache_nsa_cmp_k[l], cache_nsa_cmp_v[l], cache_nsa_sel_k[l], cache_nsa_sel_v[l])
    om_s = o_m_s.reshape(1, db, D_MOBA_Q).astype(BF16)
    on_s = o_n_s.reshape(1, db, D_NSA_Q).astype(BF16)

    yp = _ffn(xp1, mods_p, (6, 7, 8), norm_ffn2[l], *ffn2_w, tm=tm_p,
              mix=(om, on, w_mo_m, w_mo_n, 5), final_g=norm_final)
    ys = _ffn(xs1, mods_s, (6, 7, 8), norm_ffn2[l], *ffn2_w, tm=db,
              mix=(om_s, on_s, w_mo_m, w_mo_n, 5), final_g=norm_final)

    w_keep = min(WINDOW, s)
    st = lambda a, n: a.reshape(1, b, a.shape[1], n, HEAD_DIM)
    ss_ = lambda a, n: a.reshape(1, db, 1, n, HEAD_DIM)
    w_buf = state_nsa_win_k.shape[2]
    return (yp, ys.reshape(db, 1, d),
            st(mk, MOBA_KV_HEADS), ss_(smk, MOBA_KV_HEADS), st(mv, MOBA_KV_HEADS), ss_(smv, MOBA_KV_HEADS),
            st(ck, NSA_KV_HEADS), ss_(sck, NSA_KV_HEADS), st(cv, NSA_KV_HEADS), ss_(scv, NSA_KV_HEADS),
            st(sk, NSA_KV_HEADS), ss_(ssk, NSA_KV_HEADS), st(sv, NSA_KV_HEADS), ss_(ssv, NSA_KV_HEADS),
            st(wk[:, s - w_keep:], NSA_KV_HEADS), win_k_s.reshape(1, db, w_buf, NSA_KV_HEADS, HEAD_DIM),
            st(wv[:, s - w_keep:], NSA_KV_HEADS), win_v_s.reshape(1, db, w_buf, NSA_KV_HEADS, HEAD_DIM))
```

```python
import functools

import numpy as np
import jax
import jax.numpy as jnp
from jax import lax
from jax.experimental import pallas as pl
from jax.experimental.pallas import tpu as pltpu

F32 = jnp.float32
BF16 = jnp.bfloat16
NEG_INF = float("-inf")
HIGHEST = lax.Precision.HIGHEST

HEAD_DIM = 64
MOBA_HEADS = 8
MOBA_KV_HEADS = 4
MOBA_GROUP = MOBA_HEADS // MOBA_KV_HEADS
NSA_HEADS = 8
NSA_KV_HEADS = 2
NSA_GROUP = NSA_HEADS // NSA_KV_HEADS
ROPE_DIM = HEAD_DIM // 4
ROPE_THETA = 500000.0
MOBA_BLOCK = 256
MOBA_TOPK = 3
CMP_LEN = 32
CMP_STRIDE = 16
CMP_HIDDEN = 2 * HEAD_DIM
SEL_LEN = 64
SEL_TOPN = 16
WINDOW = 512
MACARON_WEIGHT = 0.5
N_MOD = 9
EPS = 1e-6
SCALE = HEAD_DIM ** -0.5
D_MOBA_Q = MOBA_HEADS * HEAD_DIM
D_MOBA_KV = MOBA_KV_HEADS * HEAD_DIM
D_NSA_Q = NSA_HEADS * HEAD_DIM
D_NSA_KV = NSA_KV_HEADS * HEAD_DIM
N_GATES = 3 * NSA_HEADS
LANE = 128
PROJ_PAD = 2432
VMEM_LIMIT = 56 * 1024 * 1024
Q_TILE = 256
FFN_CHUNK = 256

OFF_MQ, OFF_MK, OFF_MV, OFF_NQ = 0, 512, 768, 1024
OFF_CK, OFF_CV, OFF_SK, OFF_SV, OFF_WK, OFF_WV, OFF_G = 1536, 1664, 1792, 1920, 2048, 2176, 2304


def _silu(x):
    return x / (1.0 + jnp.exp(-x))


def _sigmoid(x):
    return 1.0 / (1.0 + jnp.exp(-x))


def _dot(a, b):
    return jnp.dot(a, b, preferred_element_type=F32)


def _dot_nt(a, b, precision=None):
    return lax.dot_general(a, b, (((1,), (1,)), ((), ())), preferred_element_type=F32, precision=precision)


def _modulated(x, g, shift, scale):
    ms = jnp.mean(x * x, axis=-1, keepdims=True)
    return (x * lax.rsqrt(ms + EPS) * g) * (1.0 + scale) + shift


def _rank(sc, ncols, axis):
    idx = lax.broadcasted_iota(jnp.int32, sc.shape, axis)
    rank = jnp.zeros(sc.shape, F32)
    for i in range(ncols):
        ci = sc[:, i:i + 1] if axis == 1 else sc[i:i + 1, :]
        beats = (ci > sc) | ((ci == sc) & (idx > i))
        rank = rank + jnp.where(beats, 1.0, 0.0)
    return rank


def _ada_kernel(c_ref, w_ref, b_ref, o_ref):
    sc = _silu(c_ref[...]).astype(BF16)
    o_ref[0] = _dot(sc, w_ref[...].astype(BF16)) + b_ref[0]


def _ada_mods(c_all, w_ada, b_ada):
    rows, d = c_all.shape
    return pl.pallas_call(
        _ada_kernel,
        grid=(N_MOD,),
        in_specs=[pl.BlockSpec((rows, d), lambda j: (0, 0)),
                  pl.BlockSpec((d, d), lambda j: (0, j)),
                  pl.BlockSpec((1, 1, d), lambda j: (j, 0, 0))],
        out_specs=pl.BlockSpec((1, rows, d), lambda j: (j, 0, 0)),
        out_shape=jax.ShapeDtypeStruct((N_MOD, rows, d), F32),
        compiler_params=pltpu.CompilerParams(dimension_semantics=("arbitrary",), vmem_limit_bytes=VMEM_LIMIT),
        name="ada_mods",
    )(c_all, w_ada, b_ada.reshape(N_MOD, 1, d))


class _Mods:
    def __init__(self, mods, n_sample, per_row):
        self.per_row = per_row
        self.n_sample = n_sample
        self.rows = mods.shape[1]
        d = mods.shape[2]
        self.d = d
        self.arr = mods if per_row else mods.reshape(N_MOD * self.rows, 1, d)

    def spec(self, j):
        if self.per_row:
            return pl.BlockSpec((1, self.n_sample, self.d), lambda g, i: (j, 0, 0))
        base = j * self.rows + self.n_sample
        return pl.BlockSpec((1, 1, self.d), lambda g, i: (base + g, 0, 0))


def _ffn_kernel(*refs, has_mix, final_norm, n_chunks):
    it = iter(refs)
    x_ref = next(it)
    if has_mix:
        mm_ref, mn_ref, wmm_ref, wmn_ref, gmix_ref = next(it), next(it), next(it), next(it), next(it)
    sh_ref, sc_ref, gt_ref, g_ref, wa_ref, wb_ref, wo_ref = (next(it) for _ in range(7))
    gf_ref = next(it) if final_norm else None
    o_ref, h_ref = next(it), next(it)

    x = x_ref[0]
    if has_mix:
        x = x + gmix_ref[0] * (_dot(mm_ref[0], wmm_ref[...]) + _dot(mn_ref[0], wmn_ref[...]))
    xm = _modulated(x, g_ref[...], sh_ref[0], sc_ref[0]).astype(BF16)
    tf = wa_ref.shape[2]
    for j in range(n_chunks):
        a = _dot(xm, wa_ref[j])
        b = _dot(xm, wb_ref[j])
        h_ref[:, j * tf:(j + 1) * tf] = (_silu(a) * b).astype(BF16)
    y = x + (MACARON_WEIGHT * gt_ref[0]) * _dot(h_ref[...], wo_ref[...])
    if final_norm:
        ms = jnp.mean(y * y, axis=-1, keepdims=True)
        y = y * lax.rsqrt(ms + EPS) * gf_ref[...]
    o_ref[0] = y


def _ffn(x, mods, jmods, norm_g, wa, wb, wo, tm, mix=None, final_g=None):
    g_, r_, d = x.shape
    n_chunks, _, tf = wa.shape
    f = n_chunks * tf
    const2 = lambda g, i: (0, 0)
    const3 = lambda g, i: (0, 0, 0)
    row_spec = lambda w: pl.BlockSpec((1, tm, w), lambda g, i: (g, i, 0))
    args, specs = [x], [row_spec(d)]
    if mix is not None:
        mm, mn, wmm, wmn, jmix = mix
        args += [mm, mn, wmm, wmn, mods.arr]
        specs += [row_spec(mm.shape[2]), row_spec(mn.shape[2]),
                  pl.BlockSpec(wmm.shape, const2), pl.BlockSpec(wmn.shape, const2), mods.spec(jmix)]
    args += [mods.arr, mods.arr, mods.arr, norm_g.reshape(1, d), wa, wb, wo]
    specs += [mods.spec(jmods[0]), mods.spec(jmods[1]), mods.spec(jmods[2]),
              pl.BlockSpec((1, d), const2),
              pl.BlockSpec(wa.shape, const3), pl.BlockSpec(wb.shape, const3), pl.BlockSpec(wo.shape, const2)]
    if final_g is not None:
        args.append(final_g.reshape(1, d))
        specs.append(pl.BlockSpec((1, d), const2))
    kern = functools.partial(_ffn_kernel, has_mix=mix is not None, final_norm=final_g is not None, n_chunks=n_chunks)
    return pl.pallas_call(
        kern,
        grid=(g_, r_ // tm),
        in_specs=specs,
        out_specs=row_spec(d),
        out_shape=jax.ShapeDtypeStruct((g_, r_, d), F32),
        scratch_shapes=[pltpu.VMEM((tm, f), BF16)],
        compiler_params=pltpu.CompilerParams(dimension_semantics=("arbitrary", "arbitrary"),
                                             vmem_limit_bytes=VMEM_LIMIT),
        name="ffn",
    )(*args)


def _rope_tables(pos):
    half = ROPE_DIM // 2
    inv_freq = ROPE_THETA ** (-jnp.arange(half, dtype=F32) / half)
    ang = pos.astype(F32)[:, None] * inv_freq
    cos, sin = jnp.cos(ang), jnp.sin(ang)
    rows = pos.shape[0]
    rest = HEAD_DIM - ROPE_DIM
    one, zero = jnp.ones((rows, rest), F32), jnp.zeros((rows, rest), F32)
    zh = jnp.zeros((rows, half), F32)
    c = jnp.concatenate([cos, cos, one], axis=1)
    sa = jnp.concatenate([-sin, zh, zero], axis=1)
    sb = jnp.concatenate([zh, sin, zero], axis=1)
    tile = lambda t: jnp.concatenate([t, t], axis=1)
    return tile(c), tile(sa), tile(sb)


def _proj_kernel(*refs, head_major):
    (x_ref, sh_ref, sc_ref, g_ref, w_ref, cos_ref, sa_ref, sb_ref) = refs[:8]
    outs = refs[8:]
    (mq_ref, mk_ref, mv_ref, ck_ref, cv_ref, sk_ref, sv_ref, wk_ref, wv_ref, gate_ref) = outs[:10]
    xm = _modulated(x_ref[0], g_ref[...], sh_ref[0], sc_ref[0]).astype(BF16)
    cos, sa, sb = cos_ref[...], sa_ref[...], sb_ref[...]
    half = ROPE_DIM // 2

    def seg(lo, width):
        return _dot(xm, w_ref[:, lo:lo + width])

    def rope(y):
        parts = []
        for c in range(y.shape[1] // LANE):
            s = y[:, c * LANE:(c + 1) * LANE]
            parts.append(s * cos + pltpu.roll(s, LANE - half, 1) * sa + pltpu.roll(s, half, 1) * sb)
        return parts[0] if len(parts) == 1 else jnp.concatenate(parts, axis=1)

    mq = rope(seg(OFF_MQ, D_MOBA_Q))
    mk = rope(seg(OFF_MK, D_MOBA_KV))
    mv = seg(OFF_MV, D_MOBA_KV)
    nqn = seg(OFF_NQ, D_NSA_Q)
    nq = rope(nqn)
    ck = seg(OFF_CK, D_NSA_KV)
    cv = seg(OFF_CV, D_NSA_KV)
    sk = rope(seg(OFF_SK, D_NSA_KV))
    sv = seg(OFF_SV, D_NSA_KV)
    wk = rope(seg(OFF_WK, D_NSA_KV))
    wv = seg(OFF_WV, D_NSA_KV)
    gate_ref[0] = _sigmoid(seg(OFF_G, LANE))
    mq_ref[0], mk_ref[0], mv_ref[0] = mq, mk, mv
    ck_ref[0], cv_ref[0], sk_ref[0], sv_ref[0], wk_ref[0], wv_ref[0] = ck, cv, sk, sv, wk, wv
    if not head_major:
        nqn_ref, nq_ref = outs[10:]
        nqn_ref[0], nq_ref[0] = nqn, nq
    else:
        (mqh_ref, mkh_ref, mvh_ref, nqnh_ref, nqh_ref, skh_ref, svh_ref, wkh_ref, wvh_ref) = outs[10:]

        def put(ref, val, n, scale):
            for h in range(n):
                piece = val[:, h * HEAD_DIM:(h + 1) * HEAD_DIM]
                ref[0, h] = (piece * scale if scale != 1.0 else piece).astype(BF16)

        put(mqh_ref, mq, MOBA_HEADS, SCALE)
        put(mkh_ref, mk, MOBA_KV_HEADS, 1.0)
        put(mvh_ref, mv, MOBA_KV_HEADS, 1.0)
        put(nqnh_ref, nqn, NSA_HEADS, SCALE)
        put(nqh_ref, nq, NSA_HEADS, SCALE)
        put(skh_ref, sk, NSA_KV_HEADS, 1.0)
        put(svh_ref, sv, NSA_KV_HEADS, 1.0)
        put(wkh_ref, wk, NSA_KV_HEADS, 1.0)
        put(wvh_ref, wv, NSA_KV_HEADS, 1.0)


def _proj(x, mods, jmods, norm_g, w_pad, tables, tm, head_major):
    g_, r_, d = x.shape
    const2 = lambda g, i: (0, 0)
    row_spec = lambda w: pl.BlockSpec((1, tm, w), lambda g, i: (g, i, 0))
    tab_spec = pl.BlockSpec((tm, LANE), lambda g, i: (i, 0))
    in_specs = [row_spec(d), mods.spec(jmods[0]), mods.spec(jmods[1]), pl.BlockSpec((1, d), const2),
                pl.BlockSpec(w_pad.shape, const2), tab_spec, tab_spec, tab_spec]
    widths = [D_MOBA_Q, D_MOBA_KV, D_MOBA_KV] + [D_NSA_KV] * 6 + [LANE]
    if not head_major:
        widths += [D_NSA_Q, D_NSA_Q]
    out_specs = [row_spec(w) for w in widths]
    out_shape = [jax.ShapeDtypeStruct((g_, r_, w), F32) for w in widths]
    if head_major:
        for n in (MOBA_HEADS, MOBA_KV_HEADS, MOBA_KV_HEADS, NSA_HEADS, NSA_HEADS) + (NSA_KV_HEADS,) * 4:
            out_specs.append(pl.BlockSpec((1, n, tm, HEAD_DIM), lambda g, i: (g, 0, i, 0)))
            out_shape.append(jax.ShapeDtypeStruct((g_, n, r_, HEAD_DIM), BF16))
    return pl.pallas_call(
        functools.partial(_proj_kernel, head_major=head_major),
        grid=(g_, r_ // tm),
        in_specs=in_specs,
        out_specs=out_specs,
        out_shape=out_shape,
        compiler_params=pltpu.CompilerParams(dimension_semantics=("arbitrary", "arbitrary"),
                                             vmem_limit_bytes=VMEM_LIMIT),
        name="proj",
    )(x, mods.arr, mods.arr, norm_g.reshape(1, d), w_pad, *tables)


def _compress_mlp(p0, p1, w2_ref):
    h = p0 + pltpu.roll(p1, p0.shape[0] - 1, 0)
    return _dot(_silu(h).astype(BF16), w2_ref[...])


def _compress_rows(x, pe_ref, w1_ref, w2_ref):
    p0 = _dot((x + pe_ref[0]).astype(BF16), w1_ref[0])
    p1 = _dot((x + pe_ref[1]).astype(BF16), w1_ref[1])
    return _compress_mlp(p0, p1, w2_ref)


def _compress_kernel(ck_ref, cv_ref, kpe_ref, kw1_ref, kw2_ref, vpe_ref, vw1_ref, vw2_ref, okc_ref, ovc_ref):
    okc_ref[0] = _compress_rows(ck_ref[0], kpe_ref, kw1_ref, kw2_ref)
    ovc_ref[0] = _compress_rows(cv_ref[0], vpe_ref, vw1_ref, vw2_ref)


def _compress_weights(pe, w1, w2):
    ratio = CMP_LEN // CMP_STRIDE
    eye = jnp.eye(NSA_KV_HEADS, dtype=F32)
    w1r = w1.reshape(ratio, CMP_STRIDE, HEAD_DIM, CMP_HIDDEN)
    w1f = jnp.einsum("rlde,kK->rlkdKe", w1r, eye).reshape(ratio, CMP_STRIDE * D_NSA_KV, NSA_KV_HEADS * CMP_HIDDEN)
    pef = jnp.broadcast_to(pe.reshape(ratio, CMP_STRIDE, 1, HEAD_DIM),
                           (ratio, CMP_STRIDE, NSA_KV_HEADS, HEAD_DIM)).reshape(ratio, 1, CMP_STRIDE * D_NSA_KV)
    w2f = jnp.einsum("ed,kK->keKd", w2, eye).reshape(NSA_KV_HEADS * CMP_HIDDEN, D_NSA_KV)
    return pef, w1f, w2f.astype(BF16)


def _compress_prompt(ck, cv, kw, vw):
    b, s, _ = ck.shape
    n = s // CMP_STRIDE
    width = CMP_STRIDE * D_NSA_KV
    ck2, cv2 = ck.reshape(b, n, width), cv.reshape(b, n, width)
    row = pl.BlockSpec((1, n, width), lambda i: (i, 0, 0))
    wspecs = []
    for w in kw + vw:
        wspecs.append(pl.BlockSpec(w.shape, (lambda i: (0, 0, 0)) if w.ndim == 3 else (lambda i: (0, 0))))
    out = pl.BlockSpec((1, n, D_NSA_KV), lambda i: (i, 0, 0))
    return pl.pallas_call(
        _compress_kernel,
        grid=(b,),
        in_specs=[row, row] + wspecs,
        out_specs=[out, out],
        out_shape=[jax.ShapeDtypeStruct((b, n, D_NSA_KV), F32)] * 2,
        compiler_params=pltpu.CompilerParams(dimension_semantics=("arbitrary",), vmem_limit_bytes=VMEM_LIMIT),
        name="compress_prompt",
    )(ck2, cv2, *kw, *vw)


def _two_pass_init(mx_ref, l_ref, acc_ref):
    mx_ref[...] = jnp.full(mx_ref.shape, NEG_INF, F32)
    l_ref[...] = jnp.zeros(l_ref.shape, F32)
    acc_ref[...] = jnp.zeros(acc_ref.shape, F32)


def _pass1(j, s, s_ref, mx_ref):
    s_ref[j] = s
    m = s[:, 0:LANE]
    for c in range(1, s.shape[1] // LANE):
        m = jnp.maximum(m, s[:, c * LANE:(c + 1) * LANE])
    mx_ref[...] = jnp.maximum(mx_ref[...], m)


def _row_max(mx_ref, mb_ref):
    m = jnp.max(mx_ref[...], axis=1, keepdims=True)
    m = jnp.where(m == NEG_INF, 0.0, m)
    mb_ref[...] = jnp.broadcast_to(m, mb_ref.shape)


def _pass2(j, v, s_ref, mb_ref, l_ref, acc_ref):
    mb = mb_ref[...]
    s = s_ref[j]
    parts = [jnp.exp(s[:, c * LANE:(c + 1) * LANE] - mb) for c in range(s.shape[1] // LANE)]
    tot = parts[0]
    for p in parts[1:]:
        tot = tot + p
    l_ref[...] += tot
    acc_ref[...] += _dot(jnp.concatenate(parts, axis=1).astype(BF16), v)


def _two_pass_out(l_ref, acc_ref):
    l = jnp.sum(l_ref[...], axis=1, keepdims=True)
    return acc_ref[...] / jnp.maximum(l, 1e-30)


def _add_bias(s, bias, groups):
    tq = bias.shape[0]
    return (s.reshape(groups, tq, s.shape[1]) + bias[None]).reshape(s.shape)


def _attn_scratch(rows, n_chunks, chunk):
    return [pltpu.VMEM((rows, HEAD_DIM), BF16),
            pltpu.VMEM((n_chunks, rows, chunk), F32),
            pltpu.VMEM((rows, LANE), F32),
            pltpu.VMEM((rows, LANE), F32),
            pltpu.VMEM((rows, LANE), F32),
            pltpu.VMEM((rows, HEAD_DIM), F32)]


def _moba_prompt_kernel(mq_ref, mqh_ref, mk_ref, mkh_ref, mvh_ref, o_ref,
                        kmean_ref, bias_ref, q_ref, s_ref, mx_ref, mb_ref, l_ref, acc_ref, *, n_blocks):
    qi = pl.program_id(1)
    tq = mq_ref.shape[1]
    rows = MOBA_GROUP * tq

    @pl.when(qi == 0)
    def _():
        for j in range(n_blocks):
            blk = mk_ref[0, j * MOBA_BLOCK:(j + 1) * MOBA_BLOCK, :]
            kmean_ref[j:j + 1, :] = jnp.sum(blk, axis=0, keepdims=True) * (1.0 / MOBA_BLOCK)

    blk_t = lax.broadcasted_iota(jnp.int32, (n_blocks, tq), 0)
    r_loc = lax.broadcasted_iota(jnp.int32, (rows, MOBA_BLOCK), 0) % tq
    c_loc = lax.broadcasted_iota(jnp.int32, (rows, MOBA_BLOCK), 1)
    pad = jnp.full((LANE - n_blocks, tq), NEG_INF, F32)

    pieces = []
    for k in range(MOBA_KV_HEADS):
        km = kmean_ref[:, k * HEAD_DIM:(k + 1) * HEAD_DIM]
        for g in range(MOBA_GROUP):
            h = k * MOBA_GROUP + g
            qf = mq_ref[0, :, h * HEAD_DIM:(h + 1) * HEAD_DIM]
            sb = jnp.where(blk_t < qi, _dot_nt(km, qf, precision=HIGHEST), NEG_INF)
            sel = (_rank(sb, n_blocks, 0) < MOBA_TOPK) & (blk_t < qi)
            bias_t = jnp.concatenate([jnp.where(sel, 0.0, NEG_INF), pad], axis=0)
            bias_ref[g * tq:(g + 1) * tq, :] = bias_t.T
            q_ref[g * tq:(g + 1) * tq, :] = mqh_ref[0, h]
        _two_pass_init(mx_ref, l_ref, acc_ref)
        for j in range(n_blocks):
            kc = mkh_ref.at[0, k, j * MOBA_BLOCK:(j + 1) * MOBA_BLOCK, :]

            @pl.when(j < qi)
            def _():
                _pass1(j, _dot_nt(q_ref[...], kc[...]) + bias_ref[:, j:j + 1], s_ref, mx_ref)

            @pl.when(j == qi)
            def _():
                _pass1(j, jnp.where(c_loc <= r_loc, _dot_nt(q_ref[...], kc[...]), NEG_INF), s_ref, mx_ref)

        _row_max(mx_ref, mb_ref)
        for j in range(n_blocks):
            @pl.when(j <= qi)
            def _():
                _pass2(j, mvh_ref[0, k, j * MOBA_BLOCK:(j + 1) * MOBA_BLOCK, :], s_ref, mb_ref, l_ref, acc_ref)

        o = _two_pass_out(l_ref, acc_ref)
        pieces += [o[g * tq:(g + 1) * tq, :] for g in range(MOBA_GROUP)]
    o_ref[0] = jnp.concatenate(pieces, axis=1).astype(BF16)


def _moba_prompt(mq, mqh, mk, mkh, mvh):
    b, s, _ = mq.shape
    tq = Q_TILE
    assert tq == MOBA_BLOCK
    n_blocks = s // MOBA_BLOCK
    rows = MOBA_GROUP * tq
    full = lambda n: pl.BlockSpec((1, n, s, HEAD_DIM), lambda i, j: (i, 0, 0, 0))
    return pl.pallas_call(
        functools.partial(_moba_prompt_kernel, n_blocks=n_blocks),
        grid=(b, s // tq),
        in_specs=[pl.BlockSpec((1, tq, D_MOBA_Q), lambda i, j: (i, j, 0)),
                  pl.BlockSpec((1, MOBA_HEADS, tq, HEAD_DIM), lambda i, j: (i, 0, j, 0)),
                  pl.BlockSpec((1, s, D_MOBA_KV), lambda i, j: (i, 0, 0)),
                  full(MOBA_KV_HEADS), full(MOBA_KV_HEADS)],
        out_specs=pl.BlockSpec((1, tq, D_MOBA_Q), lambda i, j: (i, j, 0)),
        out_shape=jax.ShapeDtypeStruct((b, s, D_MOBA_Q), BF16),
        scratch_shapes=[pltpu.VMEM((n_blocks, D_MOBA_KV), F32),
                        pltpu.VMEM((rows, LANE), F32)] + _attn_scratch(rows, n_blocks, MOBA_BLOCK),
        compiler_params=pltpu.CompilerParams(dimension_semantics=("arbitrary", "arbitrary"),
                                             vmem_limit_bytes=VMEM_LIMIT),
        name="moba_prompt",
    )(mq, mqh, mk, mkh, mvh)


def _cmp_to_sel(nc_pad, nc, nsb_pad, nsb):
    i = np.arange(nc_pad)[:, None]
    j = np.arange(nsb_pad)[None, :]
    start = i * CMP_STRIDE
    m = (start < (j + 1) * SEL_LEN) & (start + CMP_LEN > j * SEL_LEN) & (i < nc) & (j < nsb)
    return m.astype(np.float32)


def _block_expand(n_blocks_pad, n_blocks, block_len, chunk):
    n_chunks = n_blocks * block_len // chunk
    pos = np.arange(n_chunks * chunk).reshape(n_chunks, 1, chunk)
    j = np.arange(n_blocks_pad).reshape(1, n_blocks_pad, 1)
    return (pos // block_len == j).astype(np.float32)


def _nsa_prompt_kernel(nqnh_ref, nqh_ref, gate_ref, ckc_ref, cvc_ref, skh_ref, svh_ref, wkh_ref, wvh_ref, mselt_ref, exp_ref,
                       o_ref, sel_ref, q_ref, s_ref, mx_ref, mb_ref, l_ref, acc_ref, *, n_cmp, n_sel):
    qi = pl.program_id(1)
    tq = nqnh_ref.shape[2]
    rows = NSA_GROUP * tq
    nc_pad = ckc_ref.shape[1]
    nsb_pad = mselt_ref.shape[0]
    n_chunks, _, chunk = exp_ref.shape
    q0 = qi * tq
    gates = gate_ref[0]

    qpos_c = q0 + lax.broadcasted_iota(jnp.int32, (rows, nc_pad), 0) % tq
    n_idx = lax.broadcasted_iota(jnp.int32, (rows, nc_pad), 1)
    cmp_mask = (n_idx * CMP_STRIDE + (CMP_LEN - 1) <= qpos_c) & (n_idx < n_cmp)

    t = q0 + lax.broadcasted_iota(jnp.int32, (nsb_pad, tq), 1)
    jb = lax.broadcasted_iota(jnp.int32, (nsb_pad, tq), 0)
    cur = t // SEL_LEN
    valid = (jb * SEL_LEN <= t) & (jb < n_sel)
    forced = (jb == 0) | (jb == cur) | (jb == cur - 1)
    pad = jnp.zeros((LANE - nsb_pad, tq), F32)

    r_loc = lax.broadcasted_iota(jnp.int32, (tq, chunk), 0)
    c_loc = lax.broadcasted_iota(jnp.int32, (tq, chunk), 1)
    lower = jnp.where(c_loc <= r_loc, 0.0, NEG_INF)
    upper = jnp.where(c_loc > r_loc, 0.0, NEG_INF)

    pieces = []
    for k in range(NSA_KV_HEADS):
        lo = k * HEAD_DIM
        for g in range(NSA_GROUP):
            q_ref[g * tq:(g + 1) * tq, :] = nqnh_ref[0, k * NSA_GROUP + g]
        kc = ckc_ref[0, :, lo:lo + HEAD_DIM].astype(BF16)
        vc = cvc_ref[0, :, lo:lo + HEAD_DIM].astype(BF16)
        s = jnp.where(cmp_mask, _dot_nt(q_ref[...], kc), NEG_INF)
        m = jnp.max(s, axis=1, keepdims=True)
        m = jnp.where(m == NEG_INF, 0.0, m)
        e = jnp.where(cmp_mask, jnp.exp(s - m), 0.0)
        p = e / jnp.maximum(jnp.sum(e, axis=1, keepdims=True), 1e-30)
        o_c = _dot(p.astype(BF16), vc)
        p_kv = p[0:tq]
        for g in range(1, NSA_GROUP):
            p_kv = p_kv + p[g * tq:(g + 1) * tq]
        score = _dot_nt(mselt_ref[...], p_kv, precision=HIGHEST)
        score = jnp.where(valid, jnp.where(forced, jnp.inf, score), NEG_INF)
        sel = (_rank(score, n_sel, 0) < SEL_TOPN) & valid
        sel_t = jnp.concatenate([jnp.where(sel, 1.0, 0.0), pad], axis=0)
        sel_ref[...] = sel_t.T[:, :nsb_pad].astype(BF16)
        for g in range(NSA_GROUP):
            q_ref[g * tq:(g + 1) * tq, :] = nqh_ref[0, k * NSA_GROUP + g]
        _two_pass_init(mx_ref, l_ref, acc_ref)
        for j in range(n_chunks):
            kj = skh_ref.at[0, k, j * chunk:(j + 1) * chunk, :]
            ej = exp_ref.at[j]

            @pl.when(j < qi)
            def _():
                bias = jnp.where(_dot(sel_ref[...], ej[...]) > 0.5, 0.0, NEG_INF)
                _pass1(j, _add_bias(_dot_nt(q_ref[...], kj[...]), bias, NSA_GROUP), s_ref, mx_ref)

            @pl.when(j == qi)
            def _():
                bias = jnp.where(_dot(sel_ref[...], ej[...]) > 0.5, lower, NEG_INF)
                _pass1(j, _add_bias(_dot_nt(q_ref[...], kj[...]), bias, NSA_GROUP), s_ref, mx_ref)

        _row_max(mx_ref, mb_ref)
        for j in range(n_chunks):
            @pl.when(j <= qi)
            def _():
                _pass2(j, svh_ref[0, k, j * chunk:(j + 1) * chunk, :], s_ref, mb_ref, l_ref, acc_ref)

        o_s = _two_pass_out(l_ref, acc_ref)
        _two_pass_init(mx_ref, l_ref, acc_ref)
        for j in range(n_chunks):
            kj = wkh_ref.at[0, k, j * chunk:(j + 1) * chunk, :]

            @pl.when(j == qi)
            def _():
                _pass1(j, _add_bias(_dot_nt(q_ref[...], kj[...]), lower, NSA_GROUP), s_ref, mx_ref)

            @pl.when(j == qi - 1)
            def _():
                _pass1(j, _dot_nt(q_ref[...], kj[...]), s_ref, mx_ref)

            @pl.when(j == qi - 2)
            def _():
                _pass1(j, _add_bias(_dot_nt(q_ref[...], kj[...]), upper, NSA_GROUP), s_ref, mx_ref)

        _row_max(mx_ref, mb_ref)
        for j in range(n_chunks):
            @pl.when((j <= qi) & (j >= qi - 2))
            def _():
                _pass2(j, wvh_ref[0, k, j * chunk:(j + 1) * chunk, :], s_ref, mb_ref, l_ref, acc_ref)

        o_w = _two_pass_out(l_ref, acc_ref)
        for g in range(NSA_GROUP):
            h = k * NSA_GROUP + g
            rs = slice(g * tq, (g + 1) * tq)
            pieces.append(gates[:, 3 * h:3 * h + 1] * o_c[rs] + gates[:, 3 * h + 1:3 * h + 2] * o_s[rs]
                          + gates[:, 3 * h + 2:3 * h + 3] * o_w[rs])
    o_ref[0] = jnp.concatenate(pieces, axis=1).astype(BF16)


def _nsa_prompt(nqnh, nqh, gates, ckc, cvc, skh, svh, wkh, wvh):
    b, _, s, _ = nqnh.shape
    tq = Q_TILE
    chunk = Q_TILE
    assert WINDOW == 2 * chunk
    n_chunks = s // CMP_STRIDE
    n_cmp = n_chunks - CMP_LEN // CMP_STRIDE + 1
    n_sel = -(-s // SEL_LEN)
    nsb_pad = -(-n_sel // 8) * 8
    mselt = jnp.asarray(_cmp_to_sel(n_chunks, n_cmp, nsb_pad, n_sel).T)
    expand = jnp.asarray(_block_expand(nsb_pad, n_sel, SEL_LEN, chunk)).astype(BF16)
    full = lambda n: pl.BlockSpec((1, n, s, HEAD_DIM), lambda i, j: (i, 0, 0, 0))
    qspec = pl.BlockSpec((1, NSA_HEADS, tq, HEAD_DIM), lambda i, j: (i, 0, j, 0))
    cspec = pl.BlockSpec((1, n_chunks, D_NSA_KV), lambda i, j: (i, 0, 0))
    rows = NSA_GROUP * tq
    return pl.pallas_call(
        functools.partial(_nsa_prompt_kernel, n_cmp=n_cmp, n_sel=n_sel),
        grid=(b, s // tq),
        in_specs=[qspec, qspec, pl.BlockSpec((1, tq, LANE), lambda i, j: (i, j, 0)), cspec, cspec,
                  full(NSA_KV_HEADS), full(NSA_KV_HEADS), full(NSA_KV_HEADS), full(NSA_KV_HEADS),
                  pl.BlockSpec(mselt.shape, lambda i, j: (0, 0)),
                  pl.BlockSpec(expand.shape, lambda i, j: (0, 0, 0))],
        out_specs=pl.BlockSpec((1, tq, D_NSA_Q), lambda i, j: (i, j, 0)),
        out_shape=jax.ShapeDtypeStruct((b, s, D_NSA_Q), BF16),
        scratch_shapes=[pltpu.VMEM((tq, nsb_pad), BF16)] + _attn_scratch(rows, s // chunk, chunk),
        compiler_params=pltpu.CompilerParams(dimension_semantics=("arbitrary", "arbitrary"),
                                             vmem_limit_bytes=VMEM_LIMIT),
        name="nsa_prompt",
    )(nqnh, nqh, gates, ckc, cvc, skh, svh, wkh, wvh, mselt, expand)


def _page_copy(cache_ref, buf_ref, sem_ref, pt_ref, b, slot, p):
    return pltpu.make_async_copy(cache_ref.at[pt_ref[b, p]], buf_ref.at[slot, p], sem_ref.at[slot])


def _pages_start(caches, bufs, sems, pt_ref, b, slot, n_pages):
    def body(p, carry):
        for c, bf, sm in zip(caches, bufs, sems):
            _page_copy(c, bf, sm, pt_ref, b, slot, p).start()
        return carry
    lax.fori_loop(0, n_pages, body, 0)


def _pages_wait(caches, bufs, sems, slot, n_pages):
    for c, bf, sm in zip(caches, bufs, sems):
        pltpu.make_async_copy(c.at[pl.ds(0, n_pages)], bf.at[slot], sm.at[slot]).wait()


def _paged_step(caches, bufs, sems, pt_ref, n_pages):
    b = pl.program_id(0)
    nb = pl.num_programs(0)
    slot = b % 2

    @pl.when(b == 0)
    def _():
        _pages_start(caches, bufs, sems, pt_ref, b, slot, n_pages)

    @pl.when(b + 1 < nb)
    def _():
        _pages_start(caches, bufs, sems, pt_ref, b + 1, 1 - slot, n_pages)

    _pages_wait(caches, bufs, sems, slot, n_pages)
    return slot


def _decode_scores(q_t, kbuf, slot, sc_ref, n_kv, group):
    n_pages, _, _, page = kbuf.shape[1:]
    for k in range(n_kv):
        qb = [jnp.broadcast_to(q_t[:, k * group + g:k * group + g + 1], (HEAD_DIM, page)) for g in range(group)]

        def body(p, carry):
            kt = kbuf[slot, p, k]
            for g in range(group):
                h = k * group + g
                sc_ref[p, h:h + 1, :] = jnp.sum(kt * qb[g], axis=0, keepdims=True)
            return carry

        lax.fori_loop(0, n_pages, body, 0, unroll=2)


def _decode_softmax(sc_ref, page_masks, s_new):
    s = jnp.where(jnp.stack(page_masks, axis=0) > 0.5, sc_ref[...], NEG_INF)
    m = jnp.maximum(jnp.max(jnp.max(s, axis=0), axis=1, keepdims=True), s_new)
    e = jnp.exp(s - m[None])
    e_new = jnp.exp(s_new - m)
    inv = 1.0 / jnp.maximum(jnp.sum(jnp.sum(e, axis=0), axis=1, keepdims=True) + e_new, 1e-30)
    sc_ref[...] = e * inv[None]
    return e_new * inv


def _decode_values(vbuf, slot, sc_ref, p_new, v_new_t, n_kv, group):
    n_pages, _, _, page = vbuf.shape[1:]
    cols = []
    for k in range(n_kv):
        for g0 in range(0, group, 2):
            hs = (k * group + g0, k * group + g0 + 1)

            def body(p, accs):
                vt = vbuf[slot, p, k]
                return tuple(a + vt * sc_ref[p, h:h + 1, :] for a, h in zip(accs, hs))

            zero = jnp.zeros((HEAD_DIM, page), F32)
            accs = lax.fori_loop(0, n_pages, body, (zero, zero), unroll=2)
            for a, h in zip(accs, hs):
                cols.append(jnp.sum(a, axis=1, keepdims=True) + p_new[h:h + 1, 0:1] * v_new_t[:, k:k + 1])
    return cols


def _place_cols(cols):
    lane = lax.broadcasted_iota(jnp.int32, (HEAD_DIM, len(cols)), 1)
    out = jnp.zeros((HEAD_DIM, len(cols)), F32)
    for h, c in enumerate(cols):
        out = jnp.where(lane == h, c, out)
    return out


def _moba_sample_kernel(pt_ref, qt_ref, q_ref, knew_ref, vnewt_ref, k_hbm, v_hbm, o_ref,
                        kbuf, vbuf, sc_ref, sem_k, sem_v, *, n_blocks):
    n_pages, _, _, page = kbuf.shape[1:]
    slot = _paged_step((k_hbm, v_hbm), (kbuf, vbuf), (sem_k, sem_v), pt_ref, n_pages)
    ppb = MOBA_BLOCK // page
    _decode_scores(qt_ref[0] * SCALE, kbuf, slot, sc_ref, MOBA_KV_HEADS, MOBA_GROUP)
    s_new = jnp.sum(q_ref[0] * knew_ref[0], axis=1, keepdims=True) * SCALE
    lane = lax.broadcasted_iota(jnp.int32, (MOBA_HEADS, LANE), 1)
    sb = jnp.full((MOBA_HEADS, LANE), NEG_INF, F32)
    for j in range(n_blocks):
        tot = sc_ref[j * ppb]
        for r in range(1, ppb):
            tot = tot + sc_ref[j * ppb + r]
        sb = jnp.where(lane == j, jnp.sum(tot, axis=1, keepdims=True) * (1.0 / MOBA_BLOCK), sb)
    sel = jnp.where((_rank(sb, n_blocks, 1) < MOBA_TOPK) & (lane < n_blocks), 1.0, 0.0)
    blk_masks = [jnp.broadcast_to(sel[:, j:j + 1], (MOBA_HEADS, page)) for j in range(n_blocks)]
    p_new = _decode_softmax(sc_ref, [blk_masks[p // ppb] for p in range(n_pages)], s_new)
    cols = _decode_values(vbuf, slot, sc_ref, p_new, vnewt_ref[0], MOBA_KV_HEADS, MOBA_GROUP)
    o_ref[0] = _place_cols(cols)


def _native(cache):
    return jnp.transpose(cache, (0, 2, 3, 1))


def _moba_sample(page_table, q, k_new, v_new, cache_k, cache_v):
    db, n_pages = page_table.shape
    page = cache_k.shape[1]
    n_past = n_pages * page
    assert n_past % MOBA_BLOCK == 0 and MOBA_BLOCK % page == 0 and page == LANE
    n_blocks = n_past // MOBA_BLOCK
    assert MOBA_TOPK <= n_blocks <= LANE
    q3 = q.reshape(db, MOBA_HEADS, HEAD_DIM)
    q_t = jnp.transpose(q3, (0, 2, 1))
    k_rows = jnp.repeat(k_new.reshape(db, MOBA_KV_HEADS, HEAD_DIM), MOBA_GROUP, axis=1)
    v_t = jnp.transpose(v_new.reshape(db, MOBA_KV_HEADS, HEAD_DIM), (0, 2, 1))
    per_b = lambda a: pl.BlockSpec((1,) + a.shape[1:], lambda b, pt: (b, 0, 0))
    any_spec = pl.BlockSpec(memory_space=pl.ANY)
    buf = pltpu.VMEM((2, n_pages, MOBA_KV_HEADS, HEAD_DIM, page), F32)
    grid_spec = pltpu.PrefetchScalarGridSpec(
        num_scalar_prefetch=1,
        grid=(db,),
        in_specs=[per_b(q_t), per_b(q3), per_b(k_rows), per_b(v_t), any_spec, any_spec],
        out_specs=pl.BlockSpec((1, HEAD_DIM, MOBA_HEADS), lambda b, pt: (b, 0, 0)),
        scratch_shapes=[buf, buf, pltpu.VMEM((n_pages, MOBA_HEADS, page), F32),
                        pltpu.SemaphoreType.DMA((2,)), pltpu.SemaphoreType.DMA((2,))])
    o_t = pl.pallas_call(
        functools.partial(_moba_sample_kernel, n_blocks=n_blocks),
        grid_spec=grid_spec,
        out_shape=jax.ShapeDtypeStruct((db, HEAD_DIM, MOBA_HEADS), F32),
        compiler_params=pltpu.CompilerParams(dimension_semantics=("arbitrary",), vmem_limit_bytes=VMEM_LIMIT),
        name="moba_sample",
    )(page_table, q_t, q3, k_rows, v_t, _native(cache_k), _native(cache_v))
    return jnp.transpose(o_t, (0, 2, 1)).reshape(db, D_MOBA_Q)


def _row_sort_matrix(page):
    cpp = page // CMP_STRIDE
    out = np.arange(page)
    src = (out % cpp) * CMP_STRIDE + out // cpp
    return (src[:, None] == np.arange(page)[None, :]).astype(np.float32)


def _compress_pages(buf, slot, xs_ref, sort_ref, bias_ref, w1_ref, w2_ref):
    n_pages, _, _, page = buf.shape[1:]
    cpp = page // CMP_STRIDE
    sort = sort_ref[...]

    def to_rows(pp, carry):
        ra = _dot_nt(sort, buf[slot, 2 * pp].reshape(D_NSA_KV, page).astype(BF16))
        rb = _dot_nt(sort, buf[slot, 2 * pp + 1].reshape(D_NSA_KV, page).astype(BF16))
        start = pl.multiple_of(pp * 2 * cpp, 2 * cpp)
        for l in range(CMP_STRIDE):
            rows = jnp.concatenate([ra[l * cpp:(l + 1) * cpp], rb[l * cpp:(l + 1) * cpp]], axis=0)
            xs_ref[l, pl.ds(start, 2 * cpp), :] = rows.astype(BF16)
        return carry

    lax.fori_loop(0, n_pages // 2, to_rows, 0, unroll=4)
    x = jnp.concatenate([xs_ref[l] for l in range(CMP_STRIDE)], axis=1)
    p = _dot(x, w1_ref[...]) + bias_ref[...]
    half = p.shape[1] // 2
    return _compress_mlp(p[:, :half], p[:, half:], w2_ref)


def _cmp_bias_kernel(kpe_ref, kw_ref, vpe_ref, vw_ref, kb_ref, vb_ref):
    for pe_ref, w_ref, b_ref in ((kpe_ref, kw_ref, kb_ref), (vpe_ref, vw_ref, vb_ref)):
        halves = [jnp.dot(pe_ref[r], w_ref[r], preferred_element_type=F32, precision=HIGHEST) for r in range(2)]
        b_ref[...] = jnp.concatenate(halves, axis=1)


def _cmp_bias(kpe, kw1, vpe, vw1):
    full = lambda a: pl.BlockSpec(a.shape, lambda i: (0,) * a.ndim)
    width = 2 * kw1.shape[2]
    out = pl.BlockSpec((1, width), lambda i: (0, 0))
    return pl.pallas_call(
        _cmp_bias_kernel,
        grid=(1,),
        in_specs=[full(kpe), full(kw1), full(vpe), full(vw1)],
        out_specs=[out, out],
        out_shape=[jax.ShapeDtypeStruct((1, width), F32)] * 2,
        compiler_params=pltpu.CompilerParams(dimension_semantics=("arbitrary",), vmem_limit_bytes=VMEM_LIMIT),
        name="cmp_bias",
    )(kpe, kw1, vpe, vw1)


def _nsa_sample_kernel(pt_ref, qn_ref, qrt_ref, qr_ref, gate_ref, sknew_ref, svnewt_ref, wknew_ref, wknewt_ref, wvnewt_ref,
                       wk_ref, wv_ref, sort_ref, kb_ref, kw1_ref, kw2_ref, vb_ref, vw1_ref, vw2_ref, msel_ref,
                       ck_hbm, cv_hbm, sk_hbm, sv_hbm, o_ref, wko_ref, wvo_ref,
                       ckbuf, cvbuf, skbuf, svbuf, xs_ref, sc_ref, sem_ck, sem_cv, sem_sk, sem_sv, *, n_cmp, n_sel):
    n_pages, _, _, page = skbuf.shape[1:]
    slot = _paged_step((ck_hbm, cv_hbm, sk_hbm, sv_hbm), (ckbuf, cvbuf, skbuf, svbuf),
                       (sem_ck, sem_cv, sem_sk, sem_sv), pt_ref, n_pages)
    n_chunks = n_pages * page // CMP_STRIDE
    qr_t = qrt_ref[0] * SCALE
    gates = gate_ref[0]
    ckc = _compress_pages(ckbuf, slot, xs_ref, sort_ref, kb_ref, kw1_ref, kw2_ref)
    cvc = _compress_pages(cvbuf, slot, xs_ref, sort_ref, vb_ref, vw1_ref, vw2_ref)
    n_idx = lax.broadcasted_iota(jnp.int32, (NSA_HEADS, n_chunks), 1)
    cmask = n_idx < n_cmp
    s = jnp.where(cmask, _dot_nt((qn_ref[0] * SCALE).astype(BF16), ckc.astype(BF16)), NEG_INF)
    m = jnp.max(s, axis=1, keepdims=True)
    e = jnp.where(cmask, jnp.exp(s - m), 0.0)
    p = e / jnp.maximum(jnp.sum(e, axis=1, keepdims=True), 1e-30)
    o_c = _dot(p.astype(BF16), cvc.astype(BF16))
    eye = jnp.where(lax.broadcasted_iota(jnp.int32, (D_NSA_KV, D_NSA_KV), 0)
                    == lax.broadcasted_iota(jnp.int32, (D_NSA_KV, D_NSA_KV), 1), 1.0, 0.0)
    o_c_t = _dot_nt(eye, o_c, precision=HIGHEST)
    p_kv = jnp.concatenate([jnp.sum(p[k * NSA_GROUP:(k + 1) * NSA_GROUP], axis=0, keepdims=True)
                            for k in range(NSA_KV_HEADS)], axis=0)
    nsb_pad = msel_ref.shape[1]
    score = jnp.dot(p_kv, msel_ref[...], preferred_element_type=F32, precision=HIGHEST)
    jb = lax.broadcasted_iota(jnp.int32, (NSA_KV_HEADS, nsb_pad), 1)
    cur = n_sel - 1
    valid = jb < n_sel
    forced = (jb == 0) | (jb == cur) | (jb == cur - 1)
    score = jnp.where(valid, jnp.where(forced, jnp.inf, score), NEG_INF)
    sel = jnp.where((_rank(score, n_sel, 1) < SEL_TOPN) & valid, 1.0, 0.0)
    row = lax.broadcasted_iota(jnp.int32, (NSA_HEADS, nsb_pad), 0)
    sel8 = jnp.where(row // NSA_GROUP == 0, sel[0:1], sel[1:2])
    _decode_scores(qr_t, skbuf, slot, sc_ref, NSA_KV_HEADS, NSA_GROUP)
    qr = qr_ref[0] * SCALE
    s_new = jnp.sum(qr * sknew_ref[0], axis=1, keepdims=True)
    bpp = page // SEL_LEN
    lane = lax.broadcasted_iota(jnp.int32, (NSA_HEADS, page), 1)
    masks = []
    for pg in range(n_pages):
        mk = jnp.broadcast_to(sel8[:, pg * bpp:pg * bpp + 1], (NSA_HEADS, page))
        for r in range(1, bpp):
            mk = jnp.where(lane >= r * SEL_LEN, sel8[:, pg * bpp + r:pg * bpp + r + 1], mk)
        masks.append(mk)
    p_new = _decode_softmax(sc_ref, masks, s_new)
    cols_s = _decode_values(svbuf, slot, sc_ref, p_new, svnewt_ref[0], NSA_KV_HEADS, NSA_GROUP)
    w_buf = wk_ref.shape[3]
    widx = lax.broadcasted_iota(jnp.int32, (NSA_HEADS, w_buf), 1)
    hrow = lax.broadcasted_iota(jnp.int32, (NSA_HEADS, w_buf), 0)
    sw = jnp.zeros((NSA_HEADS, w_buf), F32)
    for h in range(NSA_HEADS):
        kt = wk_ref[0, h // NSA_GROUP]
        sw = jnp.where(hrow == h, jnp.sum(kt * qr_t[:, h:h + 1], axis=0, keepdims=True), sw)
    sw = jnp.where(widx > w_buf - WINDOW, sw, NEG_INF)
    sw_new = jnp.sum(qr * wknew_ref[0], axis=1, keepdims=True)
    mw = jnp.maximum(jnp.max(sw, axis=1, keepdims=True), sw_new)
    ew = jnp.exp(sw - mw)
    ew_new = jnp.exp(sw_new - mw)
    invw = 1.0 / jnp.maximum(jnp.sum(ew, axis=1, keepdims=True) + ew_new, 1e-30)
    pw = ew * invw
    pw_new = ew_new * invw
    wvnew_t = wvnewt_ref[0]
    cols = []
    for h in range(NSA_HEADS):
        k = h // NSA_GROUP
        o_w = jnp.sum(wv_ref[0, k] * pw[h:h + 1, :], axis=1, keepdims=True) + pw_new[h:h + 1, 0:1] * wvnew_t[:, k:k + 1]
        o_cmp = o_c_t[k * HEAD_DIM:(k + 1) * HEAD_DIM, h:h + 1]
        cols.append(gates[0:1, h:h + 1] * o_cmp + gates[1:2, h:h + 1] * cols_s[h] + gates[2:3, h:h + 1] * o_w)
    o_ref[0] = _place_cols(cols)
    last = lax.broadcasted_iota(jnp.int32, (HEAD_DIM, w_buf), 1) == w_buf - 1
    wknew_t = wknewt_ref[0]
    for k in range(NSA_KV_HEADS):
        wko_ref[0, k] = jnp.where(last, wknew_t[:, k:k + 1], pltpu.roll(wk_ref[0, k], w_buf - 1, 1))
        wvo_ref[0, k] = jnp.where(last, wvnew_t[:, k:k + 1], pltpu.roll(wv_ref[0, k], w_buf - 1, 1))


def _expand_heads(q, n_heads, n_kv):
    b = q.shape[0]
    group = n_heads // n_kv
    place = jnp.asarray((np.arange(n_heads)[:, None] // group == np.arange(n_kv)[None, :]).astype(np.float32))
    q4 = q.reshape(b, n_heads, 1, HEAD_DIM) * place[None, :, :, None]
    return q4.reshape(b, n_heads, n_kv * HEAD_DIM)


def _nsa_sample(page_table, qn, qr, gates, sk_new, sv_new, wk_new, wv_new, state_wk, state_wv, kw, vw,
                cache_ck, cache_cv, cache_sk, cache_sv):
    db, n_pages = page_table.shape
    page = cache_ck.shape[1]
    n_past = n_pages * page
    assert page == LANE and page % SEL_LEN == 0
    n_chunks = n_past // CMP_STRIDE
    n_cmp = n_chunks - CMP_LEN // CMP_STRIDE + 1
    n_sel = -(-(n_past + 1) // SEL_LEN)
    nsb_pad = -(-n_sel // LANE) * LANE
    w_buf = state_wk.shape[1]
    heads = lambda a, n: a.reshape(db, n, HEAD_DIM)
    t = lambda a: jnp.transpose(a, (0, 2, 1))
    rep = lambda a: jnp.repeat(heads(a, NSA_KV_HEADS), NSA_GROUP, axis=1)
    qr3 = heads(qr, NSA_HEADS)
    msel = jnp.asarray(_cmp_to_sel(n_chunks, n_cmp, nsb_pad, n_sel))
    small = [_expand_heads(qn, NSA_HEADS, NSA_KV_HEADS), t(qr3), qr3, t(gates.reshape(db, NSA_HEADS, 3)),
             rep(sk_new), t(heads(sv_new, NSA_KV_HEADS)), rep(wk_new), t(heads(wk_new, NSA_KV_HEADS)),
             t(heads(wv_new, NSA_KV_HEADS))]
    kb, vb = _cmp_bias(kw[0], kw[1], vw[0], vw[1])
    cat = lambda w1: jnp.concatenate([w1[0], w1[1]], axis=1).astype(BF16)
    consts = [jnp.asarray(_row_sort_matrix(page)).astype(BF16), kb, cat(kw[1]), kw[2], vb, cat(vw[1]), vw[2], msel]
    wk_t, wv_t = _native(state_wk), _native(state_wv)
    per_b = lambda a: pl.BlockSpec((1,) + a.shape[1:], lambda b, pt: (b,) + (0,) * (a.ndim - 1))
    const = lambda a: pl.BlockSpec(a.shape, lambda b, pt: (0,) * a.ndim)
    any_spec = pl.BlockSpec(memory_space=pl.ANY)
    buf = pltpu.VMEM((2, n_pages, NSA_KV_HEADS, HEAD_DIM, page), F32)
    wspec = pl.BlockSpec((1, NSA_KV_HEADS, HEAD_DIM, w_buf), lambda b, pt: (b, 0, 0, 0))
    grid_spec = pltpu.PrefetchScalarGridSpec(
        num_scalar_prefetch=1,
        grid=(db,),
        in_specs=[per_b(a) for a in small] + [wspec, wspec] + [const(a) for a in consts] + [any_spec] * 4,
        out_specs=[pl.BlockSpec((1, HEAD_DIM, NSA_HEADS), lambda b, pt: (b, 0, 0)), wspec, wspec],
        scratch_shapes=[buf, buf, buf, buf,
                        pltpu.VMEM((CMP_STRIDE, n_chunks, D_NSA_KV), BF16),
                        pltpu.VMEM((n_pages, NSA_HEADS, page), F32)] + [pltpu.SemaphoreType.DMA((2,))] * 4)
    o_t, wko, wvo = pl.pallas_call(
        functools.partial(_nsa_sample_kernel, n_cmp=n_cmp, n_sel=n_sel),
        grid_spec=grid_spec,
        out_shape=[jax.ShapeDtypeStruct((db, HEAD_DIM, NSA_HEADS), F32),
                   jax.ShapeDtypeStruct(wk_t.shape, F32), jax.ShapeDtypeStruct(wv_t.shape, F32)],
        compiler_params=pltpu.CompilerParams(dimension_semantics=("arbitrary",), vmem_limit_bytes=VMEM_LIMIT),
        name="nsa_sample",
    )(page_table, *small, wk_t, wv_t, *consts,
      _native(cache_ck), _native(cache_cv), _native(cache_sk), _native(cache_sv))
    back = lambda a: jnp.transpose(a, (0, 3, 1, 2))
    return jnp.transpose(o_t, (0, 2, 1)).reshape(db, D_NSA_Q), back(wko), back(wvo)


def _ffn_weights(w_in, w_out):
    d, f2 = w_in.shape
    f = f2 // 2
    n_chunks = f // FFN_CHUNK
    wa = w_in[:, :f].reshape(d, n_chunks, FFN_CHUNK).transpose(1, 0, 2).astype(BF16)
    wb = w_in[:, f:].reshape(d, n_chunks, FFN_CHUNK).transpose(1, 0, 2).astype(BF16)
    return wa, wb, w_out.astype(BF16)


def kernel(x_prompt, x_sample, cache_moba_k, cache_moba_v, cache_nsa_cmp_k, cache_nsa_cmp_v, cache_nsa_sel_k, cache_nsa_sel_v, state_nsa_win_k, state_nsa_win_v, page_table, c_prompt, c_sample, w_ada, b_ada, norm_ffn1, w_ffn1_in, w_ffn1_out, norm_mix, w_mix_in, w_mix_out, norm_ffn2, w_ffn2_in, w_ffn2_out, cmp_k_pe, cmp_k_w1, cmp_k_w2, cmp_v_pe, cmp_v_w1, cmp_v_w2, norm_final):
    depth = w_ada.shape[0]
    assert depth == 1, "single-layer step"
    b, s, d = x_prompt.shape
    db = x_sample.shape[0]
    assert x_sample.shape[1] == 1
    n_pages = page_table.shape[1]
    page = cache_moba_k.shape[2]
    n_past = n_pages * page
    assert s % Q_TILE == 0 and db % 8 == 0
    l = 0

    ffn1_w = _ffn_weights(w_ffn1_in[l], w_ffn1_out[l])
    ffn2_w = _ffn_weights(w_ffn2_in[l], w_ffn2_out[l])
    w_proj = jnp.pad(w_mix_in[l], ((0, 0), (0, PROJ_PAD - w_mix_in.shape[2]))).astype(BF16)
    w_mo = w_mix_out[l].astype(BF16)
    w_mo_m, w_mo_n = w_mo[:D_MOBA_Q], w_mo[D_MOBA_Q:]
    kw = _compress_weights(cmp_k_pe[l], cmp_k_w1[l], cmp_k_w2[l])
    vw = _compress_weights(cmp_v_pe[l], cmp_v_w1[l], cmp_v_w2[l])

    mods = _ada_mods(jnp.concatenate([c_sample, c_prompt], axis=0), w_ada[l], b_ada[l])
    mods_p = _Mods(mods, db, per_row=False)
    mods_s = _Mods(mods, db, per_row=True)
    xs = x_sample.reshape(1, db, d)
    tm_p = 512 if s % 512 == 0 else Q_TILE

    xp1 = _ffn(x_prompt, mods_p, (0, 1, 2), norm_ffn1[l], *ffn1_w, tm=tm_p)
    xs1 = _ffn(xs, mods_s, (0, 1, 2), norm_ffn1[l], *ffn1_w, tm=db)

    tabs_p = _rope_tables(jnp.arange(s, dtype=jnp.int32))
    pp = _proj(xp1, mods_p, (3, 4), norm_mix[l], w_proj, tabs_p, tm=tm_p, head_major=True)
    (mq, mk, mv, ck, cv, sk, sv, wk, wv, gates,
     mqh, mkh, mvh, nqnh, nqh, skh, svh, wkh, wvh) = pp
    om = _moba_prompt(mq, mqh, mk, mkh, mvh)
    bf = lambda w: (w[0], w[1].astype(BF16), w[2])
    ckc, cvc = _compress_prompt(ck, cv, bf(kw), bf(vw))
    on = _nsa_prompt(nqnh, nqh, gates, ckc, cvc, skh, svh, wkh, wvh)

    tabs_s = _rope_tables(jnp.full((db,), n_past, dtype=jnp.int32))
    ps = _proj(xs1, mods_s, (3, 4), norm_mix[l], w_proj, tabs_s, tm=db, head_major=False)
    (smq, smk, smv, sck, scv, ssk, ssv, swk, swv, sgates, snqn, snq) = [a[0] for a in ps]
    o_m_s = _moba_sample(page_table, smq, smk, smv, cache_moba_k[l], cache_moba_v[l])
    o_n_s, win_k_s, win_v_s = _nsa_sample(
        page_table, snqn, snq, sgates[:, :N_GATES], ssk, ssv, swk, swv,
        state_nsa_win_k[l], state_nsa_win_v[l], kw, vw,
        cache_nsa_cmp_k[l], cache_nsa_cmp_v[l], cache_nsa_sel_k[l], cache_nsa_sel_v[l])
    om_s = o_m_s.reshape(1, db, D_MOBA_Q).astype(BF16)
    on_s = o_n_s.reshape(1, db, D_NSA_Q).astype(BF16)

    yp = _ffn(xp1, mods_p, (6, 7, 8), norm_ffn2[l], *ffn2_w, tm=tm_p,
              mix=(om, on, w_mo_m, w_mo_n, 5), final_g=norm_final)
    ys = _ffn(xs1, mods_s, (6, 7, 8), norm_ffn2[l], *ffn2_w, tm=db,
              mix=(om_s, on_s, w_mo_m, w_mo_n, 5), final_g=norm_final)

    w_keep = min(WINDOW, s)
    st = lambda a, n: a.reshape(1, b, a.shape[1], n, HEAD_DIM)
    ss_ = lambda a, n: a.reshape(1, db, 1, n, HEAD_DIM)
    return (yp, ys.reshape(db, 1, d),
            st(mk, MOBA_KV_HEADS), ss_(smk, MOBA_KV_HEADS), st(mv, MOBA_KV_HEADS), ss_(smv, MOBA_KV_HEADS),
            st(ck, NSA_KV_HEADS), ss_(sck, NSA_KV_HEADS), st(cv, NSA_KV_HEADS), ss_(scv, NSA_KV_HEADS),
            st(sk, NSA_KV_HEADS), ss_(ssk, NSA_KV_HEADS), st(sv, NSA_KV_HEADS), ss_(ssv, NSA_KV_HEADS),
            st(wk[:, s - w_keep:], NSA_KV_HEADS), win_k_s[None],
            st(wv[:, s - w_keep:], NSA_KV_HEADS), win_v_s[None])
```

```python
import functools

import numpy as np
import jax
import jax.numpy as jnp
from jax import lax
from jax.experimental import pallas as pl
from jax.experimental.pallas import tpu as pltpu

F32 = jnp.float32
BF16 = jnp.bfloat16
NEG_INF = float("-inf")
HIGHEST = lax.Precision.HIGHEST

HEAD_DIM = 64
MOBA_HEADS = 8
MOBA_KV_HEADS = 4
MOBA_GROUP = MOBA_HEADS // MOBA_KV_HEADS
NSA_HEADS = 8
NSA_KV_HEADS = 2
NSA_GROUP = NSA_HEADS // NSA_KV_HEADS
ROPE_DIM = HEAD_DIM // 4
ROPE_THETA = 500000.0
MOBA_BLOCK = 256
MOBA_TOPK = 3
CMP_LEN = 32
CMP_STRIDE = 16
CMP_HIDDEN = 2 * HEAD_DIM
SEL_LEN = 64
SEL_TOPN = 16
WINDOW = 512
MACARON_WEIGHT = 0.5
N_MOD = 9
EPS = 1e-6
SCALE = HEAD_DIM ** -0.5
D_MOBA_Q = MOBA_HEADS * HEAD_DIM
D_MOBA_KV = MOBA_KV_HEADS * HEAD_DIM
D_NSA_Q = NSA_HEADS * HEAD_DIM
D_NSA_KV = NSA_KV_HEADS * HEAD_DIM
N_GATES = 3 * NSA_HEADS
LANE = 128
PROJ_PAD = 2432
VMEM_LIMIT = 56 * 1024 * 1024
Q_TILE = 256
FFN_CHUNK = 256

OFF_MQ, OFF_MK, OFF_MV, OFF_NQ = 0, 512, 768, 1024
OFF_CK, OFF_CV, OFF_SK, OFF_SV, OFF_WK, OFF_WV, OFF_G = 1536, 1664, 1792, 1920, 2048, 2176, 2304


def _silu(x):
    return x / (1.0 + jnp.exp(-x))


def _sigmoid(x):
    return 1.0 / (1.0 + jnp.exp(-x))


def _dot(a, b):
    return jnp.dot(a, b, preferred_element_type=F32)


def _dot_nt(a, b, precision=None):
    return lax.dot_general(a, b, (((1,), (1,)), ((), ())), preferred_element_type=F32, precision=precision)


def _modulated(x, g, shift, scale):
    ms = jnp.mean(x * x, axis=-1, keepdims=True)
    return (x * lax.rsqrt(ms + EPS) * g) * (1.0 + scale) + shift


def _rank(sc, ncols, axis):
    idx = lax.broadcasted_iota(jnp.int32, sc.shape, axis)
    rank = jnp.zeros(sc.shape, F32)
    for i in range(ncols):
        ci = sc[:, i:i + 1] if axis == 1 else sc[i:i + 1, :]
        beats = (ci > sc) | ((ci == sc) & (idx > i))
        rank = rank + jnp.where(beats, 1.0, 0.0)
    return rank


def _ada_kernel(c_ref, w_ref, b_ref, o_ref):
    sc = _silu(c_ref[...]).astype(BF16)
    o_ref[0] = _dot(sc, w_ref[...].astype(BF16)) + b_ref[0]


def _ada_mods(c_all, w_ada, b_ada):
    rows, d = c_all.shape
    return pl.pallas_call(
        _ada_kernel,
        grid=(N_MOD,),
        in_specs=[pl.BlockSpec((rows, d), lambda j: (0, 0)),
                  pl.BlockSpec((d, d), lambda j: (0, j)),
                  pl.BlockSpec((1, 1, d), lambda j: (j, 0, 0))],
        out_specs=pl.BlockSpec((1, rows, d), lambda j: (j, 0, 0)),
        out_shape=jax.ShapeDtypeStruct((N_MOD, rows, d), F32),
        compiler_params=pltpu.CompilerParams(dimension_semantics=("arbitrary",), vmem_limit_bytes=VMEM_LIMIT),
        name="ada_mods",
    )(c_all, w_ada, b_ada.reshape(N_MOD, 1, d))


class _Mods:
    def __init__(self, mods, n_sample, per_row):
        self.per_row = per_row
        self.n_sample = n_sample
        self.rows = mods.shape[1]
        d = mods.shape[2]
        self.d = d
        self.arr = mods if per_row else mods.reshape(N_MOD * self.rows, 1, d)

    def spec(self, j):
        if self.per_row:
            return pl.BlockSpec((1, self.n_sample, self.d), lambda g, i: (j, 0, 0))
        base = j * self.rows + self.n_sample
        return pl.BlockSpec((1, 1, self.d), lambda g, i: (base + g, 0, 0))


def _ffn_kernel(*refs, has_mix, final_norm, n_chunks):
    it = iter(refs)
    x_ref = next(it)
    if has_mix:
        mm_ref, mn_ref, wmm_ref, wmn_ref, gmix_ref = next(it), next(it), next(it), next(it), next(it)
    sh_ref, sc_ref, gt_ref, g_ref, win_ref, wo_ref = (next(it) for _ in range(6))
    gf_ref = next(it) if final_norm else None
    o_ref, h_ref = next(it), next(it)

    x = x_ref[0]
    if has_mix:
        x = x + gmix_ref[0] * (_dot(mm_ref[0], wmm_ref[...]) + _dot(mn_ref[0], wmn_ref[...]))
    xm = _modulated(x, g_ref[...], sh_ref[0], sc_ref[0]).astype(BF16)
    f = wo_ref.shape[0]
    tf = f // n_chunks
    for j in range(n_chunks):
        a = _dot(xm, win_ref[:, j * tf:(j + 1) * tf])
        b = _dot(xm, win_ref[:, f + j * tf:f + (j + 1) * tf])
        h_ref[:, j * tf:(j + 1) * tf] = (_silu(a) * b).astype(BF16)
    y = x + (MACARON_WEIGHT * gt_ref[0]) * _dot(h_ref[...], wo_ref[...])
    if final_norm:
        ms = jnp.mean(y * y, axis=-1, keepdims=True)
        y = y * lax.rsqrt(ms + EPS) * gf_ref[...]
    o_ref[0] = y


def _ffn(x, mods, jmods, norm_g, win, wo, tm, mix=None, final_g=None):
    g_, r_, d = x.shape
    f = wo.shape[0]
    assert f % FFN_CHUNK == 0
    n_chunks = f // FFN_CHUNK
    const2 = lambda g, i: (0, 0)
    row_spec = lambda w: pl.BlockSpec((1, tm, w), lambda g, i: (g, i, 0))
    args, specs = [x], [row_spec(d)]
    if mix is not None:
        mm, mn, wmm, wmn, jmix = mix
        args += [mm, mn, wmm, wmn, mods.arr]
        specs += [row_spec(mm.shape[2]), row_spec(mn.shape[2]),
                  pl.BlockSpec(wmm.shape, const2), pl.BlockSpec(wmn.shape, const2), mods.spec(jmix)]
    args += [mods.arr, mods.arr, mods.arr, norm_g.reshape(1, d), win, wo]
    specs += [mods.spec(jmods[0]), mods.spec(jmods[1]), mods.spec(jmods[2]),
              pl.BlockSpec((1, d), const2), pl.BlockSpec(win.shape, const2), pl.BlockSpec(wo.shape, const2)]
    if final_g is not None:
        args.append(final_g.reshape(1, d))
        specs.append(pl.BlockSpec((1, d), const2))
    kern = functools.partial(_ffn_kernel, has_mix=mix is not None, final_norm=final_g is not None, n_chunks=n_chunks)
    return pl.pallas_call(
        kern,
        grid=(g_, r_ // tm),
        in_specs=specs,
        out_specs=row_spec(d),
        out_shape=jax.ShapeDtypeStruct((g_, r_, d), F32),
        scratch_shapes=[pltpu.VMEM((tm, f), BF16)],
        compiler_params=pltpu.CompilerParams(dimension_semantics=("arbitrary", "arbitrary"),
                                             vmem_limit_bytes=VMEM_LIMIT),
        name="ffn",
    )(*args)


def _rope_tables(pos):
    half = ROPE_DIM // 2
    inv_freq = ROPE_THETA ** (-jnp.arange(half, dtype=F32) / half)
    ang = pos.astype(F32)[:, None] * inv_freq
    cos, sin = jnp.cos(ang), jnp.sin(ang)
    rows = pos.shape[0]
    rest = HEAD_DIM - ROPE_DIM
    one, zero = jnp.ones((rows, rest), F32), jnp.zeros((rows, rest), F32)
    zh = jnp.zeros((rows, half), F32)
    c = jnp.concatenate([cos, cos, one], axis=1)
    sa = jnp.concatenate([-sin, zh, zero], axis=1)
    sb = jnp.concatenate([zh, sin, zero], axis=1)
    tile = lambda t: jnp.concatenate([t, t], axis=1)
    return tile(c), tile(sa), tile(sb)


def _proj_kernel(*refs, head_major):
    (x_ref, sh_ref, sc_ref, g_ref, w_ref, cos_ref, sa_ref, sb_ref) = refs[:8]
    outs = refs[8:]
    (mq_ref, mk_ref, mv_ref, ck_ref, cv_ref, sk_ref, sv_ref, wk_ref, wv_ref, gate_ref) = outs[:10]
    xm = _modulated(x_ref[0], g_ref[...], sh_ref[0], sc_ref[0]).astype(BF16)
    cos, sa, sb = cos_ref[...], sa_ref[...], sb_ref[...]
    half = ROPE_DIM // 2

    def seg(lo, width):
        return _dot(xm, w_ref[:, lo:lo + width])

    def rope(y):
        parts = []
        for c in range(y.shape[1] // LANE):
            s = y[:, c * LANE:(c + 1) * LANE]
            parts.append(s * cos + pltpu.roll(s, LANE - half, 1) * sa + pltpu.roll(s, half, 1) * sb)
        return parts[0] if len(parts) == 1 else jnp.concatenate(parts, axis=1)

    mq = rope(seg(OFF_MQ, D_MOBA_Q))
    mk = rope(seg(OFF_MK, D_MOBA_KV))
    mv = seg(OFF_MV, D_MOBA_KV)
    nqn = seg(OFF_NQ, D_NSA_Q)
    nq = rope(nqn)
    ck = seg(OFF_CK, D_NSA_KV)
    cv = seg(OFF_CV, D_NSA_KV)
    sk = rope(seg(OFF_SK, D_NSA_KV))
    sv = seg(OFF_SV, D_NSA_KV)
    wk = rope(seg(OFF_WK, D_NSA_KV))
    wv = seg(OFF_WV, D_NSA_KV)
    gate_ref[0] = _sigmoid(seg(OFF_G, LANE))
    mq_ref[0], mk_ref[0], mv_ref[0] = mq, mk, mv
    ck_ref[0], cv_ref[0], sk_ref[0], sv_ref[0], wk_ref[0], wv_ref[0] = ck, cv, sk, sv, wk, wv
    if not head_major:
        nqn_ref, nq_ref = outs[10:]
        nqn_ref[0], nq_ref[0] = nqn, nq
    else:
        (mqh_ref, mkh_ref, mvh_ref, nqnh_ref, nqh_ref, skh_ref, svh_ref, wkh_ref, wvh_ref) = outs[10:]

        def put(ref, val, n, scale):
            for h in range(n):
                piece = val[:, h * HEAD_DIM:(h + 1) * HEAD_DIM]
                ref[0, h] = (piece * scale if scale != 1.0 else piece).astype(BF16)

        put(mqh_ref, mq, MOBA_HEADS, SCALE)
        put(mkh_ref, mk, MOBA_KV_HEADS, 1.0)
        put(mvh_ref, mv, MOBA_KV_HEADS, 1.0)
        put(nqnh_ref, nqn, NSA_HEADS, SCALE)
        put(nqh_ref, nq, NSA_HEADS, SCALE)
        put(skh_ref, sk, NSA_KV_HEADS, 1.0)
        put(svh_ref, sv, NSA_KV_HEADS, 1.0)
        put(wkh_ref, wk, NSA_KV_HEADS, 1.0)
        put(wvh_ref, wv, NSA_KV_HEADS, 1.0)


def _proj(x, mods, jmods, norm_g, w_pad, tables, tm, head_major):
    g_, r_, d = x.shape
    const2 = lambda g, i: (0, 0)
    row_spec = lambda w: pl.BlockSpec((1, tm, w), lambda g, i: (g, i, 0))
    tab_spec = pl.BlockSpec((tm, LANE), lambda g, i: (i, 0))
    in_specs = [row_spec(d), mods.spec(jmods[0]), mods.spec(jmods[1]), pl.BlockSpec((1, d), const2),
                pl.BlockSpec(w_pad.shape, const2), tab_spec, tab_spec, tab_spec]
    widths = [D_MOBA_Q, D_MOBA_KV, D_MOBA_KV] + [D_NSA_KV] * 6 + [LANE]
    if not head_major:
        widths += [D_NSA_Q, D_NSA_Q]
    out_specs = [row_spec(w) for w in widths]
    out_shape = [jax.ShapeDtypeStruct((g_, r_, w), F32) for w in widths]
    if head_major:
        for n in (MOBA_HEADS, MOBA_KV_HEADS, MOBA_KV_HEADS, NSA_HEADS, NSA_HEADS) + (NSA_KV_HEADS,) * 4:
            out_specs.append(pl.BlockSpec((1, n, tm, HEAD_DIM), lambda g, i: (g, 0, i, 0)))
            out_shape.append(jax.ShapeDtypeStruct((g_, n, r_, HEAD_DIM), BF16))
    return pl.pallas_call(
        functools.partial(_proj_kernel, head_major=head_major),
        grid=(g_, r_ // tm),
        in_specs=in_specs,
        out_specs=out_specs,
        out_shape=out_shape,
        compiler_params=pltpu.CompilerParams(dimension_semantics=("arbitrary", "arbitrary"),
                                             vmem_limit_bytes=VMEM_LIMIT),
        name="proj",
    )(x, mods.arr, mods.arr, norm_g.reshape(1, d), w_pad, *tables)


def _compress_mlp(p0, p1, w2_ref):
    h = p0 + pltpu.roll(p1, p0.shape[0] - 1, 0)
    return _dot(_silu(h).astype(BF16), w2_ref[...])


def _compress_rows(x, pe_ref, w1_ref, w2_ref):
    p0 = _dot((x + pe_ref[0]).astype(BF16), w1_ref[0])
    p1 = _dot((x + pe_ref[1]).astype(BF16), w1_ref[1])
    return _compress_mlp(p0, p1, w2_ref)


def _compress_kernel(ck_ref, cv_ref, kpe_ref, kw1_ref, kw2_ref, vpe_ref, vw1_ref, vw2_ref, okc_ref, ovc_ref):
    okc_ref[0] = _compress_rows(ck_ref[0], kpe_ref, kw1_ref, kw2_ref)
    ovc_ref[0] = _compress_rows(cv_ref[0], vpe_ref, vw1_ref, vw2_ref)


def _compress_weights(pe, w1, w2):
    ratio = CMP_LEN // CMP_STRIDE
    eye = jnp.eye(NSA_KV_HEADS, dtype=F32)
    w1r = w1.reshape(ratio, CMP_STRIDE, HEAD_DIM, CMP_HIDDEN)
    w1f = jnp.einsum("rlde,kK->rlkdKe", w1r, eye).reshape(ratio, CMP_STRIDE * D_NSA_KV, NSA_KV_HEADS * CMP_HIDDEN)
    pef = jnp.broadcast_to(pe.reshape(ratio, CMP_STRIDE, 1, HEAD_DIM),
                           (ratio, CMP_STRIDE, NSA_KV_HEADS, HEAD_DIM)).reshape(ratio, 1, CMP_STRIDE * D_NSA_KV)
    w2f = jnp.einsum("ed,kK->keKd", w2, eye).reshape(NSA_KV_HEADS * CMP_HIDDEN, D_NSA_KV)
    return pef, w1f, w2f.astype(BF16)


def _compress_prompt(ck, cv, kw, vw):
    b, s, _ = ck.shape
    n = s // CMP_STRIDE
    width = CMP_STRIDE * D_NSA_KV
    ck2, cv2 = ck.reshape(b, n, width), cv.reshape(b, n, width)
    row = pl.BlockSpec((1, n, width), lambda i: (i, 0, 0))
    wspecs = []
    for w in kw + vw:
        wspecs.append(pl.BlockSpec(w.shape, (lambda i: (0, 0, 0)) if w.ndim == 3 else (lambda i: (0, 0))))
    out = pl.BlockSpec((1, n, D_NSA_KV), lambda i: (i, 0, 0))
    return pl.pallas_call(
        _compress_kernel,
        grid=(b,),
        in_specs=[row, row] + wspecs,
        out_specs=[out, out],
        out_shape=[jax.ShapeDtypeStruct((b, n, D_NSA_KV), F32)] * 2,
        compiler_params=pltpu.CompilerParams(dimension_semantics=("arbitrary",), vmem_limit_bytes=VMEM_LIMIT),
        name="compress_prompt",
    )(ck2, cv2, *kw, *vw)


def _two_pass_init(mx_ref, l_ref, acc_ref):
    mx_ref[...] = jnp.full(mx_ref.shape, NEG_INF, F32)
    l_ref[...] = jnp.zeros(l_ref.shape, F32)
    acc_ref[...] = jnp.zeros(acc_ref.shape, F32)


def _pass1(j, s, s_ref, mx_ref):
    s_ref[j] = s
    m = s[:, 0:LANE]
    for c in range(1, s.shape[1] // LANE):
        m = jnp.maximum(m, s[:, c * LANE:(c + 1) * LANE])
    mx_ref[...] = jnp.maximum(mx_ref[...], m)


def _row_max(mx_ref, mb_ref):
    m = jnp.max(mx_ref[...], axis=1, keepdims=True)
    m = jnp.where(m == NEG_INF, 0.0, m)
    mb_ref[...] = jnp.broadcast_to(m, mb_ref.shape)


def _pass2(j, v, s_ref, mb_ref, l_ref, acc_ref):
    mb = mb_ref[...]
    s = s_ref[j]
    parts = [jnp.exp(s[:, c * LANE:(c + 1) * LANE] - mb) for c in range(s.shape[1] // LANE)]
    tot = parts[0]
    for p in parts[1:]:
        tot = tot + p
    l_ref[...] += tot
    acc_ref[...] += _dot(jnp.concatenate(parts, axis=1).astype(BF16), v)


def _two_pass_out(l_ref, acc_ref):
    l = jnp.sum(l_ref[...], axis=1, keepdims=True)
    return acc_ref[...] / jnp.maximum(l, 1e-30)


def _add_bias(s, bias, groups):
    tq = bias.shape[0]
    return (s.reshape(groups, tq, s.shape[1]) + bias[None]).reshape(s.shape)


def _attn_scratch(rows, n_chunks, chunk):
    return [pltpu.VMEM((rows, HEAD_DIM), BF16),
            pltpu.VMEM((n_chunks, rows, chunk), F32),
            pltpu.VMEM((rows, LANE), F32),
            pltpu.VMEM((rows, LANE), F32),
            pltpu.VMEM((rows, LANE), F32),
            pltpu.VMEM((rows, HEAD_DIM), F32)]


def _moba_prompt_kernel(mq_ref, mqh_ref, mk_ref, mkh_ref, mvh_ref, o_ref,
                        kmean_ref, bias_ref, q_ref, s_ref, mx_ref, mb_ref, l_ref, acc_ref, *, n_blocks):
    qi = pl.program_id(1)
    tq = mq_ref.shape[1]
    rows = MOBA_GROUP * tq

    @pl.when(qi == 0)
    def _():
        for j in range(n_blocks):
            blk = mk_ref[0, j * MOBA_BLOCK:(j + 1) * MOBA_BLOCK, :]
            kmean_ref[j:j + 1, :] = jnp.sum(blk, axis=0, keepdims=True) * (1.0 / MOBA_BLOCK)

    blk_t = lax.broadcasted_iota(jnp.int32, (n_blocks, tq), 0)
    r_loc = lax.broadcasted_iota(jnp.int32, (rows, MOBA_BLOCK), 0) % tq
    c_loc = lax.broadcasted_iota(jnp.int32, (rows, MOBA_BLOCK), 1)
    pad = jnp.full((LANE - n_blocks, tq), NEG_INF, F32)

    pieces = []
    for k in range(MOBA_KV_HEADS):
        km = kmean_ref[:, k * HEAD_DIM:(k + 1) * HEAD_DIM]
        for g in range(MOBA_GROUP):
            h = k * MOBA_GROUP + g
            qf = mq_ref[0, :, h * HEAD_DIM:(h + 1) * HEAD_DIM]
            sb = jnp.where(blk_t < qi, _dot_nt(km, qf, precision=HIGHEST), NEG_INF)
            sel = (_rank(sb, n_blocks, 0) < MOBA_TOPK) & (blk_t < qi)
            bias_t = jnp.concatenate([jnp.where(sel, 0.0, NEG_INF), pad], axis=0)
            bias_ref[g * tq:(g + 1) * tq, :] = bias_t.T
            q_ref[g * tq:(g + 1) * tq, :] = mqh_ref[0, h]
        _two_pass_init(mx_ref, l_ref, acc_ref)
        for j in range(n_blocks):
            kc = mkh_ref.at[0, k, j * MOBA_BLOCK:(j + 1) * MOBA_BLOCK, :]

            @pl.when(j < qi)
            def _():
                _pass1(j, _dot_nt(q_ref[...], kc[...]) + bias_ref[:, j:j + 1], s_ref, mx_ref)

            @pl.when(j == qi)
            def _():
                _pass1(j, jnp.where(c_loc <= r_loc, _dot_nt(q_ref[...], kc[...]), NEG_INF), s_ref, mx_ref)

        _row_max(mx_ref, mb_ref)
        for j in range(n_blocks):
            @pl.when(j <= qi)
            def _():
                _pass2(j, mvh_ref[0, k, j * MOBA_BLOCK:(j + 1) * MOBA_BLOCK, :], s_ref, mb_ref, l_ref, acc_ref)

        o = _two_pass_out(l_ref, acc_ref)
        pieces += [o[g * tq:(g + 1) * tq, :] for g in range(MOBA_GROUP)]
    o_ref[0] = jnp.concatenate(pieces, axis=1).astype(BF16)


def _moba_prompt(mq, mqh, mk, mkh, mvh):
    b, s, _ = mq.shape
    tq = Q_TILE
    assert tq == MOBA_BLOCK
    n_blocks = s // MOBA_BLOCK
    rows = MOBA_GROUP * tq
    full = lambda n: pl.BlockSpec((1, n, s, HEAD_DIM), lambda i, j: (i, 0, 0, 0))
    return pl.pallas_call(
        functools.partial(_moba_prompt_kernel, n_blocks=n_blocks),
        grid=(b, s // tq),
        in_specs=[pl.BlockSpec((1, tq, D_MOBA_Q), lambda i, j: (i, j, 0)),
                  pl.BlockSpec((1, MOBA_HEADS, tq, HEAD_DIM), lambda i, j: (i, 0, j, 0)),
                  pl.BlockSpec((1, s, D_MOBA_KV), lambda i, j: (i, 0, 0)),
                  full(MOBA_KV_HEADS), full(MOBA_KV_HEADS)],
        out_specs=pl.BlockSpec((1, tq, D_MOBA_Q), lambda i, j: (i, j, 0)),
        out_shape=jax.ShapeDtypeStruct((b, s, D_MOBA_Q), BF16),
        scratch_shapes=[pltpu.VMEM((n_blocks, D_MOBA_KV), F32),
                        pltpu.VMEM((rows, LANE), F32)] + _attn_scratch(rows, n_blocks, MOBA_BLOCK),
        compiler_params=pltpu.CompilerParams(dimension_semantics=("arbitrary", "arbitrary"),
                                             vmem_limit_bytes=VMEM_LIMIT),
        name="moba_prompt",
    )(mq, mqh, mk, mkh, mvh)


def _cmp_to_sel(nc_pad, nc, nsb_pad, nsb):
    i = np.arange(nc_pad)[:, None]
    j = np.arange(nsb_pad)[None, :]
    start = i * CMP_STRIDE
    m = (start < (j + 1) * SEL_LEN) & (start + CMP_LEN > j * SEL_LEN) & (i < nc) & (j < nsb)
    return m.astype(np.float32)


def _block_expand(n_blocks_pad, n_blocks, block_len, chunk):
    n_chunks = n_blocks * block_len // chunk
    pos = np.arange(n_chunks * chunk).reshape(n_chunks, 1, chunk)
    j = np.arange(n_blocks_pad).reshape(1, n_blocks_pad, 1)
    return (pos // block_len == j).astype(np.float32)


def _nsa_prompt_kernel(nqnh_ref, nqh_ref, gate_ref, ckc_ref, cvc_ref, skh_ref, svh_ref, wkh_ref, wvh_ref, mselt_ref, exp_ref,
                       o_ref, sel_ref, q_ref, s_ref, mx_ref, mb_ref, l_ref, acc_ref, *, n_cmp, n_sel):
    qi = pl.program_id(1)
    tq = nqnh_ref.shape[2]
    rows = NSA_GROUP * tq
    nc_pad = ckc_ref.shape[1]
    nsb_pad = mselt_ref.shape[0]
    n_chunks, _, chunk = exp_ref.shape
    q0 = qi * tq
    gates = gate_ref[0]

    qpos_c = q0 + lax.broadcasted_iota(jnp.int32, (rows, nc_pad), 0) % tq
    n_idx = lax.broadcasted_iota(jnp.int32, (rows, nc_pad), 1)
    cmp_mask = (n_idx * CMP_STRIDE + (CMP_LEN - 1) <= qpos_c) & (n_idx < n_cmp)

    t = q0 + lax.broadcasted_iota(jnp.int32, (nsb_pad, tq), 1)
    jb = lax.broadcasted_iota(jnp.int32, (nsb_pad, tq), 0)
    cur = t // SEL_LEN
    valid = (jb * SEL_LEN <= t) & (jb < n_sel)
    forced = (jb == 0) | (jb == cur) | (jb == cur - 1)
    pad = jnp.zeros((LANE - nsb_pad, tq), F32)

    r_loc = lax.broadcasted_iota(jnp.int32, (tq, chunk), 0)
    c_loc = lax.broadcasted_iota(jnp.int32, (tq, chunk), 1)
    lower = jnp.where(c_loc <= r_loc, 0.0, NEG_INF)
    upper = jnp.where(c_loc > r_loc, 0.0, NEG_INF)

    pieces = []
    for k in range(NSA_KV_HEADS):
        lo = k * HEAD_DIM
        for g in range(NSA_GROUP):
            q_ref[g * tq:(g + 1) * tq, :] = nqnh_ref[0, k * NSA_GROUP + g]
        kc = ckc_ref[0, :, lo:lo + HEAD_DIM].astype(BF16)
        vc = cvc_ref[0, :, lo:lo + HEAD_DIM].astype(BF16)
        s = jnp.where(cmp_mask, _dot_nt(q_ref[...], kc), NEG_INF)
        m = jnp.max(s, axis=1, keepdims=True)
        m = jnp.where(m == NEG_INF, 0.0, m)
        e = jnp.where(cmp_mask, jnp.exp(s - m), 0.0)
        p = e / jnp.maximum(jnp.sum(e, axis=1, keepdims=True), 1e-30)
        o_c = _dot(p.astype(BF16), vc)
        p_kv = p[0:tq]
        for g in range(1, NSA_GROUP):
            p_kv = p_kv + p[g * tq:(g + 1) * tq]
        score = _dot_nt(mselt_ref[...], p_kv, precision=HIGHEST)
        score = jnp.where(valid, jnp.where(forced, jnp.inf, score), NEG_INF)
        sel = (_rank(score, n_sel, 0) < SEL_TOPN) & valid
        sel_t = jnp.concatenate([jnp.where(sel, 1.0, 0.0), pad], axis=0)
        sel_ref[...] = sel_t.T[:, :nsb_pad].astype(BF16)
        for g in range(NSA_GROUP):
            q_ref[g * tq:(g + 1) * tq, :] = nqh_ref[0, k * NSA_GROUP + g]
        _two_pass_init(mx_ref, l_ref, acc_ref)
        for j in range(n_chunks):
            kj = skh_ref.at[0, k, j * chunk:(j + 1) * chunk, :]
            ej = exp_ref.at[j]

            @pl.when(j < qi)
            def _():
                bias = jnp.where(_dot(sel_ref[...], ej[...]) > 0.5, 0.0, NEG_INF)
                _pass1(j, _add_bias(_dot_nt(q_ref[...], kj[...]), bias, NSA_GROUP), s_ref, mx_ref)

            @pl.when(j == qi)
            def _():
                bias = jnp.where(_dot(sel_ref[...], ej[...]) > 0.5, lower, NEG_INF)
                _pass1(j, _add_bias(_dot_nt(q_ref[...], kj[...]), bias, NSA_GROUP), s_ref, mx_ref)

        _row_max(mx_ref, mb_ref)
        for j in range(n_chunks):
            @pl.when(j <= qi)
            def _():
                _pass2(j, svh_ref[0, k, j * chunk:(j + 1) * chunk, :], s_ref, mb_ref, l_ref, acc_ref)

        o_s = _two_pass_out(l_ref, acc_ref)
        _two_pass_init(mx_ref, l_ref, acc_ref)
        for j in range(n_chunks):
            kj = wkh_ref.at[0, k, j * chunk:(j + 1) * chunk, :]

            @pl.when(j == qi)
            def _():
                _pass1(j, _add_bias(_dot_nt(q_ref[...], kj[...]), lower, NSA_GROUP), s_ref, mx_ref)

            @pl.when(j == qi - 1)
            def _():
                _pass1(j, _dot_nt(q_ref[...], kj[...]), s_ref, mx_ref)

            @pl.when(j == qi - 2)
            def _():
                _pass1(j, _add_bias(_dot_nt(q_ref[...], kj[...]), upper, NSA_GROUP), s_ref, mx_ref)

        _row_max(mx_ref, mb_ref)
        for j in range(n_chunks):
            @pl.when((j <= qi) & (j >= qi - 2))
            def _():
                _pass2(j, wvh_ref[0, k, j * chunk:(j + 1) * chunk, :], s_ref, mb_ref, l_ref, acc_ref)

        o_w = _two_pass_out(l_ref, acc_ref)
        for g in range(NSA_GROUP):
            h = k * NSA_GROUP + g
            rs = slice(g * tq, (g + 1) * tq)
            pieces.append(gates[:, 3 * h:3 * h + 1] * o_c[rs] + gates[:, 3 * h + 1:3 * h + 2] * o_s[rs]
                          + gates[:, 3 * h + 2:3 * h + 3] * o_w[rs])
    o_ref[0] = jnp.concatenate(pieces, axis=1).astype(BF16)


def _nsa_prompt(nqnh, nqh, gates, ckc, cvc, skh, svh, wkh, wvh):
    b, _, s, _ = nqnh.shape
    tq = Q_TILE
    chunk = Q_TILE
    assert WINDOW == 2 * chunk
    n_chunks = s // CMP_STRIDE
    n_cmp = n_chunks - CMP_LEN // CMP_STRIDE + 1
    n_sel = -(-s // SEL_LEN)
    nsb_pad = -(-n_sel // 8) * 8
    mselt = jnp.asarray(_cmp_to_sel(n_chunks, n_cmp, nsb_pad, n_sel).T)
    expand = jnp.asarray(_block_expand(nsb_pad, n_sel, SEL_LEN, chunk)).astype(BF16)
    full = lambda n: pl.BlockSpec((1, n, s, HEAD_DIM), lambda i, j: (i, 0, 0, 0))
    qspec = pl.BlockSpec((1, NSA_HEADS, tq, HEAD_DIM), lambda i, j: (i, 0, j, 0))
    cspec = pl.BlockSpec((1, n_chunks, D_NSA_KV), lambda i, j: (i, 0, 0))
    rows = NSA_GROUP * tq
    return pl.pallas_call(
        functools.partial(_nsa_prompt_kernel, n_cmp=n_cmp, n_sel=n_sel),
        grid=(b, s // tq),
        in_specs=[qspec, qspec, pl.BlockSpec((1, tq, LANE), lambda i, j: (i, j, 0)), cspec, cspec,
                  full(NSA_KV_HEADS), full(NSA_KV_HEADS), full(NSA_KV_HEADS), full(NSA_KV_HEADS),
                  pl.BlockSpec(mselt.shape, lambda i, j: (0, 0)),
                  pl.BlockSpec(expand.shape, lambda i, j: (0, 0, 0))],
        out_specs=pl.BlockSpec((1, tq, D_NSA_Q), lambda i, j: (i, j, 0)),
        out_shape=jax.ShapeDtypeStruct((b, s, D_NSA_Q), BF16),
        scratch_shapes=[pltpu.VMEM((tq, nsb_pad), BF16)] + _attn_scratch(rows, s // chunk, chunk),
        compiler_params=pltpu.CompilerParams(dimension_semantics=("arbitrary", "arbitrary"),
                                             vmem_limit_bytes=VMEM_LIMIT),
        name="nsa_prompt",
    )(nqnh, nqh, gates, ckc, cvc, skh, svh, wkh, wvh, mselt, expand)


def _page_copy(cache_ref, buf_ref, sem_ref, pt_ref, b, slot, p):
    return pltpu.make_async_copy(cache_ref.at[pt_ref[b, p]], buf_ref.at[slot, p], sem_ref.at[slot])


def _pages_start(caches, bufs, sems, pt_ref, b, slot, n_pages):
    def body(p, carry):
        for c, bf, sm in zip(caches, bufs, sems):
            _page_copy(c, bf, sm, pt_ref, b, slot, p).start()
        return carry
    lax.fori_loop(0, n_pages, body, 0)


def _pages_wait(caches, bufs, sems, slot, n_pages):
    for c, bf, sm in zip(caches, bufs, sems):
        pltpu.make_async_copy(c.at[pl.ds(0, n_pages)], bf.at[slot], sm.at[slot]).wait()


def _paged_step(caches, bufs, sems, pt_ref, n_pages):
    b = pl.program_id(0)
    nb = pl.num_programs(0)
    slot = b % 2

    @pl.when(b == 0)
    def _():
        _pages_start(caches, bufs, sems, pt_ref, b, slot, n_pages)

    @pl.when(b + 1 < nb)
    def _():
        _pages_start(caches, bufs, sems, pt_ref, b + 1, 1 - slot, n_pages)

    _pages_wait(caches, bufs, sems, slot, n_pages)
    return slot


def _decode_scores(q_t, kbuf, slot, sc_ref, n_kv, group):
    n_pages, _, _, page = kbuf.shape[1:]
    for k in range(n_kv):
        qb = [jnp.broadcast_to(q_t[:, k * group + g:k * group + g + 1], (HEAD_DIM, page)) for g in range(group)]

        def body(p, carry):
            kt = kbuf[slot, p, k]
            for g in range(group):
                h = k * group + g
                sc_ref[p, h:h + 1, :] = jnp.sum(kt * qb[g], axis=0, keepdims=True)
            return carry

        lax.fori_loop(0, n_pages, body, 0, unroll=2)


def _decode_softmax(sc_ref, page_masks, s_new):
    s = jnp.where(jnp.stack(page_masks, axis=0) > 0.5, sc_ref[...], NEG_INF)
    m = jnp.maximum(jnp.max(jnp.max(s, axis=0), axis=1, keepdims=True), s_new)
    e = jnp.exp(s - m[None])
    e_new = jnp.exp(s_new - m)
    inv = 1.0 / jnp.maximum(jnp.sum(jnp.sum(e, axis=0), axis=1, keepdims=True) + e_new, 1e-30)
    sc_ref[...] = e * inv[None]
    return e_new * inv


def _decode_values(vbuf, slot, p_ref, entries, p_new, v_new_t, group):
    cols = []
    for h, ent in enumerate(entries):
        k = h // group
        acc = None
        for pg, idx in ent:
            term = vbuf[slot, pg, k] * p_ref[idx, h:h + 1, :]
            acc = term if acc is None else acc + term
        cols.append(jnp.sum(acc, axis=1, keepdims=True) + p_new[h:h + 1, 0:1] * v_new_t[:, k:k + 1])
    return cols


def _ranked_index(rank, ok, r):
    lane = lax.broadcasted_iota(jnp.int32, rank.shape, 1).astype(F32)
    hit = (rank == float(r)) & ok
    return jnp.sum(jnp.where(hit, lane, 0.0), axis=1, keepdims=True).astype(jnp.int32)


def _place_cols(cols):
    lane = lax.broadcasted_iota(jnp.int32, (HEAD_DIM, len(cols)), 1)
    out = jnp.zeros((HEAD_DIM, len(cols)), F32)
    for h, c in enumerate(cols):
        out = jnp.where(lane == h, c, out)
    return out


def _moba_sample_kernel(pt_ref, qt_ref, q_ref, knew_ref, vnewt_ref, k_hbm, v_hbm, o_ref,
                        kbuf, vbuf, sc_ref, sem_k, sem_v, *, n_blocks):
    n_pages, _, _, page = kbuf.shape[1:]
    slot = _paged_step((k_hbm, v_hbm), (kbuf, vbuf), (sem_k, sem_v), pt_ref, n_pages)
    ppb = MOBA_BLOCK // page
    _decode_scores(qt_ref[0] * SCALE, kbuf, slot, sc_ref, MOBA_KV_HEADS, MOBA_GROUP)
    s_new = jnp.sum(q_ref[0] * knew_ref[0], axis=1, keepdims=True) * SCALE
    lane = lax.broadcasted_iota(jnp.int32, (MOBA_HEADS, LANE), 1)
    sb = jnp.full((MOBA_HEADS, LANE), NEG_INF, F32)
    for j in range(n_blocks):
        tot = sc_ref[j * ppb]
        for r in range(1, ppb):
            tot = tot + sc_ref[j * ppb + r]
        sb = jnp.where(lane == j, jnp.sum(tot, axis=1, keepdims=True) * (1.0 / MOBA_BLOCK), sb)
    rank = _rank(sb, n_blocks, 1)
    in_range = lane < n_blocks
    sel = jnp.where((rank < MOBA_TOPK) & in_range, 1.0, 0.0)
    blk_masks = [jnp.broadcast_to(sel[:, j:j + 1], (MOBA_HEADS, page)) for j in range(n_blocks)]
    p_new = _decode_softmax(sc_ref, [blk_masks[p // ppb] for p in range(n_pages)], s_new)
    tops = [_ranked_index(rank, in_range, r) for r in range(MOBA_TOPK)]
    entries = []
    for h in range(MOBA_HEADS):
        pages = [tops[r][h, 0] * ppb + t for r in range(MOBA_TOPK) for t in range(ppb)]
        entries.append([(pg, pg) for pg in pages])
    cols = _decode_values(vbuf, slot, sc_ref, entries, p_new, vnewt_ref[0], MOBA_GROUP)
    o_ref[0] = _place_cols(cols)


def _native(cache):
    return jnp.transpose(cache, (0, 2, 3, 1))


def _moba_sample(page_table, q, k_new, v_new, cache_k, cache_v):
    db, n_pages = page_table.shape
    page = cache_k.shape[1]
    n_past = n_pages * page
    assert n_past % MOBA_BLOCK == 0 and MOBA_BLOCK % page == 0 and page == LANE
    n_blocks = n_past // MOBA_BLOCK
    assert MOBA_TOPK <= n_blocks <= LANE
    q3 = q.reshape(db, MOBA_HEADS, HEAD_DIM)
    q_t = jnp.transpose(q3, (0, 2, 1))
    k_rows = jnp.repeat(k_new.reshape(db, MOBA_KV_HEADS, HEAD_DIM), MOBA_GROUP, axis=1)
    v_t = jnp.transpose(v_new.reshape(db, MOBA_KV_HEADS, HEAD_DIM), (0, 2, 1))
    per_b = lambda a: pl.BlockSpec((1,) + a.shape[1:], lambda b, pt: (b, 0, 0))
    any_spec = pl.BlockSpec(memory_space=pl.ANY)
    buf = pltpu.VMEM((2, n_pages, MOBA_KV_HEADS, HEAD_DIM, page), F32)
    grid_spec = pltpu.PrefetchScalarGridSpec(
        num_scalar_prefetch=1,
        grid=(db,),
        in_specs=[per_b(q_t), per_b(q3), per_b(k_rows), per_b(v_t), any_spec, any_spec],
        out_specs=pl.BlockSpec((1, HEAD_DIM, MOBA_HEADS), lambda b, pt: (b, 0, 0)),
        scratch_shapes=[buf, buf, pltpu.VMEM((n_pages, MOBA_HEADS, page), F32),
                        pltpu.SemaphoreType.DMA((2,)), pltpu.SemaphoreType.DMA((2,))])
    o_t = pl.pallas_call(
        functools.partial(_moba_sample_kernel, n_blocks=n_blocks),
        grid_spec=grid_spec,
        out_shape=jax.ShapeDtypeStruct((db, HEAD_DIM, MOBA_HEADS), F32),
        compiler_params=pltpu.CompilerParams(dimension_semantics=("arbitrary",), vmem_limit_bytes=VMEM_LIMIT),
        name="moba_sample",
    )(page_table, q_t, q3, k_rows, v_t, _native(cache_k), _native(cache_v))
    return jnp.transpose(o_t, (0, 2, 1)).reshape(db, D_MOBA_Q)


def _row_sort_matrix(page):
    cpp = page // CMP_STRIDE
    out = np.arange(page)
    src = (out % cpp) * CMP_STRIDE + out // cpp
    return (src[:, None] == np.arange(page)[None, :]).astype(np.float32)


def _compress_pages(buf, slot, xs_ref, sort_ref, bias_ref, w1_ref, w2_ref):
    n_pages, _, _, page = buf.shape[1:]
    cpp = page // CMP_STRIDE
    n_chunks = n_pages * cpp
    sort = sort_ref[...]
    first = lax.broadcasted_iota(jnp.int32, (2 * cpp, D_NSA_KV), 1) < HEAD_DIM

    def to_rows(pp, carry):
        ra = _dot_nt(sort, buf[slot, 2 * pp].reshape(D_NSA_KV, page).astype(BF16))
        rb = _dot_nt(sort, buf[slot, 2 * pp + 1].reshape(D_NSA_KV, page).astype(BF16))
        start = pl.multiple_of(pp * 2 * cpp, 2 * cpp)
        for l2 in range(CMP_STRIDE // 2):
            lo, hi = 2 * l2 * cpp, (2 * l2 + 1) * cpp
            even = jnp.concatenate([ra[lo:lo + cpp], rb[lo:lo + cpp]], axis=0)
            odd = jnp.concatenate([ra[hi:hi + cpp], rb[hi:hi + cpp]], axis=0)
            xs_ref[0, l2, pl.ds(start, 2 * cpp), :] = jnp.where(first, even, pltpu.roll(odd, HEAD_DIM, 1)).astype(BF16)
            xs_ref[1, l2, pl.ds(start, 2 * cpp), :] = jnp.where(first, pltpu.roll(even, HEAD_DIM, 1), odd).astype(BF16)
        return carry

    lax.fori_loop(0, n_pages // 2, to_rows, 0, unroll=8)
    x = jnp.concatenate([jnp.concatenate([xs_ref[k, l2] for l2 in range(CMP_STRIDE // 2)], axis=1)
                         for k in range(NSA_KV_HEADS)], axis=0)
    hid = w1_ref.shape[1] // 2
    bias = bias_ref[:, 0:hid] + bias_ref[:, 2 * hid:3 * hid]
    p = _dot(x, w1_ref[...])
    hs = []
    for k in range(NSA_KV_HEADS):
        pk = p[k * n_chunks:(k + 1) * n_chunks]
        hs.append(_silu(pk[:, :hid] + pltpu.roll(pk[:, hid:], n_chunks - 1, 0) + bias).astype(BF16))
    return _dot(jnp.concatenate(hs, axis=1), w2_ref[...])


def _cmp_bias_kernel(kpe_ref, kw_ref, vpe_ref, vw_ref, kb_ref, vb_ref):
    for pe_ref, w_ref, b_ref in ((kpe_ref, kw_ref, kb_ref), (vpe_ref, vw_ref, vb_ref)):
        halves = [jnp.dot(pe_ref[r], w_ref[r], preferred_element_type=F32, precision=HIGHEST) for r in range(2)]
        b_ref[...] = jnp.concatenate(halves, axis=1)


def _cmp_bias(kpe, kw1, vpe, vw1):
    full = lambda a: pl.BlockSpec(a.shape, lambda i: (0,) * a.ndim)
    width = 2 * kw1.shape[2]
    out = pl.BlockSpec((1, width), lambda i: (0, 0))
    return pl.pallas_call(
        _cmp_bias_kernel,
        grid=(1,),
        in_specs=[full(kpe), full(kw1), full(vpe), full(vw1)],
        out_specs=[out, out],
        out_shape=[jax.ShapeDtypeStruct((1, width), F32)] * 2,
        compiler_params=pltpu.CompilerParams(dimension_semantics=("arbitrary",), vmem_limit_bytes=VMEM_LIMIT),
        name="cmp_bias",
    )(kpe, kw1, vpe, vw1)


def _nsa_sample_kernel(pt_ref, qn_ref, qrt_ref, qr_ref, gate_ref, sknew_ref, svnewt_ref, wknew_ref, wknewt_ref, wvnewt_ref,
                       wk_ref, wv_ref, sort_ref, kb_ref, kw1_ref, kw2_ref, vb_ref, vw1_ref, vw2_ref, msel_ref,
                       ck_hbm, cv_hbm, sk_hbm, sv_hbm, o_ref, wko_ref, wvo_ref,
                       ckbuf, cvbuf, skbuf, svbuf, xs_ref, sc_ref, sem_ck, sem_cv, sem_sk, sem_sv, *, n_cmp, n_sel):
    n_pages, _, _, page = skbuf.shape[1:]
    slot = _paged_step((ck_hbm, cv_hbm, sk_hbm, sv_hbm), (ckbuf, cvbuf, skbuf, svbuf),
                       (sem_ck, sem_cv, sem_sk, sem_sv), pt_ref, n_pages)
    n_chunks = n_pages * page // CMP_STRIDE
    qr_t = qrt_ref[0] * SCALE
    gates = gate_ref[0]
    ckc = _compress_pages(ckbuf, slot, xs_ref, sort_ref, kb_ref, kw1_ref, kw2_ref)
    cvc = _compress_pages(cvbuf, slot, xs_ref, sort_ref, vb_ref, vw1_ref, vw2_ref)
    n_idx = lax.broadcasted_iota(jnp.int32, (NSA_HEADS, n_chunks), 1)
    cmask = n_idx < n_cmp
    s = jnp.where(cmask, _dot_nt((qn_ref[0] * SCALE).astype(BF16), ckc.astype(BF16)), NEG_INF)
    m = jnp.max(s, axis=1, keepdims=True)
    e = jnp.where(cmask, jnp.exp(s - m), 0.0)
    p = e / jnp.maximum(jnp.sum(e, axis=1, keepdims=True), 1e-30)
    o_c = _dot(p.astype(BF16), cvc.astype(BF16))
    eye = jnp.where(lax.broadcasted_iota(jnp.int32, (D_NSA_KV, D_NSA_KV), 0)
                    == lax.broadcasted_iota(jnp.int32, (D_NSA_KV, D_NSA_KV), 1), 1.0, 0.0)
    o_c_t = _dot_nt(eye, o_c, precision=HIGHEST)
    p_kv = jnp.concatenate([jnp.sum(p[k * NSA_GROUP:(k + 1) * NSA_GROUP], axis=0, keepdims=True)
                            for k in range(NSA_KV_HEADS)], axis=0)
    nsb_pad = msel_ref.shape[1]
    score = jnp.dot(p_kv, msel_ref[...], preferred_element_type=F32, precision=HIGHEST)
    jb = lax.broadcasted_iota(jnp.int32, (NSA_KV_HEADS, nsb_pad), 1)
    cur = n_sel - 1
    valid = jb < n_sel
    forced = (jb == 0) | (jb == cur) | (jb == cur - 1)
    score = jnp.where(valid, jnp.where(forced, jnp.inf, score), NEG_INF)
    rank = _rank(score, n_sel, 1)
    bpp = page // SEL_LEN
    lane = lax.broadcasted_iota(jnp.int32, (1, page), 1)
    entries = [[] for _ in range(NSA_HEADS)]
    chosen = [_ranked_index(rank, valid, r) for r in range(SEL_TOPN)]
    for k in range(NSA_KV_HEADS):
        qb = [jnp.broadcast_to(qr_t[:, k * NSA_GROUP + g:k * NSA_GROUP + g + 1], (HEAD_DIM, page))
              for g in range(NSA_GROUP)]
        for r in range(SEL_TOPN):
            blk = chosen[r][k, 0]
            pg = jnp.minimum(blk // bpp, n_pages - 1)
            keep = lane // SEL_LEN == jnp.where(blk < cur, blk % bpp, -1)
            kt = skbuf[slot, pg, k]
            for g in range(NSA_GROUP):
                h = k * NSA_GROUP + g
                sc_ref[r, h:h + 1, :] = jnp.where(keep, jnp.sum(kt * qb[g], axis=0, keepdims=True), NEG_INF)
                entries[h].append((pg, r))
    qr = qr_ref[0] * SCALE
    s_new = jnp.sum(qr * sknew_ref[0], axis=1, keepdims=True)
    s = sc_ref[...]
    m = jnp.maximum(jnp.max(jnp.max(s, axis=0), axis=1, keepdims=True), s_new)
    e = jnp.exp(s - m[None])
    e_new = jnp.exp(s_new - m)
    inv = 1.0 / jnp.maximum(jnp.sum(jnp.sum(e, axis=0), axis=1, keepdims=True) + e_new, 1e-30)
    sc_ref[...] = e * inv[None]
    cols_s = _decode_values(svbuf, slot, sc_ref, entries, e_new * inv, svnewt_ref[0], NSA_GROUP)
    w_buf = wk_ref.shape[3]
    widx = lax.broadcasted_iota(jnp.int32, (NSA_HEADS, w_buf), 1)
    hrow = lax.broadcasted_iota(jnp.int32, (NSA_HEADS, w_buf), 0)
    sw = jnp.zeros((NSA_HEADS, w_buf), F32)
    for h in range(NSA_HEADS):
        kt = wk_ref[0, h // NSA_GROUP]
        sw = jnp.where(hrow == h, jnp.sum(kt * qr_t[:, h:h + 1], axis=0, keepdims=True), sw)
    sw = jnp.where(widx > w_buf - WINDOW, sw, NEG_INF)
    sw_new = jnp.sum(qr * wknew_ref[0], axis=1, keepdims=True)
    mw = jnp.maximum(jnp.max(sw, axis=1, keepdims=True), sw_new)
    ew = jnp.exp(sw - mw)
    ew_new = jnp.exp(sw_new - mw)
    invw = 1.0 / jnp.maximum(jnp.sum(ew, axis=1, keepdims=True) + ew_new, 1e-30)
    pw = ew * invw
    pw_new = ew_new * invw
    wvnew_t = wvnewt_ref[0]
    cols = []
    for h in range(NSA_HEADS):
        k = h // NSA_GROUP
        o_w = jnp.sum(wv_ref[0, k] * pw[h:h + 1, :], axis=1, keepdims=True) + pw_new[h:h + 1, 0:1] * wvnew_t[:, k:k + 1]
        o_cmp = o_c_t[k * HEAD_DIM:(k + 1) * HEAD_DIM, h:h + 1]
        cols.append(gates[0:1, h:h + 1] * o_cmp + gates[1:2, h:h + 1] * cols_s[h] + gates[2:3, h:h + 1] * o_w)
    o_ref[0] = _place_cols(cols)
    last = lax.broadcasted_iota(jnp.int32, (HEAD_DIM, w_buf), 1) == w_buf - 1
    wknew_t = wknewt_ref[0]
    for k in range(NSA_KV_HEADS):
        wko_ref[0, k] = jnp.where(last, wknew_t[:, k:k + 1], pltpu.roll(wk_ref[0, k], w_buf - 1, 1))
        wvo_ref[0, k] = jnp.where(last, wvnew_t[:, k:k + 1], pltpu.roll(wv_ref[0, k], w_buf - 1, 1))


def _expand_heads(q, n_heads, n_kv):
    b = q.shape[0]
    group = n_heads // n_kv
    place = jnp.asarray((np.arange(n_heads)[:, None] // group == np.arange(n_kv)[None, :]).astype(np.float32))
    q4 = q.reshape(b, n_heads, 1, HEAD_DIM) * place[None, :, :, None]
    return q4.reshape(b, n_heads, n_kv * HEAD_DIM)


def _nsa_sample(page_table, qn, qr, gates, sk_new, sv_new, wk_new, wv_new, state_wk, state_wv, kw, vw,
                cache_ck, cache_cv, cache_sk, cache_sv):
    db, n_pages = page_table.shape
    page = cache_ck.shape[1]
    n_past = n_pages * page
    assert page == LANE and page % SEL_LEN == 0 and n_pages % 2 == 0
    n_chunks = n_past // CMP_STRIDE
    n_cmp = n_chunks - CMP_LEN // CMP_STRIDE + 1
    n_sel = -(-(n_past + 1) // SEL_LEN)
    assert n_sel >= SEL_TOPN
    nsb_pad = -(-n_sel // LANE) * LANE
    w_buf = state_wk.shape[1]
    heads = lambda a, n: a.reshape(db, n, HEAD_DIM)
    t = lambda a: jnp.transpose(a, (0, 2, 1))
    rep = lambda a: jnp.repeat(heads(a, NSA_KV_HEADS), NSA_GROUP, axis=1)
    qr3 = heads(qr, NSA_HEADS)
    msel = jnp.asarray(_cmp_to_sel(n_chunks, n_cmp, nsb_pad, n_sel))
    small = [_expand_heads(qn, NSA_HEADS, NSA_KV_HEADS), t(qr3), qr3, t(gates.reshape(db, NSA_HEADS, 3)),
             rep(sk_new), t(heads(sv_new, NSA_KV_HEADS)), rep(wk_new), t(heads(wk_new, NSA_KV_HEADS)),
             t(heads(wv_new, NSA_KV_HEADS))]
    kb, vb = _cmp_bias(kw[0], kw[1], vw[0], vw[1])
    def cat(w1):
        one = w1[:, :, :CMP_HIDDEN].reshape(2, CMP_STRIDE, D_NSA_KV, CMP_HIDDEN)[:, :, :HEAD_DIM, :]
        return jnp.transpose(one, (1, 2, 0, 3)).reshape(CMP_STRIDE * HEAD_DIM, 2 * CMP_HIDDEN).astype(BF16)
    consts = [jnp.asarray(_row_sort_matrix(page)).astype(BF16), kb, cat(kw[1]), kw[2], vb, cat(vw[1]), vw[2], msel]
    wk_t, wv_t = _native(state_wk), _native(state_wv)
    per_b = lambda a: pl.BlockSpec((1,) + a.shape[1:], lambda b, pt: (b,) + (0,) * (a.ndim - 1))
    const = lambda a: pl.BlockSpec(a.shape, lambda b, pt: (0,) * a.ndim)
    any_spec = pl.BlockSpec(memory_space=pl.ANY)
    buf = pltpu.VMEM((2, n_pages, NSA_KV_HEADS, HEAD_DIM, page), F32)
    wspec = pl.BlockSpec((1, NSA_KV_HEADS, HEAD_DIM, w_buf), lambda b, pt: (b, 0, 0, 0))
    grid_spec = pltpu.PrefetchScalarGridSpec(
        num_scalar_prefetch=1,
        grid=(db,),
        in_specs=[per_b(a) for a in small] + [wspec, wspec] + [const(a) for a in consts] + [any_spec] * 4,
        out_specs=[pl.BlockSpec((1, HEAD_DIM, NSA_HEADS), lambda b, pt: (b, 0, 0)), wspec, wspec],
        scratch_shapes=[buf, buf, buf, buf,
                        pltpu.VMEM((NSA_KV_HEADS, CMP_STRIDE // 2, n_chunks, D_NSA_KV), BF16),
                        pltpu.VMEM((SEL_TOPN, NSA_HEADS, page), F32)] + [pltpu.SemaphoreType.DMA((2,))] * 4)
    o_t, wko, wvo = pl.pallas_call(
        functools.partial(_nsa_sample_kernel, n_cmp=n_cmp, n_sel=n_sel),
        grid_spec=grid_spec,
        out_shape=[jax.ShapeDtypeStruct((db, HEAD_DIM, NSA_HEADS), F32),
                   jax.ShapeDtypeStruct(wk_t.shape, F32), jax.ShapeDtypeStruct(wv_t.shape, F32)],
        compiler_params=pltpu.CompilerParams(dimension_semantics=("arbitrary",), vmem_limit_bytes=VMEM_LIMIT),
        name="nsa_sample",
    )(page_table, *small, wk_t, wv_t, *consts,
      _native(cache_ck), _native(cache_cv), _native(cache_sk), _native(cache_sv))
    back = lambda a: jnp.transpose(a, (0, 3, 1, 2))
    return jnp.transpose(o_t, (0, 2, 1)).reshape(db, D_NSA_Q), back(wko), back(wvo)


def _ffn_weights(w_in, w_out):
    return w_in.astype(BF16), w_out.astype(BF16)


def kernel(x_prompt, x_sample, cache_moba_k, cache_moba_v, cache_nsa_cmp_k, cache_nsa_cmp_v, cache_nsa_sel_k, cache_nsa_sel_v, state_nsa_win_k, state_nsa_win_v, page_table, c_prompt, c_sample, w_ada, b_ada, norm_ffn1, w_ffn1_in, w_ffn1_out, norm_mix, w_mix_in, w_mix_out, norm_ffn2, w_ffn2_in, w_ffn2_out, cmp_k_pe, cmp_k_w1, cmp_k_w2, cmp_v_pe, cmp_v_w1, cmp_v_w2, norm_final):
    depth = w_ada.shape[0]
    assert depth == 1, "single-layer step"
    b, s, d = x_prompt.shape
    db = x_sample.shape[0]
    assert x_sample.shape[1] == 1
    n_pages = page_table.shape[1]
    page = cache_moba_k.shape[2]
    n_past = n_pages * page
    assert s % Q_TILE == 0 and db % 8 == 0
    l = 0

    ffn1_w = _ffn_weights(w_ffn1_in[l], w_ffn1_out[l])
    ffn2_w = _ffn_weights(w_ffn2_in[l], w_ffn2_out[l])
    w_proj = jnp.pad(w_mix_in[l], ((0, 0), (0, PROJ_PAD - w_mix_in.shape[2]))).astype(BF16)
    w_mo = w_mix_out[l].astype(BF16)
    w_mo_m, w_mo_n = w_mo[:D_MOBA_Q], w_mo[D_MOBA_Q:]
    kw = _compress_weights(cmp_k_pe[l], cmp_k_w1[l], cmp_k_w2[l])
    vw = _compress_weights(cmp_v_pe[l], cmp_v_w1[l], cmp_v_w2[l])

    mods = _ada_mods(jnp.concatenate([c_sample, c_prompt], axis=0), w_ada[l], b_ada[l])
    mods_p = _Mods(mods, db, per_row=False)
    mods_s = _Mods(mods, db, per_row=True)
    xs = x_sample.reshape(1, db, d)
    tm_p = 512 if s % 512 == 0 else Q_TILE

    xp1 = _ffn(x_prompt, mods_p, (0, 1, 2), norm_ffn1[l], *ffn1_w, tm=tm_p)
    xs1 = _ffn(xs, mods_s, (0, 1, 2), norm_ffn1[l], *ffn1_w, tm=db)

    tabs_p = _rope_tables(jnp.arange(s, dtype=jnp.int32))
    pp = _proj(xp1, mods_p, (3, 4), norm_mix[l], w_proj, tabs_p, tm=tm_p, head_major=True)
    (mq, mk, mv, ck, cv, sk, sv, wk, wv, gates,
     mqh, mkh, mvh, nqnh, nqh, skh, svh, wkh, wvh) = pp
    om = _moba_prompt(mq, mqh, mk, mkh, mvh)
    bf = lambda w: (w[0], w[1].astype(BF16), w[2])
    ckc, cvc = _compress_prompt(ck, cv, bf(kw), bf(vw))
    on = _nsa_prompt(nqnh, nqh, gates, ckc, cvc, skh, svh, wkh, wvh)

    tabs_s = _rope_tables(jnp.full((db,), n_past, dtype=jnp.int32))
    ps = _proj(xs1, mods_s, (3, 4), norm_mix[l], w_proj, tabs_s, tm=db, head_major=False)
    (smq, smk, smv, sck, scv, ssk, ssv, swk, swv, sgates, snqn, snq) = [a[0] for a in ps]
    o_m_s = _moba_sample(page_table, smq, smk, smv, cache_moba_k[l], cache_moba_v[l])
    o_n_s, win_k_s, win_v_s = _nsa_sample(
        page_table, snqn, snq, sgates[:, :N_GATES], ssk, ssv, swk, swv,
        state_nsa_win_k[l], state_nsa_win_v[l], kw, vw,
        cache_nsa_cmp_k[l], cache_nsa_cmp_v[l], cache_nsa_sel_k[l], cache_nsa_sel_v[l])
    om_s = o_m_s.reshape(1, db, D_MOBA_Q).astype(BF16)
    on_s = o_n_s.reshape(1, db, D_NSA_Q).astype(BF16)

    yp = _ffn(xp1, mods_p, (6, 7, 8), norm_ffn2[l], *ffn2_w, tm=tm_p,
              mix=(om, on, w_mo_m, w_mo_n, 5), final_g=norm_final)
    ys = _ffn(xs1, mods_s, (6, 7, 8), norm_ffn2[l], *ffn2_w, tm=db,
              mix=(om_s, on_s, w_mo_m, w_mo_n, 5), final_g=norm_final)

    w_keep = min(WINDOW, s)
    st = lambda a, n: a.reshape(1, b, a.shape[1], n, HEAD_DIM)
    ss_ = lambda a, n: a.reshape(1, db, 1, n, HEAD_DIM)
    return (yp, ys.reshape(db, 1, d),
            st(mk, MOBA_KV_HEADS), ss_(smk, MOBA_KV_HEADS), st(mv, MOBA_KV_HEADS), ss_(smv, MOBA_KV_HEADS),
            st(ck, NSA_KV_HEADS), ss_(sck, NSA_KV_HEADS), st(cv, NSA_KV_HEADS), ss_(scv, NSA_KV_HEADS),
            st(sk, NSA_KV_HEADS), ss_(ssk, NSA_KV_HEADS), st(sv, NSA_KV_HEADS), ss_(ssv, NSA_KV_HEADS),
            st(wk[:, s - w_keep:], NSA_KV_HEADS), win_k_s[None],
            st(wv[:, s - w_keep:], NSA_KV_HEADS), win_v_s[None])
```

```python
import functools

import numpy as np
import jax
import jax.numpy as jnp
from jax import lax
from jax.experimental import pallas as pl
from jax.experimental.pallas import tpu as pltpu

F32 = jnp.float32
BF16 = jnp.bfloat16
NEG_INF = float("-inf")
HIGHEST = lax.Precision.HIGHEST

HEAD_DIM = 64
MOBA_HEADS = 8
MOBA_KV_HEADS = 4
MOBA_GROUP = MOBA_HEADS // MOBA_KV_HEADS
NSA_HEADS = 8
NSA_KV_HEADS = 2
NSA_GROUP = NSA_HEADS // NSA_KV_HEADS
ROPE_DIM = HEAD_DIM // 4
ROPE_THETA = 500000.0
MOBA_BLOCK = 256
MOBA_TOPK = 3
CMP_LEN = 32
CMP_STRIDE = 16
CMP_HIDDEN = 2 * HEAD_DIM
SEL_LEN = 64
SEL_TOPN = 16
WINDOW = 512
MACARON_WEIGHT = 0.5
N_MOD = 9
EPS = 1e-6
SCALE = HEAD_DIM ** -0.5
D_MOBA_Q = MOBA_HEADS * HEAD_DIM
D_MOBA_KV = MOBA_KV_HEADS * HEAD_DIM
D_NSA_Q = NSA_HEADS * HEAD_DIM
D_NSA_KV = NSA_KV_HEADS * HEAD_DIM
N_GATES = 3 * NSA_HEADS
LANE = 128
PROJ_PAD = 2432
VMEM_LIMIT = 56 * 1024 * 1024
Q_TILE = 256
FFN_CHUNK = 256
MOBA_KV_PER_PASS = 4

OFF_MQ, OFF_MK, OFF_MV, OFF_NQ = 0, 512, 768, 1024
OFF_CK, OFF_CV, OFF_SK, OFF_SV, OFF_WK, OFF_WV, OFF_G = 1536, 1664, 1792, 1920, 2048, 2176, 2304


def _silu(x):
    return x / (1.0 + jnp.exp(-x))


def _sigmoid(x):
    return 1.0 / (1.0 + jnp.exp(-x))


def _dot(a, b):
    return jnp.dot(a, b, preferred_element_type=F32)


def _dot_nt(a, b, precision=None):
    return lax.dot_general(a, b, (((1,), (1,)), ((), ())), preferred_element_type=F32, precision=precision)


def _modulated(x, g, shift, scale):
    ms = jnp.mean(x * x, axis=-1, keepdims=True)
    return (x * lax.rsqrt(ms + EPS) * g) * (1.0 + scale) + shift


def _rank(sc, ncols, axis):
    idx = lax.broadcasted_iota(jnp.int32, sc.shape, axis)
    rank = jnp.zeros(sc.shape, F32)
    for i in range(ncols):
        ci = sc[:, i:i + 1] if axis == 1 else sc[i:i + 1, :]
        beats = (ci > sc) | ((ci == sc) & (idx > i))
        rank = rank + jnp.where(beats, 1.0, 0.0)
    return rank


def _ada_kernel(c_ref, w_ref, b_ref, o_ref):
    sc = _silu(c_ref[...]).astype(BF16)
    o_ref[0] = _dot(sc, w_ref[...].astype(BF16)) + b_ref[0]


def _ada_mods(c_all, w_ada, b_ada):
    rows, d = c_all.shape
    return pl.pallas_call(
        _ada_kernel,
        grid=(N_MOD,),
        in_specs=[pl.BlockSpec((rows, d), lambda j: (0, 0)),
                  pl.BlockSpec((d, d), lambda j: (0, j)),
                  pl.BlockSpec((1, 1, d), lambda j: (j, 0, 0))],
        out_specs=pl.BlockSpec((1, rows, d), lambda j: (j, 0, 0)),
        out_shape=jax.ShapeDtypeStruct((N_MOD, rows, d), F32),
        compiler_params=pltpu.CompilerParams(dimension_semantics=("arbitrary",), vmem_limit_bytes=VMEM_LIMIT),
        name="ada_mods",
    )(c_all, w_ada, b_ada.reshape(N_MOD, 1, d))


class _Mods:
    def __init__(self, mods, n_sample, per_row):
        self.per_row = per_row
        self.n_sample = n_sample
        self.rows = mods.shape[1]
        d = mods.shape[2]
        self.d = d
        self.arr = mods if per_row else mods.reshape(N_MOD * self.rows, 1, d)

    def spec(self, j):
        if self.per_row:
            return pl.BlockSpec((1, self.n_sample, self.d), lambda g, i: (j, 0, 0))
        base = j * self.rows + self.n_sample
        return pl.BlockSpec((1, 1, self.d), lambda g, i: (base + g, 0, 0))


def _ffn_kernel(*refs, has_mix, final_norm, n_chunks):
    it = iter(refs)
    x_ref = next(it)
    if has_mix:
        mm_ref, mn_ref, wmm_ref, wmn_ref, gmix_ref = next(it), next(it), next(it), next(it), next(it)
    sh_ref, sc_ref, gt_ref, g_ref, win_ref, wo_ref = (next(it) for _ in range(6))
    gf_ref = next(it) if final_norm else None
    o_ref, h_ref = next(it), next(it)

    x = x_ref[0]
    if has_mix:
        x = x + gmix_ref[0] * (_dot(mm_ref[0], wmm_ref[...]) + _dot(mn_ref[0], wmn_ref[...]))
    xm = _modulated(x, g_ref[...], sh_ref[0], sc_ref[0]).astype(BF16)
    f = wo_ref.shape[0]
    tf = f // n_chunks
    for j in range(n_chunks):
        a = _dot(xm, win_ref[:, j * tf:(j + 1) * tf])
        b = _dot(xm, win_ref[:, f + j * tf:f + (j + 1) * tf])
        h_ref[:, j * tf:(j + 1) * tf] = (_silu(a) * b).astype(BF16)
    y = x + (MACARON_WEIGHT * gt_ref[0]) * _dot(h_ref[...], wo_ref[...])
    if final_norm:
        ms = jnp.mean(y * y, axis=-1, keepdims=True)
        y = y * lax.rsqrt(ms + EPS) * gf_ref[...]
    o_ref[0] = y


def _ffn(x, mods, jmods, norm_g, win, wo, tm, mix=None, final_g=None):
    g_, r_, d = x.shape
    f = wo.shape[0]
    assert f % FFN_CHUNK == 0
    n_chunks = f // FFN_CHUNK
    const2 = lambda g, i: (0, 0)
    row_spec = lambda w: pl.BlockSpec((1, tm, w), lambda g, i: (g, i, 0))
    args, specs = [x], [row_spec(d)]
    if mix is not None:
        mm, mn, wmm, wmn, jmix = mix
        args += [mm, mn, wmm, wmn, mods.arr]
        specs += [row_spec(mm.shape[2]), row_spec(mn.shape[2]),
                  pl.BlockSpec(wmm.shape, const2), pl.BlockSpec(wmn.shape, const2), mods.spec(jmix)]
    args += [mods.arr, mods.arr, mods.arr, norm_g.reshape(1, d), win, wo]
    specs += [mods.spec(jmods[0]), mods.spec(jmods[1]), mods.spec(jmods[2]),
              pl.BlockSpec((1, d), const2), pl.BlockSpec(win.shape, const2), pl.BlockSpec(wo.shape, const2)]
    if final_g is not None:
        args.append(final_g.reshape(1, d))
        specs.append(pl.BlockSpec((1, d), const2))
    kern = functools.partial(_ffn_kernel, has_mix=mix is not None, final_norm=final_g is not None, n_chunks=n_chunks)
    return pl.pallas_call(
        kern,
        grid=(g_, r_ // tm),
        in_specs=specs,
        out_specs=row_spec(d),
        out_shape=jax.ShapeDtypeStruct((g_, r_, d), F32),
        scratch_shapes=[pltpu.VMEM((tm, f), BF16)],
        compiler_params=pltpu.CompilerParams(dimension_semantics=("arbitrary", "arbitrary"),
                                             vmem_limit_bytes=VMEM_LIMIT),
        name="ffn",
    )(*args)


def _rope_tables(pos):
    half = ROPE_DIM // 2
    inv_freq = ROPE_THETA ** (-jnp.arange(half, dtype=F32) / half)
    ang = pos.astype(F32)[:, None] * inv_freq
    cos, sin = jnp.cos(ang), jnp.sin(ang)
    rows = pos.shape[0]
    rest = HEAD_DIM - ROPE_DIM
    one, zero = jnp.ones((rows, rest), F32), jnp.zeros((rows, rest), F32)
    zh = jnp.zeros((rows, half), F32)
    c = jnp.concatenate([cos, cos, one], axis=1)
    sa = jnp.concatenate([-sin, zh, zero], axis=1)
    sb = jnp.concatenate([zh, sin, zero], axis=1)
    tile = lambda t: jnp.concatenate([t, t], axis=1)
    return tile(c), tile(sa), tile(sb)


def _proj_kernel(*refs, head_major):
    (x_ref, sh_ref, sc_ref, g_ref, w_ref, cos_ref, sa_ref, sb_ref) = refs[:8]
    outs = refs[8:]
    (mq_ref, mk_ref, mv_ref, ck_ref, cv_ref, sk_ref, sv_ref, wk_ref, wv_ref, gate_ref) = outs[:10]
    xm = _modulated(x_ref[0], g_ref[...], sh_ref[0], sc_ref[0]).astype(BF16)
    cos, sa, sb = cos_ref[...], sa_ref[...], sb_ref[...]
    half = ROPE_DIM // 2

    def seg(lo, width):
        return _dot(xm, w_ref[:, lo:lo + width])

    def rope(y):
        parts = []
        for c in range(y.shape[1] // LANE):
            s = y[:, c * LANE:(c + 1) * LANE]
            parts.append(s * cos + pltpu.roll(s, LANE - half, 1) * sa + pltpu.roll(s, half, 1) * sb)
        return parts[0] if len(parts) == 1 else jnp.concatenate(parts, axis=1)

    mq = rope(seg(OFF_MQ, D_MOBA_Q))
    mk = rope(seg(OFF_MK, D_MOBA_KV))
    mv = seg(OFF_MV, D_MOBA_KV)
    nqn = seg(OFF_NQ, D_NSA_Q)
    nq = rope(nqn)
    ck = seg(OFF_CK, D_NSA_KV)
    cv = seg(OFF_CV, D_NSA_KV)
    sk = rope(seg(OFF_SK, D_NSA_KV))
    sv = seg(OFF_SV, D_NSA_KV)
    wk = rope(seg(OFF_WK, D_NSA_KV))
    wv = seg(OFF_WV, D_NSA_KV)
    gate_ref[0] = _sigmoid(seg(OFF_G, LANE))
    mq_ref[0], mk_ref[0], mv_ref[0] = mq, mk, mv
    ck_ref[0], cv_ref[0], sk_ref[0], sv_ref[0], wk_ref[0], wv_ref[0] = ck, cv, sk, sv, wk, wv
    if not head_major:
        nqn_ref, nq_ref = outs[10:]
        nqn_ref[0], nq_ref[0] = nqn, nq
    else:
        (mqh_ref, mkh_ref, mvh_ref, nqnh_ref, nqh_ref, skh_ref, svh_ref, wkh_ref, wvh_ref) = outs[10:]

        def put(ref, val, n, scale):
            for h in range(n):
                piece = val[:, h * HEAD_DIM:(h + 1) * HEAD_DIM]
                ref[0, h] = (piece * scale if scale != 1.0 else piece).astype(BF16)

        put(mqh_ref, mq, MOBA_HEADS, SCALE)
        put(mkh_ref, mk, MOBA_KV_HEADS, 1.0)
        put(mvh_ref, mv, MOBA_KV_HEADS, 1.0)
        put(nqnh_ref, nqn, NSA_HEADS, SCALE)
        put(nqh_ref, nq, NSA_HEADS, SCALE)
        put(skh_ref, sk, NSA_KV_HEADS, 1.0)
        put(svh_ref, sv, NSA_KV_HEADS, 1.0)
        put(wkh_ref, wk, NSA_KV_HEADS, 1.0)
        put(wvh_ref, wv, NSA_KV_HEADS, 1.0)


def _proj(x, mods, jmods, norm_g, w_pad, tables, tm, head_major):
    g_, r_, d = x.shape
    const2 = lambda g, i: (0, 0)
    row_spec = lambda w: pl.BlockSpec((1, tm, w), lambda g, i: (g, i, 0))
    tab_spec = pl.BlockSpec((tm, LANE), lambda g, i: (i, 0))
    in_specs = [row_spec(d), mods.spec(jmods[0]), mods.spec(jmods[1]), pl.BlockSpec((1, d), const2),
                pl.BlockSpec(w_pad.shape, const2), tab_spec, tab_spec, tab_spec]
    widths = [D_MOBA_Q, D_MOBA_KV, D_MOBA_KV] + [D_NSA_KV] * 6 + [LANE]
    if not head_major:
        widths += [D_NSA_Q, D_NSA_Q]
    out_specs = [row_spec(w) for w in widths]
    out_shape = [jax.ShapeDtypeStruct((g_, r_, w), F32) for w in widths]
    if head_major:
        for n in (MOBA_HEADS, MOBA_KV_HEADS, MOBA_KV_HEADS, NSA_HEADS, NSA_HEADS) + (NSA_KV_HEADS,) * 4:
            out_specs.append(pl.BlockSpec((1, n, tm, HEAD_DIM), lambda g, i: (g, 0, i, 0)))
            out_shape.append(jax.ShapeDtypeStruct((g_, n, r_, HEAD_DIM), BF16))
    return pl.pallas_call(
        functools.partial(_proj_kernel, head_major=head_major),
        grid=(g_, r_ // tm),
        in_specs=in_specs,
        out_specs=out_specs,
        out_shape=out_shape,
        compiler_params=pltpu.CompilerParams(dimension_semantics=("arbitrary", "arbitrary"),
                                             vmem_limit_bytes=VMEM_LIMIT),
        name="proj",
    )(x, mods.arr, mods.arr, norm_g.reshape(1, d), w_pad, *tables)


def _compress_mlp(p0, p1, w2_ref):
    h = p0 + pltpu.roll(p1, p0.shape[0] - 1, 0)
    return _dot(_silu(h).astype(BF16), w2_ref[...])


def _compress_rows(x, pe_ref, w1_ref, w2_ref):
    p0 = _dot((x + pe_ref[0]).astype(BF16), w1_ref[0])
    p1 = _dot((x + pe_ref[1]).astype(BF16), w1_ref[1])
    return _compress_mlp(p0, p1, w2_ref)


def _compress_kernel(ck_ref, cv_ref, kpe_ref, kw1_ref, kw2_ref, vpe_ref, vw1_ref, vw2_ref, okc_ref, ovc_ref):
    okc_ref[0] = _compress_rows(ck_ref[0], kpe_ref, kw1_ref, kw2_ref)
    ovc_ref[0] = _compress_rows(cv_ref[0], vpe_ref, vw1_ref, vw2_ref)


def _compress_weights(pe, w1, w2):
    ratio = CMP_LEN // CMP_STRIDE
    eye = jnp.eye(NSA_KV_HEADS, dtype=F32)
    w1r = w1.reshape(ratio, CMP_STRIDE, HEAD_DIM, CMP_HIDDEN)
    w1f = jnp.einsum("rlde,kK->rlkdKe", w1r, eye).reshape(ratio, CMP_STRIDE * D_NSA_KV, NSA_KV_HEADS * CMP_HIDDEN)
    pef = jnp.broadcast_to(pe.reshape(ratio, CMP_STRIDE, 1, HEAD_DIM),
                           (ratio, CMP_STRIDE, NSA_KV_HEADS, HEAD_DIM)).reshape(ratio, 1, CMP_STRIDE * D_NSA_KV)
    w2f = jnp.einsum("ed,kK->keKd", w2, eye).reshape(NSA_KV_HEADS * CMP_HIDDEN, D_NSA_KV)
    return pef, w1f, w2f.astype(BF16)


def _compress_prompt(ck, cv, kw, vw):
    b, s, _ = ck.shape
    n = s // CMP_STRIDE
    width = CMP_STRIDE * D_NSA_KV
    ck2, cv2 = ck.reshape(b, n, width), cv.reshape(b, n, width)
    row = pl.BlockSpec((1, n, width), lambda i: (i, 0, 0))
    wspecs = []
    for w in kw + vw:
        wspecs.append(pl.BlockSpec(w.shape, (lambda i: (0, 0, 0)) if w.ndim == 3 else (lambda i: (0, 0))))
    out = pl.BlockSpec((1, n, D_NSA_KV), lambda i: (i, 0, 0))
    return pl.pallas_call(
        _compress_kernel,
        grid=(b,),
        in_specs=[row, row] + wspecs,
        out_specs=[out, out],
        out_shape=[jax.ShapeDtypeStruct((b, n, D_NSA_KV), F32)] * 2,
        compiler_params=pltpu.CompilerParams(dimension_semantics=("arbitrary",), vmem_limit_bytes=VMEM_LIMIT),
        name="compress_prompt",
    )(ck2, cv2, *kw, *vw)


def _two_pass_init(mx_ref, l_ref, acc_ref):
    mx_ref[...] = jnp.full(mx_ref.shape, NEG_INF, F32)
    l_ref[...] = jnp.zeros(l_ref.shape, F32)
    acc_ref[...] = jnp.zeros(acc_ref.shape, F32)


def _pass1(j, s, s_ref, mx_ref):
    s_ref[j] = s
    m = s[:, 0:LANE]
    for c in range(1, s.shape[1] // LANE):
        m = jnp.maximum(m, s[:, c * LANE:(c + 1) * LANE])
    mx_ref[...] = jnp.maximum(mx_ref[...], m)


def _row_max(mx_ref, mb_ref):
    m = jnp.max(mx_ref[...], axis=1, keepdims=True)
    m = jnp.where(m == NEG_INF, 0.0, m)
    mb_ref[...] = jnp.broadcast_to(m, mb_ref.shape)


def _pass2(j, vs, s_ref, mb_ref, l_ref, acc_ref):
    mb = mb_ref[...]
    s = s_ref[j]
    parts = [jnp.exp(s[:, c * LANE:(c + 1) * LANE] - mb) for c in range(s.shape[1] // LANE)]
    tot = parts[0]
    for p in parts[1:]:
        tot = tot + p
    l_ref[...] += tot
    p = jnp.concatenate(parts, axis=1).astype(BF16)
    n = p.shape[0] // len(vs)
    upd = [_dot(p[i * n:(i + 1) * n], v) for i, v in enumerate(vs)]
    acc_ref[...] += upd[0] if len(upd) == 1 else jnp.concatenate(upd, axis=0)


def _two_pass_out(l_ref, acc_ref):
    l = jnp.sum(l_ref[...], axis=1, keepdims=True)
    return acc_ref[...] / jnp.maximum(l, 1e-30)


def _add_bias(s, bias, groups):
    tq = bias.shape[0]
    return (s.reshape(groups, tq, s.shape[1]) + bias[None]).reshape(s.shape)


def _attn_scratch(rows, n_chunks, chunk):
    return [pltpu.VMEM((rows, HEAD_DIM), BF16),
            pltpu.VMEM((n_chunks, rows, chunk), F32),
            pltpu.VMEM((rows, LANE), F32),
            pltpu.VMEM((rows, LANE), F32),
            pltpu.VMEM((rows, LANE), F32),
            pltpu.VMEM((rows, HEAD_DIM), F32)]


def _moba_prompt_kernel(mq_ref, mqh_ref, mk_ref, mkh_ref, mvh_ref, o_ref,
                        kmean_ref, bias_ref, q_ref, s_ref, mx_ref, mb_ref, l_ref, acc_ref, *, n_blocks):
    qi = pl.program_id(1)
    tq = mq_ref.shape[1]
    per_kv = MOBA_GROUP * tq
    rows = MOBA_KV_PER_PASS * per_kv

    @pl.when(qi == 0)
    def _():
        for j in range(n_blocks):
            blk = mk_ref[0, j * MOBA_BLOCK:(j + 1) * MOBA_BLOCK, :]
            kmean_ref[j:j + 1, :] = jnp.sum(blk, axis=0, keepdims=True) * (1.0 / MOBA_BLOCK)

    blk_t = lax.broadcasted_iota(jnp.int32, (n_blocks, tq), 0)
    r_loc = lax.broadcasted_iota(jnp.int32, (rows, MOBA_BLOCK), 0) % tq
    c_loc = lax.broadcasted_iota(jnp.int32, (rows, MOBA_BLOCK), 1)
    pad = jnp.full((LANE - n_blocks, tq), NEG_INF, F32)

    pieces = []
    for k0 in range(0, MOBA_KV_HEADS, MOBA_KV_PER_PASS):
        kvs = list(range(k0, k0 + MOBA_KV_PER_PASS))
        for k in kvs:
            km = kmean_ref[:, k * HEAD_DIM:(k + 1) * HEAD_DIM]
            for g in range(MOBA_GROUP):
                h = k * MOBA_GROUP + g
                lo = (h - k0 * MOBA_GROUP) * tq
                qf = mq_ref[0, :, h * HEAD_DIM:(h + 1) * HEAD_DIM]
                sb = jnp.where(blk_t < qi, _dot_nt(km, qf, precision=HIGHEST), NEG_INF)
                sel = (_rank(sb, n_blocks, 0) < MOBA_TOPK) & (blk_t < qi)
                bias_t = jnp.concatenate([jnp.where(sel, 0.0, NEG_INF), pad], axis=0)
                bias_ref[lo:lo + tq, :] = bias_t.T
                q_ref[lo:lo + tq, :] = mqh_ref[0, h]
        _two_pass_init(mx_ref, l_ref, acc_ref)

        def scores(j):
            ks = slice(j * MOBA_BLOCK, (j + 1) * MOBA_BLOCK)
            return jnp.concatenate([_dot_nt(q_ref[i * per_kv:(i + 1) * per_kv, :], mkh_ref[0, k, ks, :])
                                    for i, k in enumerate(kvs)], axis=0)

        for j in range(n_blocks):
            @pl.when(j < qi)
            def _():
                _pass1(j, scores(j) + bias_ref[:, j:j + 1], s_ref, mx_ref)

            @pl.when(j == qi)
            def _():
                _pass1(j, jnp.where(c_loc <= r_loc, scores(j), NEG_INF), s_ref, mx_ref)

        _row_max(mx_ref, mb_ref)
        for j in range(n_blocks):
            @pl.when(j <= qi)
            def _():
                vs = [mvh_ref[0, k, j * MOBA_BLOCK:(j + 1) * MOBA_BLOCK, :] for k in kvs]
                _pass2(j, vs, s_ref, mb_ref, l_ref, acc_ref)

        o = _two_pass_out(l_ref, acc_ref)
        pieces += [o[i * tq:(i + 1) * tq, :] for i in range(MOBA_KV_PER_PASS * MOBA_GROUP)]
    o_ref[0] = jnp.concatenate(pieces, axis=1).astype(BF16)


def _moba_prompt(mq, mqh, mk, mkh, mvh):
    b, s, _ = mq.shape
    tq = Q_TILE
    assert tq == MOBA_BLOCK
    n_blocks = s // MOBA_BLOCK
    rows = MOBA_KV_PER_PASS * MOBA_GROUP * tq
    full = lambda n: pl.BlockSpec((1, n, s, HEAD_DIM), lambda i, j: (i, 0, 0, 0))
    return pl.pallas_call(
        functools.partial(_moba_prompt_kernel, n_blocks=n_blocks),
        grid=(b, s // tq),
        in_specs=[pl.BlockSpec((1, tq, D_MOBA_Q), lambda i, j: (i, j, 0)),
                  pl.BlockSpec((1, MOBA_HEADS, tq, HEAD_DIM), lambda i, j: (i, 0, j, 0)),
                  pl.BlockSpec((1, s, D_MOBA_KV), lambda i, j: (i, 0, 0)),
                  full(MOBA_KV_HEADS), full(MOBA_KV_HEADS)],
        out_specs=pl.BlockSpec((1, tq, D_MOBA_Q), lambda i, j: (i, j, 0)),
        out_shape=jax.ShapeDtypeStruct((b, s, D_MOBA_Q), BF16),
        scratch_shapes=[pltpu.VMEM((n_blocks, D_MOBA_KV), F32),
                        pltpu.VMEM((rows, LANE), F32)] + _attn_scratch(rows, n_blocks, MOBA_BLOCK),
        compiler_params=pltpu.CompilerParams(dimension_semantics=("arbitrary", "arbitrary"),
                                             vmem_limit_bytes=VMEM_LIMIT),
        name="moba_prompt",
    )(mq, mqh, mk, mkh, mvh)


def _cmp_to_sel(nc_pad, nc, nsb_pad, nsb):
    i = np.arange(nc_pad)[:, None]
    j = np.arange(nsb_pad)[None, :]
    start = i * CMP_STRIDE
    m = (start < (j + 1) * SEL_LEN) & (start + CMP_LEN > j * SEL_LEN) & (i < nc) & (j < nsb)
    return m.astype(np.float32)


def _block_expand(n_blocks_pad, n_blocks, block_len, chunk):
    n_chunks = n_blocks * block_len // chunk
    pos = np.arange(n_chunks * chunk).reshape(n_chunks, 1, chunk)
    j = np.arange(n_blocks_pad).reshape(1, n_blocks_pad, 1)
    return (pos // block_len == j).astype(np.float32)


def _nsa_prompt_kernel(nqnh_ref, nqh_ref, gate_ref, ckc_ref, cvc_ref, skh_ref, svh_ref, wkh_ref, wvh_ref, mselt_ref, exp_ref,
                       o_ref, sel_ref, oc_ref, os_ref, q_ref, s_ref, mx_ref, mb_ref, l_ref, acc_ref, *, n_cmp, n_sel):
    qi = pl.program_id(1)
    tq = nqnh_ref.shape[2]
    rows = NSA_GROUP * tq
    nc_pad = ckc_ref.shape[1]
    nsb_pad = mselt_ref.shape[0]
    n_chunks, _, chunk = exp_ref.shape
    q0 = qi * tq
    gates = gate_ref[0]

    qpos_c = q0 + lax.broadcasted_iota(jnp.int32, (rows, nc_pad), 0) % tq
    n_idx = lax.broadcasted_iota(jnp.int32, (rows, nc_pad), 1)
    cmp_mask = (n_idx * CMP_STRIDE + (CMP_LEN - 1) <= qpos_c) & (n_idx < n_cmp)

    t = q0 + lax.broadcasted_iota(jnp.int32, (nsb_pad, tq), 1)
    jb = lax.broadcasted_iota(jnp.int32, (nsb_pad, tq), 0)
    cur = t // SEL_LEN
    valid = (jb * SEL_LEN <= t) & (jb < n_sel)
    forced = (jb == 0) | (jb == cur) | (jb == cur - 1)
    pad = jnp.zeros((LANE - nsb_pad, tq), F32)

    r_loc = lax.broadcasted_iota(jnp.int32, (tq, chunk), 0)
    c_loc = lax.broadcasted_iota(jnp.int32, (tq, chunk), 1)
    lower = jnp.where(c_loc <= r_loc, 0.0, NEG_INF)
    upper = jnp.where(c_loc > r_loc, 0.0, NEG_INF)

    kvs = range(NSA_KV_HEADS)
    for k in kvs:
        lo = k * HEAD_DIM
        for g in range(NSA_GROUP):
            q_ref[g * tq:(g + 1) * tq, :] = nqnh_ref[0, k * NSA_GROUP + g]
        kc = ckc_ref[0, :, lo:lo + HEAD_DIM].astype(BF16)
        vc = cvc_ref[0, :, lo:lo + HEAD_DIM].astype(BF16)
        s = jnp.where(cmp_mask, _dot_nt(q_ref[0:rows, :], kc), NEG_INF)
        m = jnp.max(s, axis=1, keepdims=True)
        m = jnp.where(m == NEG_INF, 0.0, m)
        e = jnp.where(cmp_mask, jnp.exp(s - m), 0.0)
        p = e / jnp.maximum(jnp.sum(e, axis=1, keepdims=True), 1e-30)
        oc_ref[k * rows:(k + 1) * rows, :] = _dot(p.astype(BF16), vc)
        p_kv = p[0:tq]
        for g in range(1, NSA_GROUP):
            p_kv = p_kv + p[g * tq:(g + 1) * tq]
        score = _dot_nt(mselt_ref[...], p_kv, precision=HIGHEST)
        score = jnp.where(valid, jnp.where(forced, jnp.inf, score), NEG_INF)
        sel = (_rank(score, n_sel, 0) < SEL_TOPN) & valid
        sel_t = jnp.concatenate([jnp.where(sel, 1.0, 0.0), pad], axis=0)
        sel_ref[k] = sel_t.T[:, :nsb_pad].astype(BF16)
    for h in range(NSA_HEADS):
        q_ref[h * tq:(h + 1) * tq, :] = nqh_ref[0, h]

    def scores(k_ref, j, biases):
        ks = slice(j * chunk, (j + 1) * chunk)
        out = []
        for k in kvs:
            s = _dot_nt(q_ref[k * rows:(k + 1) * rows, :], k_ref[0, k, ks, :])
            out.append(s if biases is None else _add_bias(s, biases[k], NSA_GROUP))
        return jnp.concatenate(out, axis=0)

    def sel_bias(j, inside):
        return [jnp.where(_dot(sel_ref[k], exp_ref[j]) > 0.5, inside, NEG_INF) for k in kvs]

    _two_pass_init(mx_ref, l_ref, acc_ref)
    for j in range(n_chunks):
        @pl.when(j < qi)
        def _():
            _pass1(j, scores(skh_ref, j, sel_bias(j, 0.0)), s_ref, mx_ref)

        @pl.when(j == qi)
        def _():
            _pass1(j, scores(skh_ref, j, sel_bias(j, lower)), s_ref, mx_ref)

    _row_max(mx_ref, mb_ref)
    for j in range(n_chunks):
        @pl.when(j <= qi)
        def _():
            _pass2(j, [svh_ref[0, k, j * chunk:(j + 1) * chunk, :] for k in kvs], s_ref, mb_ref, l_ref, acc_ref)

    os_ref[...] = _two_pass_out(l_ref, acc_ref)
    _two_pass_init(mx_ref, l_ref, acc_ref)
    for j in range(n_chunks):
        @pl.when(j == qi)
        def _():
            _pass1(j, scores(wkh_ref, j, [lower] * NSA_KV_HEADS), s_ref, mx_ref)

        @pl.when(j == qi - 1)
        def _():
            _pass1(j, scores(wkh_ref, j, None), s_ref, mx_ref)

        @pl.when(j == qi - 2)
        def _():
            _pass1(j, scores(wkh_ref, j, [upper] * NSA_KV_HEADS), s_ref, mx_ref)

    _row_max(mx_ref, mb_ref)
    for j in range(n_chunks):
        @pl.when((j <= qi) & (j >= qi - 2))
        def _():
            _pass2(j, [wvh_ref[0, k, j * chunk:(j + 1) * chunk, :] for k in kvs], s_ref, mb_ref, l_ref, acc_ref)

    o_w = _two_pass_out(l_ref, acc_ref)
    pieces = []
    for h in range(NSA_HEADS):
        rs = slice(h * tq, (h + 1) * tq)
        pieces.append(gates[:, 3 * h:3 * h + 1] * oc_ref[rs, :] + gates[:, 3 * h + 1:3 * h + 2] * os_ref[rs, :]
                      + gates[:, 3 * h + 2:3 * h + 3] * o_w[rs])
    o_ref[0] = jnp.concatenate(pieces, axis=1).astype(BF16)


def _nsa_prompt(nqnh, nqh, gates, ckc, cvc, skh, svh, wkh, wvh):
    b, _, s, _ = nqnh.shape
    tq = Q_TILE
    chunk = Q_TILE
    assert WINDOW == 2 * chunk
    n_chunks = s // CMP_STRIDE
    n_cmp = n_chunks - CMP_LEN // CMP_STRIDE + 1
    n_sel = -(-s // SEL_LEN)
    nsb_pad = -(-n_sel // 8) * 8
    mselt = jnp.asarray(_cmp_to_sel(n_chunks, n_cmp, nsb_pad, n_sel).T)
    expand = jnp.asarray(_block_expand(nsb_pad, n_sel, SEL_LEN, chunk)).astype(BF16)
    full = lambda n: pl.BlockSpec((1, n, s, HEAD_DIM), lambda i, j: (i, 0, 0, 0))
    qspec = pl.BlockSpec((1, NSA_HEADS, tq, HEAD_DIM), lambda i, j: (i, 0, j, 0))
    cspec = pl.BlockSpec((1, n_chunks, D_NSA_KV), lambda i, j: (i, 0, 0))
    rows = NSA_GROUP * tq
    return pl.pallas_call(
        functools.partial(_nsa_prompt_kernel, n_cmp=n_cmp, n_sel=n_sel),
        grid=(b, s // tq),
        in_specs=[qspec, qspec, pl.BlockSpec((1, tq, LANE), lambda i, j: (i, j, 0)), cspec, cspec,
                  full(NSA_KV_HEADS), full(NSA_KV_HEADS), full(NSA_KV_HEADS), full(NSA_KV_HEADS),
                  pl.BlockSpec(mselt.shape, lambda i, j: (0, 0)),
                  pl.BlockSpec(expand.shape, lambda i, j: (0, 0, 0))],
        out_specs=pl.BlockSpec((1, tq, D_NSA_Q), lambda i, j: (i, j, 0)),
        out_shape=jax.ShapeDtypeStruct((b, s, D_NSA_Q), BF16),
        scratch_shapes=[pltpu.VMEM((NSA_KV_HEADS, tq, nsb_pad), BF16),
                        pltpu.VMEM((NSA_KV_HEADS * rows, HEAD_DIM), F32),
                        pltpu.VMEM((NSA_KV_HEADS * rows, HEAD_DIM), F32)]
        + _attn_scratch(NSA_KV_HEADS * rows, s // chunk, chunk),
        compiler_params=pltpu.CompilerParams(dimension_semantics=("arbitrary", "arbitrary"),
                                             vmem_limit_bytes=VMEM_LIMIT),
        name="nsa_prompt",
    )(nqnh, nqh, gates, ckc, cvc, skh, svh, wkh, wvh, mselt, expand)


def _page_copy(cache_ref, buf_ref, sem_ref, pt_ref, b, slot, p):
    return pltpu.make_async_copy(cache_ref.at[pt_ref[b, p]], buf_ref.at[slot, p], sem_ref.at[slot])


def _pages_start(caches, bufs, sems, pt_ref, b, slot, n_pages):
    def body(p, carry):
        for c, bf, sm in zip(caches, bufs, sems):
            _page_copy(c, bf, sm, pt_ref, b, slot, p).start()
        return carry
    lax.fori_loop(0, n_pages, body, 0)


def _pages_wait(caches, bufs, sems, slot, n_pages):
    for c, bf, sm in zip(caches, bufs, sems):
        pltpu.make_async_copy(c.at[pl.ds(0, n_pages)], bf.at[slot], sm.at[slot]).wait()


def _paged_step(caches, bufs, sems, pt_ref, n_pages):
    b = pl.program_id(0)
    nb = pl.num_programs(0)
    slot = b % 2

    @pl.when(b == 0)
    def _():
        _pages_start(caches, bufs, sems, pt_ref, b, slot, n_pages)

    @pl.when(b + 1 < nb)
    def _():
        _pages_start(caches, bufs, sems, pt_ref, b + 1, 1 - slot, n_pages)

    _pages_wait(caches, bufs, sems, slot, n_pages)
    return slot


def _decode_scores(q_t, kbuf, slot, sc_ref, n_kv, group):
    n_pages, _, _, page = kbuf.shape[1:]
    for k in range(n_kv):
        qb = [jnp.broadcast_to(q_t[:, k * group + g:k * group + g + 1], (HEAD_DIM, page)) for g in range(group)]

        def body(p, carry):
            kt = kbuf[slot, p, k]
            for g in range(group):
                h = k * group + g
                sc_ref[p, h:h + 1, :] = jnp.sum(kt * qb[g], axis=0, keepdims=True)
            return carry

        lax.fori_loop(0, n_pages, body, 0, unroll=2)


def _decode_softmax(sc_ref, page_masks, s_new):
    s = jnp.where(jnp.stack(page_masks, axis=0) > 0.5, sc_ref[...], NEG_INF)
    m = jnp.maximum(jnp.max(jnp.max(s, axis=0), axis=1, keepdims=True), s_new)
    e = jnp.exp(s - m[None])
    e_new = jnp.exp(s_new - m)
    inv = 1.0 / jnp.maximum(jnp.sum(jnp.sum(e, axis=0), axis=1, keepdims=True) + e_new, 1e-30)
    sc_ref[...] = e * inv[None]
    return e_new * inv


def _decode_values(vbuf, slot, p_ref, entries, p_new, v_new_t, group):
    cols = []
    for h, ent in enumerate(entries):
        k = h // group
        acc = None
        for pg, idx in ent:
            term = vbuf[slot, pg, k] * p_ref[idx, h:h + 1, :]
            acc = term if acc is None else acc + term
        cols.append(jnp.sum(acc, axis=1, keepdims=True) + p_new[h:h + 1, 0:1] * v_new_t[:, k:k + 1])
    return cols


def _ranked_index(rank, ok, r):
    lane = lax.broadcasted_iota(jnp.int32, rank.shape, 1).astype(F32)
    hit = (rank == float(r)) & ok
    return jnp.sum(jnp.where(hit, lane, 0.0), axis=1, keepdims=True).astype(jnp.int32)


def _place_cols(cols):
    lane = lax.broadcasted_iota(jnp.int32, (HEAD_DIM, len(cols)), 1)
    out = jnp.zeros((HEAD_DIM, len(cols)), F32)
    for h, c in enumerate(cols):
        out = jnp.where(lane == h, c, out)
    return out


def _moba_sample_kernel(pt_ref, qt_ref, q_ref, knew_ref, vnewt_ref, k_hbm, v_hbm, o_ref,
                        kbuf, vbuf, sc_ref, sem_k, sem_v, *, n_blocks):
    n_pages, _, _, page = kbuf.shape[1:]
    slot = _paged_step((k_hbm, v_hbm), (kbuf, vbuf), (sem_k, sem_v), pt_ref, n_pages)
    ppb = MOBA_BLOCK // page
    _decode_scores(qt_ref[0] * SCALE, kbuf, slot, sc_ref, MOBA_KV_HEADS, MOBA_GROUP)
    s_new = jnp.sum(q_ref[0] * knew_ref[0], axis=1, keepdims=True) * SCALE
    lane = lax.broadcasted_iota(jnp.int32, (MOBA_HEADS, LANE), 1)
    sb = jnp.full((MOBA_HEADS, LANE), NEG_INF, F32)
    for j in range(n_blocks):
        tot = sc_ref[j * ppb]
        for r in range(1, ppb):
            tot = tot + sc_ref[j * ppb + r]
        sb = jnp.where(lane == j, jnp.sum(tot, axis=1, keepdims=True) * (1.0 / MOBA_BLOCK), sb)
    rank = _rank(sb, n_blocks, 1)
    in_range = lane < n_blocks
    sel = jnp.where((rank < MOBA_TOPK) & in_range, 1.0, 0.0)
    blk_masks = [jnp.broadcast_to(sel[:, j:j + 1], (MOBA_HEADS, page)) for j in range(n_blocks)]
    p_new = _decode_softmax(sc_ref, [blk_masks[p // ppb] for p in range(n_pages)], s_new)
    tops = [_ranked_index(rank, in_range, r) for r in range(MOBA_TOPK)]
    entries = []
    for h in range(MOBA_HEADS):
        pages = [tops[r][h, 0] * ppb + t for r in range(MOBA_TOPK) for t in range(ppb)]
        entries.append([(pg, pg) for pg in pages])
    cols = _decode_values(vbuf, slot, sc_ref, entries, p_new, vnewt_ref[0], MOBA_GROUP)
    o_ref[0] = _place_cols(cols)


def _native(cache):
    return jnp.transpose(cache, (0, 2, 3, 1))


def _moba_sample(page_table, q, k_new, v_new, cache_k, cache_v):
    db, n_pages = page_table.shape
    page = cache_k.shape[1]
    n_past = n_pages * page
    assert n_past % MOBA_BLOCK == 0 and MOBA_BLOCK % page == 0 and page == LANE
    n_blocks = n_past // MOBA_BLOCK
    assert MOBA_TOPK <= n_blocks <= LANE
    q3 = q.reshape(db, MOBA_HEADS, HEAD_DIM)
    q_t = jnp.transpose(q3, (0, 2, 1))
    k_rows = jnp.repeat(k_new.reshape(db, MOBA_KV_HEADS, HEAD_DIM), MOBA_GROUP, axis=1)
    v_t = jnp.transpose(v_new.reshape(db, MOBA_KV_HEADS, HEAD_DIM), (0, 2, 1))
    per_b = lambda a: pl.BlockSpec((1,) + a.shape[1:], lambda b, pt: (b, 0, 0))
    any_spec = pl.BlockSpec(memory_space=pl.ANY)
    buf = pltpu.VMEM((2, n_pages, MOBA_KV_HEADS, HEAD_DIM, page), F32)
    grid_spec = pltpu.PrefetchScalarGridSpec(
        num_scalar_prefetch=1,
        grid=(db,),
        in_specs=[per_b(q_t), per_b(q3), per_b(k_rows), per_b(v_t), any_spec, any_spec],
        out_specs=pl.BlockSpec((1, HEAD_DIM, MOBA_HEADS), lambda b, pt: (b, 0, 0)),
        scratch_shapes=[buf, buf, pltpu.VMEM((n_pages, MOBA_HEADS, page), F32),
                        pltpu.SemaphoreType.DMA((2,)), pltpu.SemaphoreType.DMA((2,))])
    o_t = pl.pallas_call(
        functools.partial(_moba_sample_kernel, n_blocks=n_blocks),
        grid_spec=grid_spec,
        out_shape=jax.ShapeDtypeStruct((db, HEAD_DIM, MOBA_HEADS), F32),
        compiler_params=pltpu.CompilerParams(dimension_semantics=("arbitrary",), vmem_limit_bytes=VMEM_LIMIT),
        name="moba_sample",
    )(page_table, q_t, q3, k_rows, v_t, _native(cache_k), _native(cache_v))
    return jnp.transpose(o_t, (0, 2, 1)).reshape(db, D_MOBA_Q)


def _row_sort_matrix(page):
    cpp = page // CMP_STRIDE
    out = np.arange(page)
    src = (out % cpp) * CMP_STRIDE + out // cpp
    return (src[:, None] == np.arange(page)[None, :]).astype(np.float32)


def _compress_pages(buf, slot, xs_ref, sort_ref, bias_ref, w1_ref, w2_ref):
    n_pages, _, _, page = buf.shape[1:]
    cpp = page // CMP_STRIDE
    n_chunks = n_pages * cpp
    sort = sort_ref[...]
    first = lax.broadcasted_iota(jnp.int32, (2 * cpp, D_NSA_KV), 1) < HEAD_DIM

    def to_rows(pp, carry):
        ra = _dot_nt(sort, buf[slot, 2 * pp].reshape(D_NSA_KV, page).astype(BF16))
        rb = _dot_nt(sort, buf[slot, 2 * pp + 1].reshape(D_NSA_KV, page).astype(BF16))
        start = pl.multiple_of(pp * 2 * cpp, 2 * cpp)
        for l2 in range(CMP_STRIDE // 2):
            lo, hi = 2 * l2 * cpp, (2 * l2 + 1) * cpp
            even = jnp.concatenate([ra[lo:lo + cpp], rb[lo:lo + cpp]], axis=0)
            odd = jnp.concatenate([ra[hi:hi + cpp], rb[hi:hi + cpp]], axis=0)
            xs_ref[0, l2, pl.ds(start, 2 * cpp), :] = jnp.where(first, even, pltpu.roll(odd, HEAD_DIM, 1)).astype(BF16)
            xs_ref[1, l2, pl.ds(start, 2 * cpp), :] = jnp.where(first, pltpu.roll(even, HEAD_DIM, 1), odd).astype(BF16)
        return carry

    lax.fori_loop(0, n_pages // 2, to_rows, 0, unroll=8)
    x = jnp.concatenate([jnp.concatenate([xs_ref[k, l2] for l2 in range(CMP_STRIDE // 2)], axis=1)
                         for k in range(NSA_KV_HEADS)], axis=0)
    hid = w1_ref.shape[1] // 2
    bias = bias_ref[:, 0:hid] + bias_ref[:, 2 * hid:3 * hid]
    p = _dot(x, w1_ref[...])
    hs = []
    for k in range(NSA_KV_HEADS):
        pk = p[k * n_chunks:(k + 1) * n_chunks]
        hs.append(_silu(pk[:, :hid] + pltpu.roll(pk[:, hid:], n_chunks - 1, 0) + bias).astype(BF16))
    return _dot(jnp.concatenate(hs, axis=1), w2_ref[...])


def _cmp_bias_kernel(kpe_ref, kw_ref, vpe_ref, vw_ref, kb_ref, vb_ref):
    for pe_ref, w_ref, b_ref in ((kpe_ref, kw_ref, kb_ref), (vpe_ref, vw_ref, vb_ref)):
        halves = [jnp.dot(pe_ref[r], w_ref[r], preferred_element_type=F32, precision=HIGHEST) for r in range(2)]
        b_ref[...] = jnp.concatenate(halves, axis=1)


def _cmp_bias(kpe, kw1, vpe, vw1):
    full = lambda a: pl.BlockSpec(a.shape, lambda i: (0,) * a.ndim)
    width = 2 * kw1.shape[2]
    out = pl.BlockSpec((1, width), lambda i: (0, 0))
    return pl.pallas_call(
        _cmp_bias_kernel,
        grid=(1,),
        in_specs=[full(kpe), full(kw1), full(vpe), full(vw1)],
        out_specs=[out, out],
        out_shape=[jax.ShapeDtypeStruct((1, width), F32)] * 2,
        compiler_params=pltpu.CompilerParams(dimension_semantics=("arbitrary",), vmem_limit_bytes=VMEM_LIMIT),
        name="cmp_bias",
    )(kpe, kw1, vpe, vw1)


def _nsa_sample_kernel(pt_ref, qn_ref, qrt_ref, qr_ref, gate_ref, sknew_ref, svnewt_ref, wknew_ref, wknewt_ref, wvnewt_ref,
                       wk_ref, wv_ref, sort_ref, kb_ref, kw1_ref, kw2_ref, vb_ref, vw1_ref, vw2_ref, msel_ref,
                       ck_hbm, cv_hbm, sk_hbm, sv_hbm, o_ref, wko_ref, wvo_ref,
                       ckbuf, cvbuf, skbuf, svbuf, xs_ref, sc_ref, sem_ck, sem_cv, sem_sk, sem_sv, *, n_cmp, n_sel):
    n_pages, _, _, page = skbuf.shape[1:]
    slot = _paged_step((ck_hbm, cv_hbm, sk_hbm, sv_hbm), (ckbuf, cvbuf, skbuf, svbuf),
                       (sem_ck, sem_cv, sem_sk, sem_sv), pt_ref, n_pages)
    n_chunks = n_pages * page // CMP_STRIDE
    qr_t = qrt_ref[0] * SCALE
    gates = gate_ref[0]
    ckc = _compress_pages(ckbuf, slot, xs_ref, sort_ref, kb_ref, kw1_ref, kw2_ref)
    cvc = _compress_pages(cvbuf, slot, xs_ref, sort_ref, vb_ref, vw1_ref, vw2_ref)
    n_idx = lax.broadcasted_iota(jnp.int32, (NSA_HEADS, n_chunks), 1)
    cmask = n_idx < n_cmp
    s = jnp.where(cmask, _dot_nt((qn_ref[0] * SCALE).astype(BF16), ckc.astype(BF16)), NEG_INF)
    m = jnp.max(s, axis=1, keepdims=True)
    e = jnp.where(cmask, jnp.exp(s - m), 0.0)
    p = e / jnp.maximum(jnp.sum(e, axis=1, keepdims=True), 1e-30)
    o_c = _dot(p.astype(BF16), cvc.astype(BF16))
    eye = jnp.where(lax.broadcasted_iota(jnp.int32, (D_NSA_KV, D_NSA_KV), 0)
                    == lax.broadcasted_iota(jnp.int32, (D_NSA_KV, D_NSA_KV), 1), 1.0, 0.0)
    o_c_t = _dot_nt(eye, o_c, precision=HIGHEST)
    p_kv = jnp.concatenate([jnp.sum(p[k * NSA_GROUP:(k + 1) * NSA_GROUP], axis=0, keepdims=True)
                            for k in range(NSA_KV_HEADS)], axis=0)
    nsb_pad = msel_ref.shape[1]
    score = jnp.dot(p_kv, msel_ref[...], preferred_element_type=F32, precision=HIGHEST)
    jb = lax.broadcasted_iota(jnp.int32, (NSA_KV_HEADS, nsb_pad), 1)
    cur = n_sel - 1
    valid = jb < n_sel
    forced = (jb == 0) | (jb == cur) | (jb == cur - 1)
    score = jnp.where(valid, jnp.where(forced, jnp.inf, score), NEG_INF)
    rank = _rank(score, n_sel, 1)
    bpp = page // SEL_LEN
    lane = lax.broadcasted_iota(jnp.int32, (1, page), 1)
    entries = [[] for _ in range(NSA_HEADS)]
    chosen = [_ranked_index(rank, valid, r) for r in range(SEL_TOPN)]
    for k in range(NSA_KV_HEADS):
        qb = [jnp.broadcast_to(qr_t[:, k * NSA_GROUP + g:k * NSA_GROUP + g + 1], (HEAD_DIM, page))
              for g in range(NSA_GROUP)]
        for r in range(SEL_TOPN):
            blk = chosen[r][k, 0]
            pg = jnp.minimum(blk // bpp, n_pages - 1)
            keep = lane // SEL_LEN == jnp.where(blk < cur, blk % bpp, -1)
            kt = skbuf[slot, pg, k]
            for g in range(NSA_GROUP):
                h = k * NSA_GROUP + g
                sc_ref[r, h:h + 1, :] = jnp.where(keep, jnp.sum(kt * qb[g], axis=0, keepdims=True), NEG_INF)
                entries[h].append((pg, r))
    qr = qr_ref[0] * SCALE
    s_new = jnp.sum(qr * sknew_ref[0], axis=1, keepdims=True)
    s = sc_ref[...]
    m = jnp.maximum(jnp.max(jnp.max(s, axis=0), axis=1, keepdims=True), s_new)
    e = jnp.exp(s - m[None])
    e_new = jnp.exp(s_new - m)
    inv = 1.0 / jnp.maximum(jnp.sum(jnp.sum(e, axis=0), axis=1, keepdims=True) + e_new, 1e-30)
    sc_ref[...] = e * inv[None]
    cols_s = _decode_values(svbuf, slot, sc_ref, entries, e_new * inv, svnewt_ref[0], NSA_GROUP)
    w_buf = wk_ref.shape[3]
    widx = lax.broadcasted_iota(jnp.int32, (NSA_HEADS, w_buf), 1)
    hrow = lax.broadcasted_iota(jnp.int32, (NSA_HEADS, w_buf), 0)
    sw = jnp.zeros((NSA_HEADS, w_buf), F32)
    for h in range(NSA_HEADS):
        kt = wk_ref[0, h // NSA_GROUP]
        sw = jnp.where(hrow == h, jnp.sum(kt * qr_t[:, h:h + 1], axis=0, keepdims=True), sw)
    sw = jnp.where(widx > w_buf - WINDOW, sw, NEG_INF)
    sw_new = jnp.sum(qr * wknew_ref[0], axis=1, keepdims=True)
    mw = jnp.maximum(jnp.max(sw, axis=1, keepdims=True), sw_new)
    ew = jnp.exp(sw - mw)
    ew_new = jnp.exp(sw_new - mw)
    invw = 1.0 / jnp.maximum(jnp.sum(ew, axis=1, keepdims=True) + ew_new, 1e-30)
    pw = ew * invw
    pw_new = ew_new * invw
    wvnew_t = wvnewt_ref[0]
    cols = []
    for h in range(NSA_HEADS):
        k = h // NSA_GROUP
        o_w = jnp.sum(wv_ref[0, k] * pw[h:h + 1, :], axis=1, keepdims=True) + pw_new[h:h + 1, 0:1] * wvnew_t[:, k:k + 1]
        o_cmp = o_c_t[k * HEAD_DIM:(k + 1) * HEAD_DIM, h:h + 1]
        cols.append(gates[0:1, h:h + 1] * o_cmp + gates[1:2, h:h + 1] * cols_s[h] + gates[2:3, h:h + 1] * o_w)
    o_ref[0] = _place_cols(cols)
    last = lax.broadcasted_iota(jnp.int32, (HEAD_DIM, w_buf), 1) == w_buf - 1
    wknew_t = wknewt_ref[0]
    for k in range(NSA_KV_HEADS):
        wko_ref[0, k] = jnp.where(last, wknew_t[:, k:k + 1], pltpu.roll(wk_ref[0, k], w_buf - 1, 1))
        wvo_ref[0, k] = jnp.where(last, wvnew_t[:, k:k + 1], pltpu.roll(wv_ref[0, k], w_buf - 1, 1))


def _expand_heads(q, n_heads, n_kv):
    b = q.shape[0]
    group = n_heads // n_kv
    place = jnp.asarray((np.arange(n_heads)[:, None] // group == np.arange(n_kv)[None, :]).astype(np.float32))
    q4 = q.reshape(b, n_heads, 1, HEAD_DIM) * place[None, :, :, None]
    return q4.reshape(b, n_heads, n_kv * HEAD_DIM)


def _nsa_sample(page_table, qn, qr, gates, sk_new, sv_new, wk_new, wv_new, state_wk, state_wv, kw, vw,
                cache_ck, cache_cv, cache_sk, cache_sv):
    db, n_pages = page_table.shape
    page = cache_ck.shape[1]
    n_past = n_pages * page
    assert page == LANE and page % SEL_LEN == 0 and n_pages % 2 == 0
    n_chunks = n_past // CMP_STRIDE
    n_cmp = n_chunks - CMP_LEN // CMP_STRIDE + 1
    n_sel = -(-(n_past + 1) // SEL_LEN)
    assert n_sel >= SEL_TOPN
    nsb_pad = -(-n_sel // LANE) * LANE
    w_buf = state_wk.shape[1]
    heads = lambda a, n: a.reshape(db, n, HEAD_DIM)
    t = lambda a: jnp.transpose(a, (0, 2, 1))
    rep = lambda a: jnp.repeat(heads(a, NSA_KV_HEADS), NSA_GROUP, axis=1)
    qr3 = heads(qr, NSA_HEADS)
    msel = jnp.asarray(_cmp_to_sel(n_chunks, n_cmp, nsb_pad, n_sel))
    small = [_expand_heads(qn, NSA_HEADS, NSA_KV_HEADS), t(qr3), qr3, t(gates.reshape(db, NSA_HEADS, 3)),
             rep(sk_new), t(heads(sv_new, NSA_KV_HEADS)), rep(wk_new), t(heads(wk_new, NSA_KV_HEADS)),
             t(heads(wv_new, NSA_KV_HEADS))]
    kb, vb = _cmp_bias(kw[0], kw[1], vw[0], vw[1])
    def cat(w1):
        one = w1[:, :, :CMP_HIDDEN].reshape(2, CMP_STRIDE, D_NSA_KV, CMP_HIDDEN)[:, :, :HEAD_DIM, :]
        return jnp.transpose(one, (1, 2, 0, 3)).reshape(CMP_STRIDE * HEAD_DIM, 2 * CMP_HIDDEN).astype(BF16)
    consts = [jnp.asarray(_row_sort_matrix(page)).astype(BF16), kb, cat(kw[1]), kw[2], vb, cat(vw[1]), vw[2], msel]
    wk_t, wv_t = _native(state_wk), _native(state_wv)
    per_b = lambda a: pl.BlockSpec((1,) + a.shape[1:], lambda b, pt: (b,) + (0,) * (a.ndim - 1))
    const = lambda a: pl.BlockSpec(a.shape, lambda b, pt: (0,) * a.ndim)
    any_spec = pl.BlockSpec(memory_space=pl.ANY)
    buf = pltpu.VMEM((2, n_pages, NSA_KV_HEADS, HEAD_DIM, page), F32)
    wspec = pl.BlockSpec((1, NSA_KV_HEADS, HEAD_DIM, w_buf), lambda b, pt: (b, 0, 0, 0))
    grid_spec = pltpu.PrefetchScalarGridSpec(
        num_scalar_prefetch=1,
        grid=(db,),
        in_specs=[per_b(a) for a in small] + [wspec, wspec] + [const(a) for a in consts] + [any_spec] * 4,
        out_specs=[pl.BlockSpec((1, HEAD_DIM, NSA_HEADS), lambda b, pt: (b, 0, 0)), wspec, wspec],
        scratch_shapes=[buf, buf, buf, buf,
                        pltpu.VMEM((NSA_KV_HEADS, CMP_STRIDE // 2, n_chunks, D_NSA_KV), BF16),
                        pltpu.VMEM((SEL_TOPN, NSA_HEADS, page), F32)] + [pltpu.SemaphoreType.DMA((2,))] * 4)
    o_t, wko, wvo = pl.pallas_call(
        functools.partial(_nsa_sample_kernel, n_cmp=n_cmp, n_sel=n_sel),
        grid_spec=grid_spec,
        out_shape=[jax.ShapeDtypeStruct((db, HEAD_DIM, NSA_HEADS), F32),
                   jax.ShapeDtypeStruct(wk_t.shape, F32), jax.ShapeDtypeStruct(wv_t.shape, F32)],
        compiler_params=pltpu.CompilerParams(dimension_semantics=("arbitrary",), vmem_limit_bytes=VMEM_LIMIT),
        name="nsa_sample",
    )(page_table, *small, wk_t, wv_t, *consts,
      _native(cache_ck), _native(cache_cv), _native(cache_sk), _native(cache_sv))
    back = lambda a: jnp.transpose(a, (0, 3, 1, 2))
    return jnp.transpose(o_t, (0, 2, 1)).reshape(db, D_NSA_Q), back(wko), back(wvo)


def _ffn_weights(w_in, w_out):
    return w_in.astype(BF16), w_out.astype(BF16)


def kernel(x_prompt, x_sample, cache_moba_k, cache_moba_v, cache_nsa_cmp_k, cache_nsa_cmp_v, cache_nsa_sel_k, cache_nsa_sel_v, state_nsa_win_k, state_nsa_win_v, page_table, c_prompt, c_sample, w_ada, b_ada, norm_ffn1, w_ffn1_in, w_ffn1_out, norm_mix, w_mix_in, w_mix_out, norm_ffn2, w_ffn2_in, w_ffn2_out, cmp_k_pe, cmp_k_w1, cmp_k_w2, cmp_v_pe, cmp_v_w1, cmp_v_w2, norm_final):
    depth = w_ada.shape[0]
    assert depth == 1, "single-layer step"
    b, s, d = x_prompt.shape
    db = x_sample.shape[0]
    assert x_sample.shape[1] == 1
    n_pages = page_table.shape[1]
    page = cache_moba_k.shape[2]
    n_past = n_pages * page
    assert s % Q_TILE == 0 and db % 8 == 0
    l = 0

    ffn1_w = _ffn_weights(w_ffn1_in[l], w_ffn1_out[l])
    ffn2_w = _ffn_weights(w_ffn2_in[l], w_ffn2_out[l])
    w_proj = jnp.pad(w_mix_in[l], ((0, 0), (0, PROJ_PAD - w_mix_in.shape[2]))).astype(BF16)
    w_mo = w_mix_out[l].astype(BF16)
    w_mo_m, w_mo_n = w_mo[:D_MOBA_Q], w_mo[D_MOBA_Q:]
    kw = _compress_weights(cmp_k_pe[l], cmp_k_w1[l], cmp_k_w2[l])
    vw = _compress_weights(cmp_v_pe[l], cmp_v_w1[l], cmp_v_w2[l])

    mods = _ada_mods(jnp.concatenate([c_sample, c_prompt], axis=0), w_ada[l], b_ada[l])
    mods_p = _Mods(mods, db, per_row=False)
    mods_s = _Mods(mods, db, per_row=True)
    xs = x_sample.reshape(1, db, d)
    tm_p = 512 if s % 512 == 0 else Q_TILE

    xp1 = _ffn(x_prompt, mods_p, (0, 1, 2), norm_ffn1[l], *ffn1_w, tm=tm_p)
    xs1 = _ffn(xs, mods_s, (0, 1, 2), norm_ffn1[l], *ffn1_w, tm=db)

    tabs_p = _rope_tables(jnp.arange(s, dtype=jnp.int32))
    pp = _proj(xp1, mods_p, (3, 4), norm_mix[l], w_proj, tabs_p, tm=tm_p, head_major=True)
    (mq, mk, mv, ck, cv, sk, sv, wk, wv, gates,
     mqh, mkh, mvh, nqnh, nqh, skh, svh, wkh, wvh) = pp
    om = _moba_prompt(mq, mqh, mk, mkh, mvh)
    bf = lambda w: (w[0], w[1].astype(BF16), w[2])
    ckc, cvc = _compress_prompt(ck, cv, bf(kw), bf(vw))
    on = _nsa_prompt(nqnh, nqh, gates, ckc, cvc, skh, svh, wkh, wvh)

    tabs_s = _rope_tables(jnp.full((db,), n_past, dtype=jnp.int32))
    ps = _proj(xs1, mods_s, (3, 4), norm_mix[l], w_proj, tabs_s, tm=db, head_major=False)
    (smq, smk, smv, sck, scv, ssk, ssv, swk, swv, sgates, snqn, snq) = [a[0] for a in ps]
    o_m_s = _moba_sample(page_table, smq, smk, smv, cache_moba_k[l], cache_moba_v[l])
    o_n_s, win_k_s, win_v_s = _nsa_sample(
        page_table, snqn, snq, sgates[:, :N_GATES], ssk, ssv, swk, swv,
        state_nsa_win_k[l], state_nsa_win_v[l], kw, vw,
        cache_nsa_cmp_k[l], cache_nsa_cmp_v[l], cache_nsa_sel_k[l], cache_nsa_sel_v[l])
    om_s = o_m_s.reshape(1, db, D_MOBA_Q).astype(BF16)
    on_s = o_n_s.reshape(1, db, D_NSA_Q).astype(BF16)

    yp = _ffn(xp1, mods_p, (6, 7, 8), norm_ffn2[l], *ffn2_w, tm=tm_p,
              mix=(om, on, w_mo_m, w_mo_n, 5), final_g=norm_final)
    ys = _ffn(xs1, mods_s, (6, 7, 8), norm_ffn2[l], *ffn2_w, tm=db,
              mix=(om_s, on_s, w_mo_m, w_mo_n, 5), final_g=norm_final)

    w_keep = min(WINDOW, s)
    st = lambda a, n: a.reshape(1, b, a.shape[1], n, HEAD_DIM)
    ss_ = lambda a, n: a.reshape(1, db, 1, n, HEAD_DIM)
    return (yp, ys.reshape(db, 1, d),
            st(mk, MOBA_KV_HEADS), ss_(smk, MOBA_KV_HEADS), st(mv, MOBA_KV_HEADS), ss_(smv, MOBA_KV_HEADS),
            st(ck, NSA_KV_HEADS), ss_(sck, NSA_KV_HEADS), st(cv, NSA_KV_HEADS), ss_(scv, NSA_KV_HEADS),
            st(sk, NSA_KV_HEADS), ss_(ssk, NSA_KV_HEADS), st(sv, NSA_KV_HEADS), ss_(ssv, NSA_KV_HEADS),
            st(wk[:, s - w_keep:], NSA_KV_HEADS), win_k_s[None],
            st(wv[:, s - w_keep:], NSA_KV_HEADS), win_v_s[None])
```

```python
import functools

import numpy as np
import jax
import jax.numpy as jnp
from jax import lax
from jax.experimental import pallas as pl
from jax.experimental.pallas import tpu as pltpu

F32 = jnp.float32
BF16 = jnp.bfloat16
NEG_INF = float("-inf")
HIGHEST = lax.Precision.HIGHEST

HEAD_DIM = 64
MOBA_HEADS = 8
MOBA_KV_HEADS = 4
MOBA_GROUP = MOBA_HEADS // MOBA_KV_HEADS
NSA_HEADS = 8
NSA_KV_HEADS = 2
NSA_GROUP = NSA_HEADS // NSA_KV_HEADS
ROPE_DIM = HEAD_DIM // 4
ROPE_THETA = 500000.0
MOBA_BLOCK = 256
MOBA_TOPK = 3
CMP_LEN = 32
CMP_STRIDE = 16
CMP_HIDDEN = 2 * HEAD_DIM
SEL_LEN = 64
SEL_TOPN = 16
WINDOW = 512
MACARON_WEIGHT = 0.5
N_MOD = 9
EPS = 1e-6
SCALE = HEAD_DIM ** -0.5
D_MOBA_Q = MOBA_HEADS * HEAD_DIM
D_MOBA_KV = MOBA_KV_HEADS * HEAD_DIM
D_NSA_Q = NSA_HEADS * HEAD_DIM
D_NSA_KV = NSA_KV_HEADS * HEAD_DIM
N_GATES = 3 * NSA_HEADS
LANE = 128
PROJ_PAD = 2432
VMEM_LIMIT = 56 * 1024 * 1024
Q_TILE = 256
FFN_CHUNK = 256
MOBA_KV_PER_PASS = 4

OFF_MQ, OFF_MK, OFF_MV, OFF_NQ = 0, 512, 768, 1024
OFF_CK, OFF_CV, OFF_SK, OFF_SV, OFF_WK, OFF_WV, OFF_G = 1536, 1664, 1792, 1920, 2048, 2176, 2304


def _silu(x):
    return x / (1.0 + jnp.exp(-x))


def _sigmoid(x):
    return 1.0 / (1.0 + jnp.exp(-x))


def _dot(a, b):
    return jnp.dot(a, b, preferred_element_type=F32)


def _dot_nt(a, b, precision=None):
    return lax.dot_general(a, b, (((1,), (1,)), ((), ())), preferred_element_type=F32, precision=precision)


def _modulated(x, g, shift, scale):
    ms = jnp.mean(x * x, axis=-1, keepdims=True)
    return (x * lax.rsqrt(ms + EPS) * g) * (1.0 + scale) + shift


def _rank(sc, ncols, axis):
    idx = lax.broadcasted_iota(jnp.int32, sc.shape, axis)
    rank = jnp.zeros(sc.shape, F32)
    for i in range(ncols):
        ci = sc[:, i:i + 1] if axis == 1 else sc[i:i + 1, :]
        beats = (ci > sc) | ((ci == sc) & (idx > i))
        rank = rank + jnp.where(beats, 1.0, 0.0)
    return rank


def _ada_kernel(c_ref, w_ref, b_ref, o_ref):
    sc = _silu(c_ref[...]).astype(BF16)
    o_ref[0] = _dot(sc, w_ref[...].astype(BF16)) + b_ref[0]


def _ada_mods(c_all, w_ada, b_ada):
    rows, d = c_all.shape
    return pl.pallas_call(
        _ada_kernel,
        grid=(N_MOD,),
        in_specs=[pl.BlockSpec((rows, d), lambda j: (0, 0)),
                  pl.BlockSpec((d, d), lambda j: (0, j)),
                  pl.BlockSpec((1, 1, d), lambda j: (j, 0, 0))],
        out_specs=pl.BlockSpec((1, rows, d), lambda j: (j, 0, 0)),
        out_shape=jax.ShapeDtypeStruct((N_MOD, rows, d), F32),
        compiler_params=pltpu.CompilerParams(dimension_semantics=("arbitrary",), vmem_limit_bytes=VMEM_LIMIT),
        name="ada_mods",
    )(c_all, w_ada, b_ada.reshape(N_MOD, 1, d))


class _Mods:
    def __init__(self, mods, n_sample, per_row):
        self.per_row = per_row
        self.n_sample = n_sample
        self.rows = mods.shape[1]
        d = mods.shape[2]
        self.d = d
        self.arr = mods if per_row else mods.reshape(N_MOD * self.rows, 1, d)

    def spec(self, j):
        if self.per_row:
            return pl.BlockSpec((1, self.n_sample, self.d), lambda g, i: (j, 0, 0))
        base = j * self.rows + self.n_sample
        return pl.BlockSpec((1, 1, self.d), lambda g, i: (base + g, 0, 0))


def _ffn_kernel(*refs, has_mix, final_norm, n_chunks):
    it = iter(refs)
    x_ref = next(it)
    if has_mix:
        mm_ref, mn_ref, wmm_ref, wmn_ref, gmix_ref = next(it), next(it), next(it), next(it), next(it)
    sh_ref, sc_ref, gt_ref, g_ref, win_ref, wo_ref = (next(it) for _ in range(6))
    gf_ref = next(it) if final_norm else None
    o_ref, h_ref = next(it), next(it)

    x = x_ref[0]
    if has_mix:
        x = x + gmix_ref[0] * (_dot(mm_ref[0], wmm_ref[...]) + _dot(mn_ref[0], wmn_ref[...]))
    xm = _modulated(x, g_ref[...], sh_ref[0], sc_ref[0]).astype(BF16)
    f = wo_ref.shape[0]
    tf = f // n_chunks
    for j in range(n_chunks):
        a = _dot(xm, win_ref[:, j * tf:(j + 1) * tf])
        b = _dot(xm, win_ref[:, f + j * tf:f + (j + 1) * tf])
        h_ref[:, j * tf:(j + 1) * tf] = (_silu(a) * b).astype(BF16)
    y = x + (MACARON_WEIGHT * gt_ref[0]) * _dot(h_ref[...], wo_ref[...])
    if final_norm:
        ms = jnp.mean(y * y, axis=-1, keepdims=True)
        y = y * lax.rsqrt(ms + EPS) * gf_ref[...]
    o_ref[0] = y


def _ffn(x, mods, jmods, norm_g, win, wo, tm, mix=None, final_g=None):
    g_, r_, d = x.shape
    f = wo.shape[0]
    assert f % FFN_CHUNK == 0
    n_chunks = f // FFN_CHUNK
    const2 = lambda g, i: (0, 0)
    row_spec = lambda w: pl.BlockSpec((1, tm, w), lambda g, i: (g, i, 0))
    args, specs = [x], [row_spec(d)]
    if mix is not None:
        mm, mn, wmm, wmn, jmix = mix
        args += [mm, mn, wmm, wmn, mods.arr]
        specs += [row_spec(mm.shape[2]), row_spec(mn.shape[2]),
                  pl.BlockSpec(wmm.shape, const2), pl.BlockSpec(wmn.shape, const2), mods.spec(jmix)]
    args += [mods.arr, mods.arr, mods.arr, norm_g.reshape(1, d), win, wo]
    specs += [mods.spec(jmods[0]), mods.spec(jmods[1]), mods.spec(jmods[2]),
              pl.BlockSpec((1, d), const2), pl.BlockSpec(win.shape, const2), pl.BlockSpec(wo.shape, const2)]
    if final_g is not None:
        args.append(final_g.reshape(1, d))
        specs.append(pl.BlockSpec((1, d), const2))
    kern = functools.partial(_ffn_kernel, has_mix=mix is not None, final_norm=final_g is not None, n_chunks=n_chunks)
    return pl.pallas_call(
        kern,
        grid=(g_, r_ // tm),
        in_specs=specs,
        out_specs=row_spec(d),
        out_shape=jax.ShapeDtypeStruct((g_, r_, d), F32),
        scratch_shapes=[pltpu.VMEM((tm, f), BF16)],
        compiler_params=pltpu.CompilerParams(dimension_semantics=("arbitrary", "arbitrary"),
                                             vmem_limit_bytes=VMEM_LIMIT),
        name="ffn",
    )(*args)


def _rope_tables(pos):
    half = ROPE_DIM // 2
    inv_freq = ROPE_THETA ** (-jnp.arange(half, dtype=F32) / half)
    ang = pos.astype(F32)[:, None] * inv_freq
    cos, sin = jnp.cos(ang), jnp.sin(ang)
    rows = pos.shape[0]
    rest = HEAD_DIM - ROPE_DIM
    one, zero = jnp.ones((rows, rest), F32), jnp.zeros((rows, rest), F32)
    zh = jnp.zeros((rows, half), F32)
    c = jnp.concatenate([cos, cos, one], axis=1)
    sa = jnp.concatenate([-sin, zh, zero], axis=1)
    sb = jnp.concatenate([zh, sin, zero], axis=1)
    tile = lambda t: jnp.concatenate([t, t], axis=1)
    return tile(c), tile(sa), tile(sb)


_STATE_HEADS = (("mk", MOBA_KV_HEADS), ("mv", MOBA_KV_HEADS)) + tuple(
    (n, NSA_KV_HEADS) for n in ("ck", "cv", "sk", "sv", "wk", "wv"))


def _proj_outputs(head_major):
    rows = [("mq", D_MOBA_Q), ("mk", D_MOBA_KV), ("ck", D_NSA_KV), ("cv", D_NSA_KV), ("gate", LANE)]
    if not head_major:
        rows += [("mv", D_MOBA_KV), ("sk", D_NSA_KV), ("sv", D_NSA_KV), ("wk", D_NSA_KV), ("wv", D_NSA_KV),
                 ("nqn", D_NSA_Q), ("nq", D_NSA_Q)]
    outs = [(n, "rows", w) for n, w in rows] + [(n, "state", h) for n, h in _STATE_HEADS]
    if head_major:
        outs += [("mq", "heads", MOBA_HEADS), ("mk", "heads", MOBA_KV_HEADS), ("mv", "heads", MOBA_KV_HEADS),
                 ("nqn", "heads", NSA_HEADS), ("nq", "heads", NSA_HEADS)]
        outs += [(n, "heads", NSA_KV_HEADS) for n in ("sk", "sv", "wk", "wv")]
    return outs


def _proj_kernel(*refs, head_major):
    (x_ref, sh_ref, sc_ref, g_ref, w_ref, cos_ref, sa_ref, sb_ref) = refs[:8]
    outs = refs[8:]
    xm = _modulated(x_ref[0], g_ref[...], sh_ref[0], sc_ref[0]).astype(BF16)
    cos, sa, sb = cos_ref[...], sa_ref[...], sb_ref[...]
    half = ROPE_DIM // 2

    def seg(lo, width):
        return _dot(xm, w_ref[:, lo:lo + width])

    def rope(y):
        parts = []
        for c in range(y.shape[1] // LANE):
            s = y[:, c * LANE:(c + 1) * LANE]
            parts.append(s * cos + pltpu.roll(s, LANE - half, 1) * sa + pltpu.roll(s, half, 1) * sb)
        return parts[0] if len(parts) == 1 else jnp.concatenate(parts, axis=1)

    mq = rope(seg(OFF_MQ, D_MOBA_Q))
    mk = rope(seg(OFF_MK, D_MOBA_KV))
    mv = seg(OFF_MV, D_MOBA_KV)
    nqn = seg(OFF_NQ, D_NSA_Q)
    nq = rope(nqn)
    ck = seg(OFF_CK, D_NSA_KV)
    cv = seg(OFF_CV, D_NSA_KV)
    sk = rope(seg(OFF_SK, D_NSA_KV))
    sv = seg(OFF_SV, D_NSA_KV)
    wk = rope(seg(OFF_WK, D_NSA_KV))
    wv = seg(OFF_WV, D_NSA_KV)
    vals = dict(mq=mq, mk=mk, mv=mv, nqn=nqn, nq=nq, ck=ck, cv=cv, sk=sk, sv=sv, wk=wk, wv=wv,
                gate=_sigmoid(seg(OFF_G, LANE)))
    pre_scaled = ("mq", "nqn", "nq")
    for ref, (name, kind, n) in zip(outs, _proj_outputs(head_major)):
        val = vals[name]
        if kind == "rows":
            ref[0] = val
        elif kind == "state":
            ref[0] = val.T.reshape(n, HEAD_DIM, val.shape[0])
        else:
            for h in range(n):
                piece = val[:, h * HEAD_DIM:(h + 1) * HEAD_DIM]
                ref[0, h] = (piece * SCALE if name in pre_scaled else piece).astype(BF16)


def _proj(x, mods, jmods, norm_g, w_pad, tables, tm, head_major):
    g_, r_, d = x.shape
    const2 = lambda g, i: (0, 0)
    row_spec = lambda w: pl.BlockSpec((1, tm, w), lambda g, i: (g, i, 0))
    tab_spec = pl.BlockSpec((tm, LANE), lambda g, i: (i, 0))
    in_specs = [row_spec(d), mods.spec(jmods[0]), mods.spec(jmods[1]), pl.BlockSpec((1, d), const2),
                pl.BlockSpec(w_pad.shape, const2), tab_spec, tab_spec, tab_spec]
    out_specs, out_shape = [], []
    layout = _proj_outputs(head_major)
    for _, kind, n in layout:
        if kind == "rows":
            out_specs.append(row_spec(n))
            out_shape.append(jax.ShapeDtypeStruct((g_, r_, n), F32))
        elif kind == "state":
            out_specs.append(pl.BlockSpec((1, n, HEAD_DIM, tm), lambda g, i: (g, 0, 0, i)))
            out_shape.append(jax.ShapeDtypeStruct((g_, n, HEAD_DIM, r_), F32))
        else:
            out_specs.append(pl.BlockSpec((1, n, tm, HEAD_DIM), lambda g, i: (g, 0, i, 0)))
            out_shape.append(jax.ShapeDtypeStruct((g_, n, r_, HEAD_DIM), BF16))
    res = pl.pallas_call(
        functools.partial(_proj_kernel, head_major=head_major),
        grid=(g_, r_ // tm),
        in_specs=in_specs,
        out_specs=out_specs,
        out_shape=out_shape,
        compiler_params=pltpu.CompilerParams(dimension_semantics=("arbitrary", "arbitrary"),
                                             vmem_limit_bytes=VMEM_LIMIT),
        name="proj",
    )(x, mods.arr, mods.arr, norm_g.reshape(1, d), w_pad, *tables)
    out = {"rows": {}, "state": {}, "heads": {}}
    for arr, (name, kind, _) in zip(res, layout):
        out[kind][name] = arr
    return out


def _compress_mlp(p0, p1, w2_ref):
    h = p0 + pltpu.roll(p1, p0.shape[0] - 1, 0)
    return _dot(_silu(h).astype(BF16), w2_ref[...])


def _compress_rows(x, pe_ref, w1_ref, w2_ref):
    p0 = _dot((x + pe_ref[0]).astype(BF16), w1_ref[0])
    p1 = _dot((x + pe_ref[1]).astype(BF16), w1_ref[1])
    return _compress_mlp(p0, p1, w2_ref)


def _compress_kernel(ck_ref, cv_ref, kpe_ref, kw1_ref, kw2_ref, vpe_ref, vw1_ref, vw2_ref, okc_ref, ovc_ref):
    okc_ref[0] = _compress_rows(ck_ref[0], kpe_ref, kw1_ref, kw2_ref)
    ovc_ref[0] = _compress_rows(cv_ref[0], vpe_ref, vw1_ref, vw2_ref)


def _compress_weights(pe, w1, w2):
    ratio = CMP_LEN // CMP_STRIDE
    eye = jnp.eye(NSA_KV_HEADS, dtype=F32)
    w1r = w1.reshape(ratio, CMP_STRIDE, HEAD_DIM, CMP_HIDDEN)
    w1f = jnp.einsum("rlde,kK->rlkdKe", w1r, eye).reshape(ratio, CMP_STRIDE * D_NSA_KV, NSA_KV_HEADS * CMP_HIDDEN)
    pef = jnp.broadcast_to(pe.reshape(ratio, CMP_STRIDE, 1, HEAD_DIM),
                           (ratio, CMP_STRIDE, NSA_KV_HEADS, HEAD_DIM)).reshape(ratio, 1, CMP_STRIDE * D_NSA_KV)
    w2f = jnp.einsum("ed,kK->keKd", w2, eye).reshape(NSA_KV_HEADS * CMP_HIDDEN, D_NSA_KV)
    return pef, w1f, w2f.astype(BF16)


def _compress_prompt(ck, cv, kw, vw):
    b, s, _ = ck.shape
    n = s // CMP_STRIDE
    width = CMP_STRIDE * D_NSA_KV
    ck2, cv2 = ck.reshape(b, n, width), cv.reshape(b, n, width)
    row = pl.BlockSpec((1, n, width), lambda i: (i, 0, 0))
    wspecs = []
    for w in kw + vw:
        wspecs.append(pl.BlockSpec(w.shape, (lambda i: (0, 0, 0)) if w.ndim == 3 else (lambda i: (0, 0))))
    out = pl.BlockSpec((1, n, D_NSA_KV), lambda i: (i, 0, 0))
    return pl.pallas_call(
        _compress_kernel,
        grid=(b,),
        in_specs=[row, row] + wspecs,
        out_specs=[out, out],
        out_shape=[jax.ShapeDtypeStruct((b, n, D_NSA_KV), F32)] * 2,
        compiler_params=pltpu.CompilerParams(dimension_semantics=("arbitrary",), vmem_limit_bytes=VMEM_LIMIT),
        name="compress_prompt",
    )(ck2, cv2, *kw, *vw)


def _two_pass_init(mx_ref, l_ref, acc_ref):
    mx_ref[...] = jnp.full(mx_ref.shape, NEG_INF, F32)
    l_ref[...] = jnp.zeros(l_ref.shape, F32)
    acc_ref[...] = jnp.zeros(acc_ref.shape, F32)


def _pass1(j, s, s_ref, mx_ref, first=False):
    s_ref[j] = s
    m = s[:, 0:LANE]
    for c in range(1, s.shape[1] // LANE):
        m = jnp.maximum(m, s[:, c * LANE:(c + 1) * LANE])
    mx_ref[...] = m if first else jnp.maximum(mx_ref[...], m)


def _row_max(mx_ref, mb_ref):
    m = jnp.max(mx_ref[...], axis=1, keepdims=True)
    m = jnp.where(m == NEG_INF, 0.0, m)
    mb_ref[...] = jnp.broadcast_to(m, mb_ref.shape)


def _pass2(j, vs, s_ref, mb_ref, l_ref, acc_ref, first=False):
    mb = mb_ref[...]
    s = s_ref[j]
    parts = [jnp.exp(s[:, c * LANE:(c + 1) * LANE] - mb) for c in range(s.shape[1] // LANE)]
    tot = parts[0]
    for p in parts[1:]:
        tot = tot + p
    p = jnp.concatenate(parts, axis=1).astype(BF16)
    n = p.shape[0] // len(vs)
    upd = [_dot(p[i * n:(i + 1) * n], v) for i, v in enumerate(vs)]
    upd = upd[0] if len(upd) == 1 else jnp.concatenate(upd, axis=0)
    if first:
        l_ref[...] = tot
        acc_ref[...] = upd
    else:
        l_ref[...] += tot
        acc_ref[...] += upd


def _two_pass_out(l_ref, acc_ref, scale=None):
    l = jnp.maximum(jnp.sum(l_ref[...], axis=1, keepdims=True), 1e-30)
    return acc_ref[...] * ((1.0 / l) if scale is None else (scale / l))


def _add_bias(s, bias, groups):
    tq = bias.shape[0]
    return (s.reshape(groups, tq, s.shape[1]) + bias[None]).reshape(s.shape)


def _attn_scratch(rows, n_chunks, chunk):
    return [pltpu.VMEM((rows, HEAD_DIM), BF16),
            pltpu.VMEM((n_chunks, rows, chunk), F32),
            pltpu.VMEM((rows, LANE), F32),
            pltpu.VMEM((rows, LANE), F32),
            pltpu.VMEM((rows, LANE), F32),
            pltpu.VMEM((rows, HEAD_DIM), F32)]


def _moba_prompt_kernel(mq_ref, mqh_ref, mk_ref, mkh_ref, mvh_ref, o_ref,
                        kmean_ref, bias_ref, q_ref, s_ref, mx_ref, mb_ref, l_ref, acc_ref, *, n_blocks):
    qi = pl.program_id(1)
    tq = mq_ref.shape[1]
    per_kv = MOBA_GROUP * tq
    rows = MOBA_KV_PER_PASS * per_kv

    @pl.when(qi == 0)
    def _():
        for j in range(n_blocks):
            blk = mk_ref[0, j * MOBA_BLOCK:(j + 1) * MOBA_BLOCK, :]
            kmean_ref[j:j + 1, :] = jnp.sum(blk, axis=0, keepdims=True) * (1.0 / MOBA_BLOCK)

    blk_t = lax.broadcasted_iota(jnp.int32, (n_blocks, tq), 0)
    r_loc = lax.broadcasted_iota(jnp.int32, (rows, MOBA_BLOCK), 0) % tq
    c_loc = lax.broadcasted_iota(jnp.int32, (rows, MOBA_BLOCK), 1)
    pad = jnp.full((LANE - n_blocks, tq), NEG_INF, F32)

    pieces = []
    for k0 in range(0, MOBA_KV_HEADS, MOBA_KV_PER_PASS):
        kvs = list(range(k0, k0 + MOBA_KV_PER_PASS))
        for k in kvs:
            km = kmean_ref[:, k * HEAD_DIM:(k + 1) * HEAD_DIM]
            for g in range(MOBA_GROUP):
                h = k * MOBA_GROUP + g
                lo = (h - k0 * MOBA_GROUP) * tq
                qf = mq_ref[0, :, h * HEAD_DIM:(h + 1) * HEAD_DIM]
                sb = jnp.where(blk_t < qi, _dot_nt(km, qf, precision=HIGHEST), NEG_INF)
                sel = (_rank(sb, n_blocks, 0) < MOBA_TOPK) & (blk_t < qi)
                bias_t = jnp.concatenate([jnp.where(sel, 0.0, NEG_INF), pad], axis=0)
                bias_ref[lo:lo + tq, :] = bias_t.T
                q_ref[lo:lo + tq, :] = mqh_ref[0, h]
        def scores(j):
            ks = slice(j * MOBA_BLOCK, (j + 1) * MOBA_BLOCK)
            return jnp.concatenate([_dot_nt(q_ref[i * per_kv:(i + 1) * per_kv, :], mkh_ref[0, k, ks, :])
                                    for i, k in enumerate(kvs)], axis=0)

        for j in range(n_blocks):
            @pl.when(j < qi)
            def _():
                _pass1(j, scores(j) + bias_ref[:, j:j + 1], s_ref, mx_ref, first=j == 0)

            @pl.when(j == qi)
            def _():
                _pass1(j, jnp.where(c_loc <= r_loc, scores(j), NEG_INF), s_ref, mx_ref, first=j == 0)

        _row_max(mx_ref, mb_ref)
        for j in range(n_blocks):
            @pl.when(j <= qi)
            def _():
                vs = [mvh_ref[0, k, j * MOBA_BLOCK:(j + 1) * MOBA_BLOCK, :] for k in kvs]
                _pass2(j, vs, s_ref, mb_ref, l_ref, acc_ref, first=j == 0)

        o = _two_pass_out(l_ref, acc_ref)
        pieces += [o[i * tq:(i + 1) * tq, :] for i in range(MOBA_KV_PER_PASS * MOBA_GROUP)]
    o_ref[0] = jnp.concatenate(pieces, axis=1).astype(BF16)


def _moba_prompt(mq, mqh, mk, mkh, mvh):
    b, s, _ = mq.shape
    tq = Q_TILE
    assert tq == MOBA_BLOCK
    n_blocks = s // MOBA_BLOCK
    rows = MOBA_KV_PER_PASS * MOBA_GROUP * tq
    full = lambda n: pl.BlockSpec((1, n, s, HEAD_DIM), lambda i, j: (i, 0, 0, 0))
    return pl.pallas_call(
        functools.partial(_moba_prompt_kernel, n_blocks=n_blocks),
        grid=(b, s // tq),
        in_specs=[pl.BlockSpec((1, tq, D_MOBA_Q), lambda i, j: (i, j, 0)),
                  pl.BlockSpec((1, MOBA_HEADS, tq, HEAD_DIM), lambda i, j: (i, 0, j, 0)),
                  pl.BlockSpec((1, s, D_MOBA_KV), lambda i, j: (i, 0, 0)),
                  full(MOBA_KV_HEADS), full(MOBA_KV_HEADS)],
        out_specs=pl.BlockSpec((1, tq, D_MOBA_Q), lambda i, j: (i, j, 0)),
        out_shape=jax.ShapeDtypeStruct((b, s, D_MOBA_Q), BF16),
        scratch_shapes=[pltpu.VMEM((n_blocks, D_MOBA_KV), F32),
                        pltpu.VMEM((rows, LANE), F32)] + _attn_scratch(rows, n_blocks, MOBA_BLOCK),
        compiler_params=pltpu.CompilerParams(dimension_semantics=("arbitrary", "arbitrary"),
                                             vmem_limit_bytes=VMEM_LIMIT),
        name="moba_prompt",
    )(mq, mqh, mk, mkh, mvh)


def _cmp_to_sel(nc_pad, nc, nsb_pad, nsb):
    i = np.arange(nc_pad)[:, None]
    j = np.arange(nsb_pad)[None, :]
    start = i * CMP_STRIDE
    m = (start < (j + 1) * SEL_LEN) & (start + CMP_LEN > j * SEL_LEN) & (i < nc) & (j < nsb)
    return m.astype(np.float32)


def _block_expand(n_blocks_pad, n_blocks, block_len, chunk):
    n_chunks = n_blocks * block_len // chunk
    pos = np.arange(n_chunks * chunk).reshape(n_chunks, 1, chunk)
    j = np.arange(n_blocks_pad).reshape(1, n_blocks_pad, 1)
    return (pos // block_len == j).astype(np.float32)


def _nsa_prompt_kernel(nqnh_ref, nqh_ref, gate_ref, ckc_ref, cvc_ref, skh_ref, svh_ref, wkh_ref, wvh_ref, mselt_ref, exp_ref,
                       o_ref, sel_ref, oc_ref, os_ref, q_ref, s_ref, mx_ref, mb_ref, l_ref, acc_ref, *, n_cmp, n_sel):
    qi = pl.program_id(1)
    tq = nqnh_ref.shape[2]
    rows = NSA_GROUP * tq
    nc_pad = ckc_ref.shape[1]
    nsb_pad = mselt_ref.shape[0]
    n_chunks, _, chunk = exp_ref.shape
    q0 = qi * tq
    gates = gate_ref[0]

    qpos_c = q0 + lax.broadcasted_iota(jnp.int32, (rows, nc_pad), 0) % tq
    n_idx = lax.broadcasted_iota(jnp.int32, (rows, nc_pad), 1)
    cmp_mask = (n_idx * CMP_STRIDE + (CMP_LEN - 1) <= qpos_c) & (n_idx < n_cmp)

    t = q0 + lax.broadcasted_iota(jnp.int32, (nsb_pad, tq), 1)
    jb = lax.broadcasted_iota(jnp.int32, (nsb_pad, tq), 0)
    cur = t // SEL_LEN
    valid = (jb * SEL_LEN <= t) & (jb < n_sel)
    forced = (jb == 0) | (jb == cur) | (jb == cur - 1)
    pad = jnp.zeros((LANE - nsb_pad, tq), F32)

    r_loc = lax.broadcasted_iota(jnp.int32, (tq, chunk), 0)
    c_loc = lax.broadcasted_iota(jnp.int32, (tq, chunk), 1)
    lower = jnp.where(c_loc <= r_loc, 0.0, NEG_INF)
    upper = jnp.where(c_loc > r_loc, 0.0, NEG_INF)

    kvs = range(NSA_KV_HEADS)
    for k in kvs:
        lo = k * HEAD_DIM
        for g in range(NSA_GROUP):
            q_ref[g * tq:(g + 1) * tq, :] = nqnh_ref[0, k * NSA_GROUP + g]
        kc = ckc_ref[0, :, lo:lo + HEAD_DIM].astype(BF16)
        vc = cvc_ref[0, :, lo:lo + HEAD_DIM].astype(BF16)
        s = jnp.where(cmp_mask, _dot_nt(q_ref[0:rows, :], kc), NEG_INF)
        m = jnp.max(s, axis=1, keepdims=True)
        m = jnp.where(m == NEG_INF, 0.0, m)
        e = jnp.where(cmp_mask, jnp.exp(s - m), 0.0)
        p = e / jnp.maximum(jnp.sum(e, axis=1, keepdims=True), 1e-30)
        oc_ref[k * rows:(k + 1) * rows, :] = _dot(p.astype(BF16), vc)
        p_kv = p[0:tq]
        for g in range(1, NSA_GROUP):
            p_kv = p_kv + p[g * tq:(g + 1) * tq]
        score = _dot_nt(mselt_ref[...], p_kv, precision=HIGHEST)
        score = jnp.where(valid, jnp.where(forced, jnp.inf, score), NEG_INF)
        sel = (_rank(score, n_sel, 0) < SEL_TOPN) & valid
        sel_t = jnp.concatenate([jnp.where(sel, 1.0, 0.0), pad], axis=0)
        sel_ref[k] = sel_t.T[:, :nsb_pad].astype(BF16)
    for h in range(NSA_HEADS):
        q_ref[h * tq:(h + 1) * tq, :] = nqh_ref[0, h]

    def scores(k_ref, j, biases):
        ks = slice(j * chunk, (j + 1) * chunk)
        out = []
        for k in kvs:
            s = _dot_nt(q_ref[k * rows:(k + 1) * rows, :], k_ref[0, k, ks, :])
            out.append(s if biases is None else _add_bias(s, biases[k], NSA_GROUP))
        return jnp.concatenate(out, axis=0)

    def sel_bias(j, inside):
        return [jnp.where(_dot(sel_ref[k], exp_ref[j]) > 0.5, inside, NEG_INF) for k in kvs]

    def gate_rows(branch):
        return jnp.concatenate([gates[:, 3 * h + branch:3 * h + branch + 1] for h in range(NSA_HEADS)], axis=0)

    for j in range(n_chunks):
        @pl.when(j < qi)
        def _():
            _pass1(j, scores(skh_ref, j, sel_bias(j, 0.0)), s_ref, mx_ref, first=j == 0)

        @pl.when(j == qi)
        def _():
            _pass1(j, scores(skh_ref, j, sel_bias(j, lower)), s_ref, mx_ref, first=j == 0)

    _row_max(mx_ref, mb_ref)
    for j in range(n_chunks):
        @pl.when(j <= qi)
        def _():
            _pass2(j, [svh_ref[0, k, j * chunk:(j + 1) * chunk, :] for k in kvs], s_ref, mb_ref, l_ref, acc_ref,
                   first=j == 0)

    os_ref[...] = _two_pass_out(l_ref, acc_ref, gate_rows(1))
    _two_pass_init(mx_ref, l_ref, acc_ref)
    for j in range(n_chunks):
        @pl.when(j == qi)
        def _():
            _pass1(j, scores(wkh_ref, j, [lower] * NSA_KV_HEADS), s_ref, mx_ref)

        @pl.when(j == qi - 1)
        def _():
            _pass1(j, scores(wkh_ref, j, None), s_ref, mx_ref)

        @pl.when(j == qi - 2)
        def _():
            _pass1(j, scores(wkh_ref, j, [upper] * NSA_KV_HEADS), s_ref, mx_ref)

    _row_max(mx_ref, mb_ref)
    for j in range(n_chunks):
        @pl.when((j <= qi) & (j >= qi - 2))
        def _():
            _pass2(j, [wvh_ref[0, k, j * chunk:(j + 1) * chunk, :] for k in kvs], s_ref, mb_ref, l_ref, acc_ref)

    out = gate_rows(0) * oc_ref[...] + os_ref[...] + _two_pass_out(l_ref, acc_ref, gate_rows(2))
    o_ref[0] = jnp.concatenate([out[h * tq:(h + 1) * tq] for h in range(NSA_HEADS)], axis=1).astype(BF16)


def _nsa_prompt(nqnh, nqh, gates, ckc, cvc, skh, svh, wkh, wvh):
    b, _, s, _ = nqnh.shape
    tq = Q_TILE
    chunk = Q_TILE
    assert WINDOW == 2 * chunk
    n_chunks = s // CMP_STRIDE
    n_cmp = n_chunks - CMP_LEN // CMP_STRIDE + 1
    n_sel = -(-s // SEL_LEN)
    nsb_pad = -(-n_sel // 8) * 8
    mselt = jnp.asarray(_cmp_to_sel(n_chunks, n_cmp, nsb_pad, n_sel).T)
    expand = jnp.asarray(_block_expand(nsb_pad, n_sel, SEL_LEN, chunk)).astype(BF16)
    full = lambda n: pl.BlockSpec((1, n, s, HEAD_DIM), lambda i, j: (i, 0, 0, 0))
    qspec = pl.BlockSpec((1, NSA_HEADS, tq, HEAD_DIM), lambda i, j: (i, 0, j, 0))
    cspec = pl.BlockSpec((1, n_chunks, D_NSA_KV), lambda i, j: (i, 0, 0))
    rows = NSA_GROUP * tq
    return pl.pallas_call(
        functools.partial(_nsa_prompt_kernel, n_cmp=n_cmp, n_sel=n_sel),
        grid=(b, s // tq),
        in_specs=[qspec, qspec, pl.BlockSpec((1, tq, LANE), lambda i, j: (i, j, 0)), cspec, cspec,
                  full(NSA_KV_HEADS), full(NSA_KV_HEADS), full(NSA_KV_HEADS), full(NSA_KV_HEADS),
                  pl.BlockSpec(mselt.shape, lambda i, j: (0, 0)),
                  pl.BlockSpec(expand.shape, lambda i, j: (0, 0, 0))],
        out_specs=pl.BlockSpec((1, tq, D_NSA_Q), lambda i, j: (i, j, 0)),
        out_shape=jax.ShapeDtypeStruct((b, s, D_NSA_Q), BF16),
        scratch_shapes=[pltpu.VMEM((NSA_KV_HEADS, tq, nsb_pad), BF16),
                        pltpu.VMEM((NSA_KV_HEADS * rows, HEAD_DIM), F32),
                        pltpu.VMEM((NSA_KV_HEADS * rows, HEAD_DIM), F32)]
        + _attn_scratch(NSA_KV_HEADS * rows, s // chunk, chunk),
        compiler_params=pltpu.CompilerParams(dimension_semantics=("arbitrary", "arbitrary"),
                                             vmem_limit_bytes=VMEM_LIMIT),
        name="nsa_prompt",
    )(nqnh, nqh, gates, ckc, cvc, skh, svh, wkh, wvh, mselt, expand)


def _page_copy(cache_ref, buf_ref, sem_ref, pt_ref, b, slot, p):
    return pltpu.make_async_copy(cache_ref.at[pt_ref[b, p]], buf_ref.at[slot, p], sem_ref.at[slot])


def _pages_start(caches, bufs, sems, pt_ref, b, slot, n_pages):
    def body(p, carry):
        for c, bf, sm in zip(caches, bufs, sems):
            _page_copy(c, bf, sm, pt_ref, b, slot, p).start()
        return carry
    lax.fori_loop(0, n_pages, body, 0)


def _pages_wait(caches, bufs, sems, slot, n_pages):
    for c, bf, sm in zip(caches, bufs, sems):
        pltpu.make_async_copy(c.at[pl.ds(0, n_pages)], bf.at[slot], sm.at[slot]).wait()


def _paged_step(caches, bufs, sems, pt_ref, n_pages):
    b = pl.program_id(0)
    nb = pl.num_programs(0)
    slot = b % 2

    @pl.when(b == 0)
    def _():
        _pages_start(caches, bufs, sems, pt_ref, b, slot, n_pages)

    @pl.when(b + 1 < nb)
    def _():
        _pages_start(caches, bufs, sems, pt_ref, b + 1, 1 - slot, n_pages)

    _pages_wait(caches, bufs, sems, slot, n_pages)
    return slot


def _decode_scores(q_t, kbuf, slot, sc_ref, n_kv, group):
    n_pages, _, _, page = kbuf.shape[1:]
    for k in range(n_kv):
        qb = [jnp.broadcast_to(q_t[:, k * group + g:k * group + g + 1], (HEAD_DIM, page)) for g in range(group)]

        def body(p, carry):
            kt = kbuf[slot, p, k]
            for g in range(group):
                h = k * group + g
                sc_ref[p, h:h + 1, :] = jnp.sum(kt * qb[g], axis=0, keepdims=True)
            return carry

        lax.fori_loop(0, n_pages, body, 0, unroll=2)


def _decode_softmax(sc_ref, page_masks, s_new):
    s = jnp.where(jnp.stack(page_masks, axis=0) > 0.5, sc_ref[...], NEG_INF)
    m = jnp.maximum(jnp.max(jnp.max(s, axis=0), axis=1, keepdims=True), s_new)
    e = jnp.exp(s - m[None])
    e_new = jnp.exp(s_new - m)
    inv = 1.0 / jnp.maximum(jnp.sum(jnp.sum(e, axis=0), axis=1, keepdims=True) + e_new, 1e-30)
    sc_ref[...] = e * inv[None]
    return e_new * inv


def _decode_values(vbuf, slot, p_ref, entries, p_new, v_new_t, group):
    cols = []
    for h, ent in enumerate(entries):
        k = h // group
        acc = None
        for pg, idx in ent:
            term = vbuf[slot, pg, k] * p_ref[idx, h:h + 1, :]
            acc = term if acc is None else acc + term
        cols.append(jnp.sum(acc, axis=1, keepdims=True) + p_new[h:h + 1, 0:1] * v_new_t[:, k:k + 1])
    return cols


def _ranked_index(rank, ok, r):
    lane = lax.broadcasted_iota(jnp.int32, rank.shape, 1).astype(F32)
    hit = (rank == float(r)) & ok
    return jnp.sum(jnp.where(hit, lane, 0.0), axis=1, keepdims=True).astype(jnp.int32)


def _place_cols(cols):
    lane = lax.broadcasted_iota(jnp.int32, (HEAD_DIM, len(cols)), 1)
    out = jnp.zeros((HEAD_DIM, len(cols)), F32)
    for h, c in enumerate(cols):
        out = jnp.where(lane == h, c, out)
    return out


def _moba_sample_kernel(pt_ref, qt_ref, q_ref, knew_ref, vnewt_ref, k_hbm, v_hbm, o_ref,
                        kbuf, vbuf, sc_ref, sem_k, sem_v, *, n_blocks):
    n_pages, _, _, page = kbuf.shape[1:]
    slot = _paged_step((k_hbm, v_hbm), (kbuf, vbuf), (sem_k, sem_v), pt_ref, n_pages)
    ppb = MOBA_BLOCK // page
    _decode_scores(qt_ref[0] * SCALE, kbuf, slot, sc_ref, MOBA_KV_HEADS, MOBA_GROUP)
    s_new = jnp.sum(q_ref[0] * knew_ref[0], axis=1, keepdims=True) * SCALE
    lane = lax.broadcasted_iota(jnp.int32, (MOBA_HEADS, LANE), 1)
    sb = jnp.full((MOBA_HEADS, LANE), NEG_INF, F32)
    for j in range(n_blocks):
        tot = sc_ref[j * ppb]
        for r in range(1, ppb):
            tot = tot + sc_ref[j * ppb + r]
        sb = jnp.where(lane == j, jnp.sum(tot, axis=1, keepdims=True) * (1.0 / MOBA_BLOCK), sb)
    rank = _rank(sb, n_blocks, 1)
    in_range = lane < n_blocks
    sel = jnp.where((rank < MOBA_TOPK) & in_range, 1.0, 0.0)
    blk_masks = [jnp.broadcast_to(sel[:, j:j + 1], (MOBA_HEADS, page)) for j in range(n_blocks)]
    p_new = _decode_softmax(sc_ref, [blk_masks[p // ppb] for p in range(n_pages)], s_new)
    tops = [_ranked_index(rank, in_range, r) for r in range(MOBA_TOPK)]
    entries = []
    for h in range(MOBA_HEADS):
        pages = [tops[r][h, 0] * ppb + t for r in range(MOBA_TOPK) for t in range(ppb)]
        entries.append([(pg, pg) for pg in pages])
    cols = _decode_values(vbuf, slot, sc_ref, entries, p_new, vnewt_ref[0], MOBA_GROUP)
    o_ref[0] = _place_cols(cols)


def _native(cache):
    return jnp.transpose(cache, (0, 2, 3, 1))


def _moba_sample(page_table, q, k_new, v_new, cache_k, cache_v):
    db, n_pages = page_table.shape
    page = cache_k.shape[1]
    n_past = n_pages * page
    assert n_past % MOBA_BLOCK == 0 and MOBA_BLOCK % page == 0 and page == LANE
    n_blocks = n_past // MOBA_BLOCK
    assert MOBA_TOPK <= n_blocks <= LANE
    q3 = q.reshape(db, MOBA_HEADS, HEAD_DIM)
    q_t = jnp.transpose(q3, (0, 2, 1))
    k_rows = jnp.repeat(k_new.reshape(db, MOBA_KV_HEADS, HEAD_DIM), MOBA_GROUP, axis=1)
    v_t = jnp.transpose(v_new.reshape(db, MOBA_KV_HEADS, HEAD_DIM), (0, 2, 1))
    per_b = lambda a: pl.BlockSpec((1,) + a.shape[1:], lambda b, pt: (b, 0, 0))
    any_spec = pl.BlockSpec(memory_space=pl.ANY)
    buf = pltpu.VMEM((2, n_pages, MOBA_KV_HEADS, HEAD_DIM, page), F32)
    grid_spec = pltpu.PrefetchScalarGridSpec(
        num_scalar_prefetch=1,
        grid=(db,),
        in_specs=[per_b(q_t), per_b(q3), per_b(k_rows), per_b(v_t), any_spec, any_spec],
        out_specs=pl.BlockSpec((1, HEAD_DIM, MOBA_HEADS), lambda b, pt: (b, 0, 0)),
        scratch_shapes=[buf, buf, pltpu.VMEM((n_pages, MOBA_HEADS, page), F32),
                        pltpu.SemaphoreType.DMA((2,)), pltpu.SemaphoreType.DMA((2,))])
    o_t = pl.pallas_call(
        functools.partial(_moba_sample_kernel, n_blocks=n_blocks),
        grid_spec=grid_spec,
        out_shape=jax.ShapeDtypeStruct((db, HEAD_DIM, MOBA_HEADS), F32),
        compiler_params=pltpu.CompilerParams(dimension_semantics=("arbitrary",), vmem_limit_bytes=VMEM_LIMIT),
        name="moba_sample",
    )(page_table, q_t, q3, k_rows, v_t, _native(cache_k), _native(cache_v))
    return jnp.transpose(o_t, (0, 2, 1)).reshape(db, D_MOBA_Q)


def _row_sort_matrix(page):
    cpp = page // CMP_STRIDE
    out = np.arange(page)
    src = (out % cpp) * CMP_STRIDE + out // cpp
    return (src[:, None] == np.arange(page)[None, :]).astype(np.float32)


def _compress_pages(buf, slot, xs_ref, sort_ref, bias_ref, w1_ref, w2_ref):
    n_pages, _, _, page = buf.shape[1:]
    cpp = page // CMP_STRIDE
    n_chunks = n_pages * cpp
    sort = sort_ref[...]
    first = lax.broadcasted_iota(jnp.int32, (2 * cpp, D_NSA_KV), 1) < HEAD_DIM

    def to_rows(pp, carry):
        ra = _dot_nt(sort, buf[slot, 2 * pp].reshape(D_NSA_KV, page).astype(BF16))
        rb = _dot_nt(sort, buf[slot, 2 * pp + 1].reshape(D_NSA_KV, page).astype(BF16))
        start = pl.multiple_of(pp * 2 * cpp, 2 * cpp)
        for l2 in range(CMP_STRIDE // 2):
            lo, hi = 2 * l2 * cpp, (2 * l2 + 1) * cpp
            even = jnp.concatenate([ra[lo:lo + cpp], rb[lo:lo + cpp]], axis=0)
            odd = jnp.concatenate([ra[hi:hi + cpp], rb[hi:hi + cpp]], axis=0)
            xs_ref[0, l2, pl.ds(start, 2 * cpp), :] = jnp.where(first, even, pltpu.roll(odd, HEAD_DIM, 1)).astype(BF16)
            xs_ref[1, l2, pl.ds(start, 2 * cpp), :] = jnp.where(first, pltpu.roll(even, HEAD_DIM, 1), odd).astype(BF16)
        return carry

    lax.fori_loop(0, n_pages // 2, to_rows, 0, unroll=8)
    x = jnp.concatenate([jnp.concatenate([xs_ref[k, l2] for l2 in range(CMP_STRIDE // 2)], axis=1)
                         for k in range(NSA_KV_HEADS)], axis=0)
    hid = w1_ref.shape[1] // 2
    bias = bias_ref[:, 0:hid] + bias_ref[:, 2 * hid:3 * hid]
    p = _dot(x, w1_ref[...])
    hs = []
    for k in range(NSA_KV_HEADS):
        pk = p[k * n_chunks:(k + 1) * n_chunks]
        hs.append(_silu(pk[:, :hid] + pltpu.roll(pk[:, hid:], n_chunks - 1, 0) + bias).astype(BF16))
    return _dot(jnp.concatenate(hs, axis=1), w2_ref[...])


def _cmp_bias_kernel(kpe_ref, kw_ref, vpe_ref, vw_ref, kb_ref, vb_ref):
    for pe_ref, w_ref, b_ref in ((kpe_ref, kw_ref, kb_ref), (vpe_ref, vw_ref, vb_ref)):
        halves = [jnp.dot(pe_ref[r], w_ref[r], preferred_element_type=F32, precision=HIGHEST) for r in range(2)]
        b_ref[...] = jnp.concatenate(halves, axis=1)


def _cmp_bias(kpe, kw1, vpe, vw1):
    full = lambda a: pl.BlockSpec(a.shape, lambda i: (0,) * a.ndim)
    width = 2 * kw1.shape[2]
    out = pl.BlockSpec((1, width), lambda i: (0, 0))
    return pl.pallas_call(
        _cmp_bias_kernel,
        grid=(1,),
        in_specs=[full(kpe), full(kw1), full(vpe), full(vw1)],
        out_specs=[out, out],
        out_shape=[jax.ShapeDtypeStruct((1, width), F32)] * 2,
        compiler_params=pltpu.CompilerParams(dimension_semantics=("arbitrary",), vmem_limit_bytes=VMEM_LIMIT),
        name="cmp_bias",
    )(kpe, kw1, vpe, vw1)


def _nsa_sample_kernel(pt_ref, qn_ref, qrt_ref, qr_ref, gate_ref, sknew_ref, svnewt_ref, wknew_ref, wknewt_ref, wvnewt_ref,
                       wk_ref, wv_ref, sort_ref, kb_ref, kw1_ref, kw2_ref, vb_ref, vw1_ref, vw2_ref, msel_ref,
                       ck_hbm, cv_hbm, sk_hbm, sv_hbm, o_ref, wko_ref, wvo_ref,
                       ckbuf, cvbuf, skbuf, svbuf, xs_ref, sc_ref, sem_ck, sem_cv, sem_sk, sem_sv, *, n_cmp, n_sel):
    n_pages, _, _, page = skbuf.shape[1:]
    slot = _paged_step((ck_hbm, cv_hbm, sk_hbm, sv_hbm), (ckbuf, cvbuf, skbuf, svbuf),
                       (sem_ck, sem_cv, sem_sk, sem_sv), pt_ref, n_pages)
    n_chunks = n_pages * page // CMP_STRIDE
    qr_t = qrt_ref[0] * SCALE
    gates = gate_ref[0]
    ckc = _compress_pages(ckbuf, slot, xs_ref, sort_ref, kb_ref, kw1_ref, kw2_ref)
    cvc = _compress_pages(cvbuf, slot, xs_ref, sort_ref, vb_ref, vw1_ref, vw2_ref)
    n_idx = lax.broadcasted_iota(jnp.int32, (NSA_HEADS, n_chunks), 1)
    cmask = n_idx < n_cmp
    s = jnp.where(cmask, _dot_nt((qn_ref[0] * SCALE).astype(BF16), ckc.astype(BF16)), NEG_INF)
    m = jnp.max(s, axis=1, keepdims=True)
    e = jnp.where(cmask, jnp.exp(s - m), 0.0)
    p = e / jnp.maximum(jnp.sum(e, axis=1, keepdims=True), 1e-30)
    o_c = _dot(p.astype(BF16), cvc.astype(BF16))
    eye = jnp.where(lax.broadcasted_iota(jnp.int32, (D_NSA_KV, D_NSA_KV), 0)
                    == lax.broadcasted_iota(jnp.int32, (D_NSA_KV, D_NSA_KV), 1), 1.0, 0.0)
    o_c_t = _dot_nt(eye, o_c, precision=HIGHEST)
    p_kv = jnp.concatenate([jnp.sum(p[k * NSA_GROUP:(k + 1) * NSA_GROUP], axis=0, keepdims=True)
                            for k in range(NSA_KV_HEADS)], axis=0)
    nsb_pad = msel_ref.shape[1]
    score = jnp.dot(p_kv, msel_ref[...], preferred_element_type=F32, precision=HIGHEST)
    jb = lax.broadcasted_iota(jnp.int32, (NSA_KV_HEADS, nsb_pad), 1)
    cur = n_sel - 1
    valid = jb < n_sel
    forced = (jb == 0) | (jb == cur) | (jb == cur - 1)
    score = jnp.where(valid, jnp.where(forced, jnp.inf, score), NEG_INF)
    rank = _rank(score, n_sel, 1)
    bpp = page // SEL_LEN
    lane = lax.broadcasted_iota(jnp.int32, (1, page), 1)
    entries = [[] for _ in range(NSA_HEADS)]
    chosen = [_ranked_index(rank, valid, r) for r in range(SEL_TOPN)]
    for k in range(NSA_KV_HEADS):
        qb = [jnp.broadcast_to(qr_t[:, k * NSA_GROUP + g:k * NSA_GROUP + g + 1], (HEAD_DIM, page))
              for g in range(NSA_GROUP)]
        for r in range(SEL_TOPN):
            blk = chosen[r][k, 0]
            pg = jnp.minimum(blk // bpp, n_pages - 1)
            keep = lane // SEL_LEN == jnp.where(blk < cur, blk % bpp, -1)
            kt = skbuf[slot, pg, k]
            for g in range(NSA_GROUP):
                h = k * NSA_GROUP + g
                sc_ref[r, h:h + 1, :] = jnp.where(keep, jnp.sum(kt * qb[g], axis=0, keepdims=True), NEG_INF)
                entries[h].append((pg, r))
    qr = qr_ref[0] * SCALE
    s_new = jnp.sum(qr * sknew_ref[0], axis=1, keepdims=True)
    s = sc_ref[...]
    m = jnp.maximum(jnp.max(jnp.max(s, axis=0), axis=1, keepdims=True), s_new)
    e = jnp.exp(s - m[None])
    e_new = jnp.exp(s_new - m)
    inv = 1.0 / jnp.maximum(jnp.sum(jnp.sum(e, axis=0), axis=1, keepdims=True) + e_new, 1e-30)
    sc_ref[...] = e * inv[None]
    cols_s = _decode_values(svbuf, slot, sc_ref, entries, e_new * inv, svnewt_ref[0], NSA_GROUP)
    w_buf = wk_ref.shape[3]
    widx = lax.broadcasted_iota(jnp.int32, (NSA_HEADS, w_buf), 1)
    hrow = lax.broadcasted_iota(jnp.int32, (NSA_HEADS, w_buf), 0)
    sw = jnp.zeros((NSA_HEADS, w_buf), F32)
    for h in range(NSA_HEADS):
        kt = wk_ref[0, h // NSA_GROUP]
        sw = jnp.where(hrow == h, jnp.sum(kt * qr_t[:, h:h + 1], axis=0, keepdims=True), sw)
    sw = jnp.where(widx > w_buf - WINDOW, sw, NEG_INF)
    sw_new = jnp.sum(qr * wknew_ref[0], axis=1, keepdims=True)
    mw = jnp.maximum(jnp.max(sw, axis=1, keepdims=True), sw_new)
    ew = jnp.exp(sw - mw)
    ew_new = jnp.exp(sw_new - mw)
    invw = 1.0 / jnp.maximum(jnp.sum(ew, axis=1, keepdims=True) + ew_new, 1e-30)
    pw = ew * invw
    pw_new = ew_new * invw
    wvnew_t = wvnewt_ref[0]
    cols = []
    for h in range(NSA_HEADS):
        k = h // NSA_GROUP
        o_w = jnp.sum(wv_ref[0, k] * pw[h:h + 1, :], axis=1, keepdims=True) + pw_new[h:h + 1, 0:1] * wvnew_t[:, k:k + 1]
        o_cmp = o_c_t[k * HEAD_DIM:(k + 1) * HEAD_DIM, h:h + 1]
        cols.append(gates[0:1, h:h + 1] * o_cmp + gates[1:2, h:h + 1] * cols_s[h] + gates[2:3, h:h + 1] * o_w)
    o_ref[0] = _place_cols(cols)
    last = lax.broadcasted_iota(jnp.int32, (HEAD_DIM, w_buf), 1) == w_buf - 1
    wknew_t = wknewt_ref[0]
    for k in range(NSA_KV_HEADS):
        wko_ref[0, k] = jnp.where(last, wknew_t[:, k:k + 1], pltpu.roll(wk_ref[0, k], w_buf - 1, 1))
        wvo_ref[0, k] = jnp.where(last, wvnew_t[:, k:k + 1], pltpu.roll(wv_ref[0, k], w_buf - 1, 1))


def _expand_heads(q, n_heads, n_kv):
    b = q.shape[0]
    group = n_heads // n_kv
    place = jnp.asarray((np.arange(n_heads)[:, None] // group == np.arange(n_kv)[None, :]).astype(np.float32))
    q4 = q.reshape(b, n_heads, 1, HEAD_DIM) * place[None, :, :, None]
    return q4.reshape(b, n_heads, n_kv * HEAD_DIM)


def _nsa_sample(page_table, qn, qr, gates, sk_new, sv_new, wk_new, wv_new, state_wk, state_wv, kw, vw,
                cache_ck, cache_cv, cache_sk, cache_sv):
    db, n_pages = page_table.shape
    page = cache_ck.shape[1]
    n_past = n_pages * page
    assert page == LANE and page % SEL_LEN == 0 and n_pages % 2 == 0
    n_chunks = n_past // CMP_STRIDE
    n_cmp = n_chunks - CMP_LEN // CMP_STRIDE + 1
    n_sel = -(-(n_past + 1) // SEL_LEN)
    assert n_sel >= SEL_TOPN
    nsb_pad = -(-n_sel // LANE) * LANE
    w_buf = state_wk.shape[1]
    heads = lambda a, n: a.reshape(db, n, HEAD_DIM)
    t = lambda a: jnp.transpose(a, (0, 2, 1))
    rep = lambda a: jnp.repeat(heads(a, NSA_KV_HEADS), NSA_GROUP, axis=1)
    qr3 = heads(qr, NSA_HEADS)
    msel = jnp.asarray(_cmp_to_sel(n_chunks, n_cmp, nsb_pad, n_sel))
    small = [_expand_heads(qn, NSA_HEADS, NSA_KV_HEADS), t(qr3), qr3, t(gates.reshape(db, NSA_HEADS, 3)),
             rep(sk_new), t(heads(sv_new, NSA_KV_HEADS)), rep(wk_new), t(heads(wk_new, NSA_KV_HEADS)),
             t(heads(wv_new, NSA_KV_HEADS))]
    kb, vb = _cmp_bias(kw[0], kw[1], vw[0], vw[1])
    def cat(w1):
        one = w1[:, :, :CMP_HIDDEN].reshape(2, CMP_STRIDE, D_NSA_KV, CMP_HIDDEN)[:, :, :HEAD_DIM, :]
        return jnp.transpose(one, (1, 2, 0, 3)).reshape(CMP_STRIDE * HEAD_DIM, 2 * CMP_HIDDEN).astype(BF16)
    consts = [jnp.asarray(_row_sort_matrix(page)).astype(BF16), kb, cat(kw[1]), kw[2], vb, cat(vw[1]), vw[2], msel]
    wk_t, wv_t = _native(state_wk), _native(state_wv)
    per_b = lambda a: pl.BlockSpec((1,) + a.shape[1:], lambda b, pt: (b,) + (0,) * (a.ndim - 1))
    const = lambda a: pl.BlockSpec(a.shape, lambda b, pt: (0,) * a.ndim)
    any_spec = pl.BlockSpec(memory_space=pl.ANY)
    buf = pltpu.VMEM((2, n_pages, NSA_KV_HEADS, HEAD_DIM, page), F32)
    wspec = pl.BlockSpec((1, NSA_KV_HEADS, HEAD_DIM, w_buf), lambda b, pt: (b, 0, 0, 0))
    grid_spec = pltpu.PrefetchScalarGridSpec(
        num_scalar_prefetch=1,
        grid=(db,),
        in_specs=[per_b(a) for a in small] + [wspec, wspec] + [const(a) for a in consts] + [any_spec] * 4,
        out_specs=[pl.BlockSpec((1, HEAD_DIM, NSA_HEADS), lambda b, pt: (b, 0, 0)), wspec, wspec],
        scratch_shapes=[buf, buf, buf, buf,
                        pltpu.VMEM((NSA_KV_HEADS, CMP_STRIDE // 2, n_chunks, D_NSA_KV), BF16),
                        pltpu.VMEM((SEL_TOPN, NSA_HEADS, page), F32)] + [pltpu.SemaphoreType.DMA((2,))] * 4)
    o_t, wko, wvo = pl.pallas_call(
        functools.partial(_nsa_sample_kernel, n_cmp=n_cmp, n_sel=n_sel),
        grid_spec=grid_spec,
        out_shape=[jax.ShapeDtypeStruct((db, HEAD_DIM, NSA_HEADS), F32),
                   jax.ShapeDtypeStruct(wk_t.shape, F32), jax.ShapeDtypeStruct(wv_t.shape, F32)],
        compiler_params=pltpu.CompilerParams(dimension_semantics=("arbitrary",), vmem_limit_bytes=VMEM_LIMIT),
        name="nsa_sample",
    )(page_table, *small, wk_t, wv_t, *consts,
      _native(cache_ck), _native(cache_cv), _native(cache_sk), _native(cache_sv))
    back = lambda a: jnp.transpose(a, (0, 3, 1, 2))
    return jnp.transpose(o_t, (0, 2, 1)).reshape(db, D_NSA_Q), back(wko), back(wvo)


def _ffn_weights(w_in, w_out):
    return w_in.astype(BF16), w_out.astype(BF16)


def kernel(x_prompt, x_sample, cache_moba_k, cache_moba_v, cache_nsa_cmp_k, cache_nsa_cmp_v, cache_nsa_sel_k, cache_nsa_sel_v, state_nsa_win_k, state_nsa_win_v, page_table, c_prompt, c_sample, w_ada, b_ada, norm_ffn1, w_ffn1_in, w_ffn1_out, norm_mix, w_mix_in, w_mix_out, norm_ffn2, w_ffn2_in, w_ffn2_out, cmp_k_pe, cmp_k_w1, cmp_k_w2, cmp_v_pe, cmp_v_w1, cmp_v_w2, norm_final):
    depth = w_ada.shape[0]
    assert depth == 1, "single-layer step"
    b, s, d = x_prompt.shape
    db = x_sample.shape[0]
    assert x_sample.shape[1] == 1
    n_pages = page_table.shape[1]
    page = cache_moba_k.shape[2]
    n_past = n_pages * page
    assert s % Q_TILE == 0 and db % 8 == 0
    l = 0

    ffn1_w = _ffn_weights(w_ffn1_in[l], w_ffn1_out[l])
    ffn2_w = _ffn_weights(w_ffn2_in[l], w_ffn2_out[l])
    w_proj = jnp.pad(w_mix_in[l], ((0, 0), (0, PROJ_PAD - w_mix_in.shape[2]))).astype(BF16)
    w_mo = w_mix_out[l].astype(BF16)
    w_mo_m, w_mo_n = w_mo[:D_MOBA_Q], w_mo[D_MOBA_Q:]
    kw = _compress_weights(cmp_k_pe[l], cmp_k_w1[l], cmp_k_w2[l])
    vw = _compress_weights(cmp_v_pe[l], cmp_v_w1[l], cmp_v_w2[l])

    mods = _ada_mods(jnp.concatenate([c_sample, c_prompt], axis=0), w_ada[l], b_ada[l])
    mods_p = _Mods(mods, db, per_row=False)
    mods_s = _Mods(mods, db, per_row=True)
    xs = x_sample.reshape(1, db, d)
    tm_p = 512 if s % 512 == 0 else Q_TILE

    xp1 = _ffn(x_prompt, mods_p, (0, 1, 2), norm_ffn1[l], *ffn1_w, tm=tm_p)
    xs1 = _ffn(xs, mods_s, (0, 1, 2), norm_ffn1[l], *ffn1_w, tm=db)

    tabs_p = _rope_tables(jnp.arange(s, dtype=jnp.int32))
    pp = _proj(xp1, mods_p, (3, 4), norm_mix[l], w_proj, tabs_p, tm=tm_p, head_major=True)
    pr, ph = pp["rows"], pp["heads"]
    om = _moba_prompt(pr["mq"], ph["mq"], pr["mk"], ph["mk"], ph["mv"])
    bf = lambda w: (w[0], w[1].astype(BF16), w[2])
    ckc, cvc = _compress_prompt(pr["ck"], pr["cv"], bf(kw), bf(vw))
    on = _nsa_prompt(ph["nqn"], ph["nq"], pr["gate"], ckc, cvc, ph["sk"], ph["sv"], ph["wk"], ph["wv"])

    tabs_s = _rope_tables(jnp.full((db,), n_past, dtype=jnp.int32))
    ps = _proj(xs1, mods_s, (3, 4), norm_mix[l], w_proj, tabs_s, tm=db, head_major=False)
    sr = {n: a[0] for n, a in ps["rows"].items()}
    o_m_s = _moba_sample(page_table, sr["mq"], sr["mk"], sr["mv"], cache_moba_k[l], cache_moba_v[l])
    o_n_s, win_k_s, win_v_s = _nsa_sample(
        page_table, sr["nqn"], sr["nq"], sr["gate"][:, :N_GATES], sr["sk"], sr["sv"], sr["wk"], sr["wv"],
        state_nsa_win_k[l], state_nsa_win_v[l], kw, vw,
        cache_nsa_cmp_k[l], cache_nsa_cmp_v[l], cache_nsa_sel_k[l], cache_nsa_sel_v[l])
    om_s = o_m_s.reshape(1, db, D_MOBA_Q).astype(BF16)
    on_s = o_n_s.reshape(1, db, D_NSA_Q).astype(BF16)

    yp = _ffn(xp1, mods_p, (6, 7, 8), norm_ffn2[l], *ffn2_w, tm=tm_p,
              mix=(om, on, w_mo_m, w_mo_n, 5), final_g=norm_final)
    ys = _ffn(xs1, mods_s, (6, 7, 8), norm_ffn2[l], *ffn2_w, tm=db,
              mix=(om_s, on_s, w_mo_m, w_mo_n, 5), final_g=norm_final)

    w_keep = min(WINDOW, s)
    back = lambda a: jnp.transpose(a, (0, 3, 1, 2))[None]
    st = lambda n: back(pp["state"][n])
    ss_ = lambda n: jnp.transpose(ps["state"][n], (0, 3, 1, 2)).reshape(1, db, 1, -1, HEAD_DIM)
    win = lambda n: back(pp["state"][n][:, :, :, s - w_keep:])
    return (yp, ys.reshape(db, 1, d),
            st("mk"), ss_("mk"), st("mv"), ss_("mv"), st("ck"), ss_("ck"), st("cv"), ss_("cv"),
            st("sk"), ss_("sk"), st("sv"), ss_("sv"),
            win("wk"), win_k_s[None], win("wv"), win_v_s[None])
```

```python
import functools

import numpy as np
import jax
import jax.numpy as jnp
from jax import lax
from jax.experimental import pallas as pl
from jax.experimental.pallas import tpu as pltpu

F32 = jnp.float32
BF16 = jnp.bfloat16
NEG_INF = float("-inf")
HIGHEST = lax.Precision.HIGHEST

HEAD_DIM = 64
MOBA_HEADS = 8
MOBA_KV_HEADS = 4
MOBA_GROUP = MOBA_HEADS // MOBA_KV_HEADS
NSA_HEADS = 8
NSA_KV_HEADS = 2
NSA_GROUP = NSA_HEADS // NSA_KV_HEADS
ROPE_DIM = HEAD_DIM // 4
ROPE_THETA = 500000.0
MOBA_BLOCK = 256
MOBA_TOPK = 3
CMP_LEN = 32
CMP_STRIDE = 16
CMP_HIDDEN = 2 * HEAD_DIM
SEL_LEN = 64
SEL_TOPN = 16
WINDOW = 512
MACARON_WEIGHT = 0.5
N_MOD = 9
EPS = 1e-6
SCALE = HEAD_DIM ** -0.5
D_MOBA_Q = MOBA_HEADS * HEAD_DIM
D_MOBA_KV = MOBA_KV_HEADS * HEAD_DIM
D_NSA_Q = NSA_HEADS * HEAD_DIM
D_NSA_KV = NSA_KV_HEADS * HEAD_DIM
N_GATES = 3 * NSA_HEADS
LANE = 128
PROJ_PAD = 2432
VMEM_LIMIT = 56 * 1024 * 1024
Q_TILE = 256
FFN_CHUNK = 256
MOBA_KV_PER_PASS = 4

OFF_MQ, OFF_MK, OFF_MV, OFF_NQ = 0, 512, 768, 1024
OFF_CK, OFF_CV, OFF_SK, OFF_SV, OFF_WK, OFF_WV, OFF_G = 1536, 1664, 1792, 1920, 2048, 2176, 2304


def _silu(x):
    return x / (1.0 + jnp.exp(-x))


def _sigmoid(x):
    return 1.0 / (1.0 + jnp.exp(-x))


def _dot(a, b):
    return jnp.dot(a, b, preferred_element_type=F32)


def _dot_nt(a, b, precision=None):
    return lax.dot_general(a, b, (((1,), (1,)), ((), ())), preferred_element_type=F32, precision=precision)


def _modulated(x, g, shift, scale):
    ms = jnp.mean(x * x, axis=-1, keepdims=True)
    return (x * lax.rsqrt(ms + EPS) * g) * (1.0 + scale) + shift


def _rank(sc, ncols, axis):
    idx = lax.broadcasted_iota(jnp.int32, sc.shape, axis)
    rank = jnp.zeros(sc.shape, F32)
    for i in range(ncols):
        ci = sc[:, i:i + 1] if axis == 1 else sc[i:i + 1, :]
        beats = (ci > sc) | ((ci == sc) & (idx > i))
        rank = rank + jnp.where(beats, 1.0, 0.0)
    return rank


def _ada_kernel(c_ref, w_ref, b_ref, o_ref):
    sc = _silu(c_ref[...]).astype(BF16)
    o_ref[0] = _dot(sc, w_ref[...].astype(BF16)) + b_ref[0]


def _ada_mods(c_all, w_ada, b_ada):
    rows, d = c_all.shape
    return pl.pallas_call(
        _ada_kernel,
        grid=(N_MOD,),
        in_specs=[pl.BlockSpec((rows, d), lambda j: (0, 0)),
                  pl.BlockSpec((d, d), lambda j: (0, j)),
                  pl.BlockSpec((1, 1, d), lambda j: (j, 0, 0))],
        out_specs=pl.BlockSpec((1, rows, d), lambda j: (j, 0, 0)),
        out_shape=jax.ShapeDtypeStruct((N_MOD, rows, d), F32),
        compiler_params=pltpu.CompilerParams(dimension_semantics=("arbitrary",), vmem_limit_bytes=VMEM_LIMIT),
        name="ada_mods",
    )(c_all, w_ada, b_ada.reshape(N_MOD, 1, d))


class _Mods:
    def __init__(self, mods, n_sample, per_row):
        self.per_row = per_row
        self.n_sample = n_sample
        self.rows = mods.shape[1]
        d = mods.shape[2]
        self.d = d
        self.arr = mods if per_row else mods.reshape(N_MOD * self.rows, 1, d)

    def spec(self, j):
        if self.per_row:
            return pl.BlockSpec((1, self.n_sample, self.d), lambda g, i: (j, 0, 0))
        base = j * self.rows + self.n_sample
        return pl.BlockSpec((1, 1, self.d), lambda g, i: (base + g, 0, 0))


def _ffn_kernel(*refs, has_mix, final_norm, n_chunks):
    it = iter(refs)
    x_ref = next(it)
    if has_mix:
        mm_ref, mn_ref, wmm_ref, wmn_ref, gmix_ref = next(it), next(it), next(it), next(it), next(it)
    sh_ref, sc_ref, gt_ref, g_ref, win_ref, wo_ref = (next(it) for _ in range(6))
    gf_ref = next(it) if final_norm else None
    o_ref, h_ref = next(it), next(it)

    x = x_ref[0]
    if has_mix:
        x = x + gmix_ref[0] * (_dot(mm_ref[0], wmm_ref[...]) + _dot(mn_ref[0], wmn_ref[...]))
    xm = _modulated(x, g_ref[...], sh_ref[0], sc_ref[0]).astype(BF16)
    f = wo_ref.shape[0]
    tf = f // n_chunks
    for j in range(n_chunks):
        a = _dot(xm, win_ref[:, j * tf:(j + 1) * tf])
        b = _dot(xm, win_ref[:, f + j * tf:f + (j + 1) * tf])
        h_ref[:, j * tf:(j + 1) * tf] = (_silu(a) * b).astype(BF16)
    y = x + (MACARON_WEIGHT * gt_ref[0]) * _dot(h_ref[...], wo_ref[...])
    if final_norm:
        ms = jnp.mean(y * y, axis=-1, keepdims=True)
        y = y * lax.rsqrt(ms + EPS) * gf_ref[...]
    o_ref[0] = y


def _ffn(x, mods, jmods, norm_g, win, wo, tm, mix=None, final_g=None):
    g_, r_, d = x.shape
    f = wo.shape[0]
    assert f % FFN_CHUNK == 0
    n_chunks = f // FFN_CHUNK
    const2 = lambda g, i: (0, 0)
    row_spec = lambda w: pl.BlockSpec((1, tm, w), lambda g, i: (g, i, 0))
    args, specs = [x], [row_spec(d)]
    if mix is not None:
        mm, mn, wmm, wmn, jmix = mix
        args += [mm, mn, wmm, wmn, mods.arr]
        specs += [row_spec(mm.shape[2]), row_spec(mn.shape[2]),
                  pl.BlockSpec(wmm.shape, const2), pl.BlockSpec(wmn.shape, const2), mods.spec(jmix)]
    args += [mods.arr, mods.arr, mods.arr, norm_g.reshape(1, d), win, wo]
    specs += [mods.spec(jmods[0]), mods.spec(jmods[1]), mods.spec(jmods[2]),
              pl.BlockSpec((1, d), const2), pl.BlockSpec(win.shape, const2), pl.BlockSpec(wo.shape, const2)]
    if final_g is not None:
        args.append(final_g.reshape(1, d))
        specs.append(pl.BlockSpec((1, d), const2))
    kern = functools.partial(_ffn_kernel, has_mix=mix is not None, final_norm=final_g is not None, n_chunks=n_chunks)
    return pl.pallas_call(
        kern,
        grid=(g_, r_ // tm),
        in_specs=specs,
        out_specs=row_spec(d),
        out_shape=jax.ShapeDtypeStruct((g_, r_, d), F32),
        scratch_shapes=[pltpu.VMEM((tm, f), BF16)],
        compiler_params=pltpu.CompilerParams(dimension_semantics=("arbitrary", "arbitrary"),
                                             vmem_limit_bytes=VMEM_LIMIT),
        name="ffn",
    )(*args)


def _rope_tables(pos):
    half = ROPE_DIM // 2
    inv_freq = ROPE_THETA ** (-jnp.arange(half, dtype=F32) / half)
    ang = pos.astype(F32)[:, None] * inv_freq
    cos, sin = jnp.cos(ang), jnp.sin(ang)
    rows = pos.shape[0]
    rest = HEAD_DIM - ROPE_DIM
    one, zero = jnp.ones((rows, rest), F32), jnp.zeros((rows, rest), F32)
    zh = jnp.zeros((rows, half), F32)
    c = jnp.concatenate([cos, cos, one], axis=1)
    sa = jnp.concatenate([-sin, zh, zero], axis=1)
    sb = jnp.concatenate([zh, sin, zero], axis=1)
    tile = lambda t: jnp.concatenate([t, t], axis=1)
    return tile(c), tile(sa), tile(sb)


_STATE_HEADS = (("mk", MOBA_KV_HEADS), ("mv", MOBA_KV_HEADS)) + tuple(
    (n, NSA_KV_HEADS) for n in ("ck", "cv", "sk", "sv", "wk", "wv"))


def _proj_outputs(head_major):
    rows = [("mq", D_MOBA_Q), ("mk", D_MOBA_KV), ("ck", D_NSA_KV), ("cv", D_NSA_KV), ("gate", LANE)]
    if not head_major:
        rows += [("mv", D_MOBA_KV), ("sk", D_NSA_KV), ("sv", D_NSA_KV), ("wk", D_NSA_KV), ("wv", D_NSA_KV),
                 ("nqn", D_NSA_Q), ("nq", D_NSA_Q)]
    outs = [(n, "rows", w) for n, w in rows] + [(n, "state", h) for n, h in _STATE_HEADS]
    if head_major:
        outs += [("mq", "heads", MOBA_HEADS), ("mk", "heads", MOBA_KV_HEADS), ("mv", "heads", MOBA_KV_HEADS),
                 ("nqn", "heads", NSA_HEADS), ("nq", "heads", NSA_HEADS)]
        outs += [(n, "heads", NSA_KV_HEADS) for n in ("sk", "sv", "wk", "wv")]
    return outs


def _proj_kernel(*refs, head_major):
    (x_ref, sh_ref, sc_ref, g_ref, w_ref, cos_ref, sa_ref, sb_ref) = refs[:8]
    outs = refs[8:]
    xm = _modulated(x_ref[0], g_ref[...], sh_ref[0], sc_ref[0]).astype(BF16)
    cos, sa, sb = cos_ref[...], sa_ref[...], sb_ref[...]
    half = ROPE_DIM // 2

    def seg(lo, width):
        return _dot(xm, w_ref[:, lo:lo + width])

    def rope(y):
        parts = []
        for c in range(y.shape[1] // LANE):
            s = y[:, c * LANE:(c + 1) * LANE]
            parts.append(s * cos + pltpu.roll(s, LANE - half, 1) * sa + pltpu.roll(s, half, 1) * sb)
        return parts[0] if len(parts) == 1 else jnp.concatenate(parts, axis=1)

    mq = rope(seg(OFF_MQ, D_MOBA_Q))
    mk = rope(seg(OFF_MK, D_MOBA_KV))
    mv = seg(OFF_MV, D_MOBA_KV)
    nqn = seg(OFF_NQ, D_NSA_Q)
    nq = rope(nqn)
    ck = seg(OFF_CK, D_NSA_KV)
    cv = seg(OFF_CV, D_NSA_KV)
    sk = rope(seg(OFF_SK, D_NSA_KV))
    sv = seg(OFF_SV, D_NSA_KV)
    wk = rope(seg(OFF_WK, D_NSA_KV))
    wv = seg(OFF_WV, D_NSA_KV)
    vals = dict(mq=mq, mk=mk, mv=mv, nqn=nqn, nq=nq, ck=ck, cv=cv, sk=sk, sv=sv, wk=wk, wv=wv,
                gate=_sigmoid(seg(OFF_G, LANE)))
    pre_scaled = ("mq", "nqn", "nq")
    for ref, (name, kind, n) in zip(outs, _proj_outputs(head_major)):
        val = vals[name]
        if kind == "rows":
            ref[0] = val
        elif kind == "state":
            ref[0] = val.T.reshape(n, HEAD_DIM, val.shape[0])
        else:
            for h in range(n):
                piece = val[:, h * HEAD_DIM:(h + 1) * HEAD_DIM]
                ref[0, h] = (piece * SCALE if name in pre_scaled else piece).astype(BF16)


def _proj(x, mods, jmods, norm_g, w_pad, tables, tm, head_major):
    g_, r_, d = x.shape
    const2 = lambda g, i: (0, 0)
    row_spec = lambda w: pl.BlockSpec((1, tm, w), lambda g, i: (g, i, 0))
    tab_spec = pl.BlockSpec((tm, LANE), lambda g, i: (i, 0))
    in_specs = [row_spec(d), mods.spec(jmods[0]), mods.spec(jmods[1]), pl.BlockSpec((1, d), const2),
                pl.BlockSpec(w_pad.shape, const2), tab_spec, tab_spec, tab_spec]
    out_specs, out_shape = [], []
    layout = _proj_outputs(head_major)
    for _, kind, n in layout:
        if kind == "rows":
            out_specs.append(row_spec(n))
            out_shape.append(jax.ShapeDtypeStruct((g_, r_, n), F32))
        elif kind == "state":
            out_specs.append(pl.BlockSpec((1, n, HEAD_DIM, tm), lambda g, i: (g, 0, 0, i)))
            out_shape.append(jax.ShapeDtypeStruct((g_, n, HEAD_DIM, r_), F32))
        else:
            out_specs.append(pl.BlockSpec((1, n, tm, HEAD_DIM), lambda g, i: (g, 0, i, 0)))
            out_shape.append(jax.ShapeDtypeStruct((g_, n, r_, HEAD_DIM), BF16))
    res = pl.pallas_call(
        functools.partial(_proj_kernel, head_major=head_major),
        grid=(g_, r_ // tm),
        in_specs=in_specs,
        out_specs=out_specs,
        out_shape=out_shape,
        compiler_params=pltpu.CompilerParams(dimension_semantics=("arbitrary", "arbitrary"),
                                             vmem_limit_bytes=VMEM_LIMIT),
        name="proj",
    )(x, mods.arr, mods.arr, norm_g.reshape(1, d), w_pad, *tables)
    out = {"rows": {}, "state": {}, "heads": {}}
    for arr, (name, kind, _) in zip(res, layout):
        out[kind][name] = arr
    return out


def _compress_mlp(p0, p1, w2_ref):
    h = p0 + pltpu.roll(p1, p0.shape[0] - 1, 0)
    return _dot(_silu(h).astype(BF16), w2_ref[...])


def _compress_rows(x, pe_ref, w1_ref, w2_ref):
    p0 = _dot((x + pe_ref[0]).astype(BF16), w1_ref[0])
    p1 = _dot((x + pe_ref[1]).astype(BF16), w1_ref[1])
    return _compress_mlp(p0, p1, w2_ref)


def _compress_kernel(ck_ref, cv_ref, kpe_ref, kw1_ref, kw2_ref, vpe_ref, vw1_ref, vw2_ref, okc_ref, ovc_ref):
    okc_ref[0] = _compress_rows(ck_ref[0], kpe_ref, kw1_ref, kw2_ref)
    ovc_ref[0] = _compress_rows(cv_ref[0], vpe_ref, vw1_ref, vw2_ref)


def _compress_weights(pe, w1, w2):
    ratio = CMP_LEN // CMP_STRIDE
    eye = jnp.eye(NSA_KV_HEADS, dtype=F32)
    w1r = w1.reshape(ratio, CMP_STRIDE, HEAD_DIM, CMP_HIDDEN)
    w1f = jnp.einsum("rlde,kK->rlkdKe", w1r, eye).reshape(ratio, CMP_STRIDE * D_NSA_KV, NSA_KV_HEADS * CMP_HIDDEN)
    pef = jnp.broadcast_to(pe.reshape(ratio, CMP_STRIDE, 1, HEAD_DIM),
                           (ratio, CMP_STRIDE, NSA_KV_HEADS, HEAD_DIM)).reshape(ratio, 1, CMP_STRIDE * D_NSA_KV)
    w2f = jnp.einsum("ed,kK->keKd", w2, eye).reshape(NSA_KV_HEADS * CMP_HIDDEN, D_NSA_KV)
    return pef, w1f, w2f.astype(BF16)


def _compress_prompt(ck, cv, kw, vw):
    b, s, _ = ck.shape
    n = s // CMP_STRIDE
    width = CMP_STRIDE * D_NSA_KV
    ck2, cv2 = ck.reshape(b, n, width), cv.reshape(b, n, width)
    row = pl.BlockSpec((1, n, width), lambda i: (i, 0, 0))
    wspecs = []
    for w in kw + vw:
        wspecs.append(pl.BlockSpec(w.shape, (lambda i: (0, 0, 0)) if w.ndim == 3 else (lambda i: (0, 0))))
    out = pl.BlockSpec((1, n, D_NSA_KV), lambda i: (i, 0, 0))
    return pl.pallas_call(
        _compress_kernel,
        grid=(b,),
        in_specs=[row, row] + wspecs,
        out_specs=[out, out],
        out_shape=[jax.ShapeDtypeStruct((b, n, D_NSA_KV), F32)] * 2,
        compiler_params=pltpu.CompilerParams(dimension_semantics=("arbitrary",), vmem_limit_bytes=VMEM_LIMIT),
        name="compress_prompt",
    )(ck2, cv2, *kw, *vw)


def _two_pass_init(mx_ref, l_ref, acc_ref):
    mx_ref[...] = jnp.full(mx_ref.shape, NEG_INF, F32)
    l_ref[...] = jnp.zeros(l_ref.shape, F32)
    acc_ref[...] = jnp.zeros(acc_ref.shape, F32)


def _pass1(j, s, s_ref, mx_ref, first=False):
    s_ref[j] = s
    m = s[:, 0:LANE]
    for c in range(1, s.shape[1] // LANE):
        m = jnp.maximum(m, s[:, c * LANE:(c + 1) * LANE])
    mx_ref[...] = m if first else jnp.maximum(mx_ref[...], m)


def _row_max(mx_ref, mb_ref):
    m = jnp.max(mx_ref[...], axis=1, keepdims=True)
    m = jnp.where(m == NEG_INF, 0.0, m)
    mb_ref[...] = jnp.broadcast_to(m, mb_ref.shape)


def _pass2(j, vs, s_ref, mb_ref, l_ref, acc_ref, first=False):
    mb = mb_ref[...]
    s = s_ref[j]
    parts = [jnp.exp(s[:, c * LANE:(c + 1) * LANE] - mb) for c in range(s.shape[1] // LANE)]
    tot = parts[0]
    for p in parts[1:]:
        tot = tot + p
    p = jnp.concatenate(parts, axis=1).astype(BF16)
    n = p.shape[0] // len(vs)
    upd = [_dot(p[i * n:(i + 1) * n], v) for i, v in enumerate(vs)]
    upd = upd[0] if len(upd) == 1 else jnp.concatenate(upd, axis=0)
    if first:
        l_ref[...] = tot
        acc_ref[...] = upd
    else:
        l_ref[...] += tot
        acc_ref[...] += upd


def _two_pass_out(l_ref, acc_ref, scale=None):
    l = jnp.maximum(jnp.sum(l_ref[...], axis=1, keepdims=True), 1e-30)
    return acc_ref[...] * ((1.0 / l) if scale is None else (scale / l))


def _add_bias(s, bias, groups):
    tq = bias.shape[0]
    return (s.reshape(groups, tq, s.shape[1]) + bias[None]).reshape(s.shape)


def _attn_scratch(rows, n_chunks, chunk):
    return [pltpu.VMEM((rows, HEAD_DIM), BF16),
            pltpu.VMEM((n_chunks, rows, chunk), F32),
            pltpu.VMEM((rows, LANE), F32),
            pltpu.VMEM((rows, LANE), F32),
            pltpu.VMEM((rows, LANE), F32),
            pltpu.VMEM((rows, HEAD_DIM), F32)]


def _moba_prompt_kernel(mq_ref, mqh_ref, mk_ref, mkh_ref, mvh_ref, o_ref,
                        kmean_ref, bias_ref, q_ref, s_ref, mx_ref, mb_ref, l_ref, acc_ref, *, n_blocks):
    qi = pl.program_id(1)
    tq = mq_ref.shape[1]
    per_kv = MOBA_GROUP * tq
    rows = MOBA_KV_PER_PASS * per_kv

    @pl.when(qi == 0)
    def _():
        for j in range(n_blocks):
            blk = mk_ref[0, j * MOBA_BLOCK:(j + 1) * MOBA_BLOCK, :]
            kmean_ref[j:j + 1, :] = jnp.sum(blk, axis=0, keepdims=True) * (1.0 / MOBA_BLOCK)

    blk_t = lax.broadcasted_iota(jnp.int32, (n_blocks, tq), 0)
    r_loc = lax.broadcasted_iota(jnp.int32, (rows, MOBA_BLOCK), 0) % tq
    c_loc = lax.broadcasted_iota(jnp.int32, (rows, MOBA_BLOCK), 1)
    pad = jnp.full((LANE - n_blocks, tq), NEG_INF, F32)

    pieces = []
    for k0 in range(0, MOBA_KV_HEADS, MOBA_KV_PER_PASS):
        kvs = list(range(k0, k0 + MOBA_KV_PER_PASS))
        for k in kvs:
            km = kmean_ref[:, k * HEAD_DIM:(k + 1) * HEAD_DIM]
            for g in range(MOBA_GROUP):
                h = k * MOBA_GROUP + g
                lo = (h - k0 * MOBA_GROUP) * tq
                qf = mq_ref[0, :, h * HEAD_DIM:(h + 1) * HEAD_DIM]
                sb = jnp.where(blk_t < qi, _dot_nt(km, qf, precision=HIGHEST), NEG_INF)
                sel = (_rank(sb, n_blocks, 0) < MOBA_TOPK) & (blk_t < qi)
                bias_t = jnp.concatenate([jnp.where(sel, 0.0, NEG_INF), pad], axis=0)
                bias_ref[lo:lo + tq, :] = bias_t.T
                q_ref[lo:lo + tq, :] = mqh_ref[0, h]
        def scores(j):
            ks = slice(j * MOBA_BLOCK, (j + 1) * MOBA_BLOCK)
            return jnp.concatenate([_dot_nt(q_ref[i * per_kv:(i + 1) * per_kv, :], mkh_ref[0, k, ks, :])
                                    for i, k in enumerate(kvs)], axis=0)

        for j in range(n_blocks):
            @pl.when(j < qi)
            def _():
                _pass1(j, scores(j) + bias_ref[:, j:j + 1], s_ref, mx_ref, first=j == 0)

            @pl.when(j == qi)
            def _():
                _pass1(j, jnp.where(c_loc <= r_loc, scores(j), NEG_INF), s_ref, mx_ref, first=j == 0)

        _row_max(mx_ref, mb_ref)
        for j in range(n_blocks):
            @pl.when(j <= qi)
            def _():
                vs = [mvh_ref[0, k, j * MOBA_BLOCK:(j + 1) * MOBA_BLOCK, :] for k in kvs]
                _pass2(j, vs, s_ref, mb_ref, l_ref, acc_ref, first=j == 0)

        o = _two_pass_out(l_ref, acc_ref)
        pieces += [o[i * tq:(i + 1) * tq, :] for i in range(MOBA_KV_PER_PASS * MOBA_GROUP)]
    o_ref[0] = jnp.concatenate(pieces, axis=1).astype(BF16)


def _moba_prompt(mq, mqh, mk, mkh, mvh):
    b, s, _ = mq.shape
    tq = Q_TILE
    assert tq == MOBA_BLOCK
    n_blocks = s // MOBA_BLOCK
    rows = MOBA_KV_PER_PASS * MOBA_GROUP * tq
    full = lambda n: pl.BlockSpec((1, n, s, HEAD_DIM), lambda i, j: (i, 0, 0, 0))
    return pl.pallas_call(
        functools.partial(_moba_prompt_kernel, n_blocks=n_blocks),
        grid=(b, s // tq),
        in_specs=[pl.BlockSpec((1, tq, D_MOBA_Q), lambda i, j: (i, j, 0)),
                  pl.BlockSpec((1, MOBA_HEADS, tq, HEAD_DIM), lambda i, j: (i, 0, j, 0)),
                  pl.BlockSpec((1, s, D_MOBA_KV), lambda i, j: (i, 0, 0)),
                  full(MOBA_KV_HEADS), full(MOBA_KV_HEADS)],
        out_specs=pl.BlockSpec((1, tq, D_MOBA_Q), lambda i, j: (i, j, 0)),
        out_shape=jax.ShapeDtypeStruct((b, s, D_MOBA_Q), BF16),
        scratch_shapes=[pltpu.VMEM((n_blocks, D_MOBA_KV), F32),
                        pltpu.VMEM((rows, LANE), F32)] + _attn_scratch(rows, n_blocks, MOBA_BLOCK),
        compiler_params=pltpu.CompilerParams(dimension_semantics=("arbitrary", "arbitrary"),
                                             vmem_limit_bytes=VMEM_LIMIT),
        name="moba_prompt",
    )(mq, mqh, mk, mkh, mvh)


def _cmp_to_sel(nc_pad, nc, nsb_pad, nsb):
    i = np.arange(nc_pad)[:, None]
    j = np.arange(nsb_pad)[None, :]
    start = i * CMP_STRIDE
    m = (start < (j + 1) * SEL_LEN) & (start + CMP_LEN > j * SEL_LEN) & (i < nc) & (j < nsb)
    return m.astype(np.float32)


def _block_expand(n_blocks_pad, n_blocks, block_len, chunk):
    n_chunks = n_blocks * block_len // chunk
    pos = np.arange(n_chunks * chunk).reshape(n_chunks, 1, chunk)
    j = np.arange(n_blocks_pad).reshape(1, n_blocks_pad, 1)
    return (pos // block_len == j).astype(np.float32)


def _nsa_prompt_kernel(nqnh_ref, nqh_ref, gate_ref, ckc_ref, cvc_ref, skh_ref, svh_ref, wkh_ref, wvh_ref, mselt_ref, exp_ref,
                       o_ref, sel_ref, oc_ref, os_ref, q_ref, s_ref, mx_ref, mb_ref, l_ref, acc_ref, *, n_cmp, n_sel):
    qi = pl.program_id(1)
    tq = nqnh_ref.shape[2]
    rows = NSA_GROUP * tq
    nc_pad = ckc_ref.shape[1]
    nsb_pad = mselt_ref.shape[0]
    n_chunks, _, chunk = exp_ref.shape
    q0 = qi * tq
    gates = gate_ref[0]

    qpos_c = q0 + lax.broadcasted_iota(jnp.int32, (rows, nc_pad), 0) % tq
    n_idx = lax.broadcasted_iota(jnp.int32, (rows, nc_pad), 1)
    cmp_mask = (n_idx * CMP_STRIDE + (CMP_LEN - 1) <= qpos_c) & (n_idx < n_cmp)

    t = q0 + lax.broadcasted_iota(jnp.int32, (nsb_pad, tq), 1)
    jb = lax.broadcasted_iota(jnp.int32, (nsb_pad, tq), 0)
    cur = t // SEL_LEN
    valid = (jb * SEL_LEN <= t) & (jb < n_sel)
    forced = (jb == 0) | (jb == cur) | (jb == cur - 1)
    pad = jnp.zeros((LANE - nsb_pad, tq), F32)

    r_loc = lax.broadcasted_iota(jnp.int32, (tq, chunk), 0)
    c_loc = lax.broadcasted_iota(jnp.int32, (tq, chunk), 1)
    lower = jnp.where(c_loc <= r_loc, 0.0, NEG_INF)
    upper = jnp.where(c_loc > r_loc, 0.0, NEG_INF)

    kvs = range(NSA_KV_HEADS)
    for k in kvs:
        lo = k * HEAD_DIM
        for g in range(NSA_GROUP):
            q_ref[g * tq:(g + 1) * tq, :] = nqnh_ref[0, k * NSA_GROUP + g]
        kc = ckc_ref[0, :, lo:lo + HEAD_DIM].astype(BF16)
        vc = cvc_ref[0, :, lo:lo + HEAD_DIM].astype(BF16)
        s = jnp.where(cmp_mask, _dot_nt(q_ref[0:rows, :], kc), NEG_INF)
        m = jnp.max(s, axis=1, keepdims=True)
        m = jnp.where(m == NEG_INF, 0.0, m)
        e = jnp.where(cmp_mask, jnp.exp(s - m), 0.0)
        p = e / jnp.maximum(jnp.sum(e, axis=1, keepdims=True), 1e-30)
        oc_ref[k * rows:(k + 1) * rows, :] = _dot(p.astype(BF16), vc)
        p_kv = p[0:tq]
        for g in range(1, NSA_GROUP):
            p_kv = p_kv + p[g * tq:(g + 1) * tq]
        score = _dot_nt(mselt_ref[...], p_kv, precision=HIGHEST)
        score = jnp.where(valid, jnp.where(forced, jnp.inf, score), NEG_INF)
        sel = (_rank(score, n_sel, 0) < SEL_TOPN) & valid
        sel_t = jnp.concatenate([jnp.where(sel, 1.0, 0.0), pad], axis=0)
        sel_ref[k] = sel_t.T[:, :nsb_pad].astype(BF16)
    for h in range(NSA_HEADS):
        q_ref[h * tq:(h + 1) * tq, :] = nqh_ref[0, h]

    def scores(k_ref, j, biases):
        ks = slice(j * chunk, (j + 1) * chunk)
        out = []
        for k in kvs:
            s = _dot_nt(q_ref[k * rows:(k + 1) * rows, :], k_ref[0, k, ks, :])
            out.append(s if biases is None else _add_bias(s, biases[k], NSA_GROUP))
        return jnp.concatenate(out, axis=0)

    def sel_bias(j, inside):
        return [jnp.where(_dot(sel_ref[k], exp_ref[j]) > 0.5, inside, NEG_INF) for k in kvs]

    def gate_rows(branch):
        return jnp.concatenate([gates[:, 3 * h + branch:3 * h + branch + 1] for h in range(NSA_HEADS)], axis=0)

    for j in range(n_chunks):
        @pl.when(j < qi)
        def _():
            _pass1(j, scores(skh_ref, j, sel_bias(j, 0.0)), s_ref, mx_ref, first=j == 0)

        @pl.when(j == qi)
        def _():
            _pass1(j, scores(skh_ref, j, sel_bias(j, lower)), s_ref, mx_ref, first=j == 0)

    _row_max(mx_ref, mb_ref)
    for j in range(n_chunks):
        @pl.when(j <= qi)
        def _():
            _pass2(j, [svh_ref[0, k, j * chunk:(j + 1) * chunk, :] for k in kvs], s_ref, mb_ref, l_ref, acc_ref,
                   first=j == 0)

    os_ref[...] = _two_pass_out(l_ref, acc_ref, gate_rows(1))
    _two_pass_init(mx_ref, l_ref, acc_ref)
    for j in range(n_chunks):
        @pl.when(j == qi)
        def _():
            _pass1(j, scores(wkh_ref, j, [lower] * NSA_KV_HEADS), s_ref, mx_ref)

        @pl.when(j == qi - 1)
        def _():
            _pass1(j, scores(wkh_ref, j, None), s_ref, mx_ref)

        @pl.when(j == qi - 2)
        def _():
            _pass1(j, scores(wkh_ref, j, [upper] * NSA_KV_HEADS), s_ref, mx_ref)

    _row_max(mx_ref, mb_ref)
    for j in range(n_chunks):
        @pl.when((j <= qi) & (j >= qi - 2))
        def _():
            _pass2(j, [wvh_ref[0, k, j * chunk:(j + 1) * chunk, :] for k in kvs], s_ref, mb_ref, l_ref, acc_ref)

    out = gate_rows(0) * oc_ref[...] + os_ref[...] + _two_pass_out(l_ref, acc_ref, gate_rows(2))
    o_ref[0] = jnp.concatenate([out[h * tq:(h + 1) * tq] for h in range(NSA_HEADS)], axis=1).astype(BF16)


def _nsa_prompt(nqnh, nqh, gates, ckc, cvc, skh, svh, wkh, wvh):
    b, _, s, _ = nqnh.shape
    tq = Q_TILE
    chunk = Q_TILE
    assert WINDOW == 2 * chunk
    n_chunks = s // CMP_STRIDE
    n_cmp = n_chunks - CMP_LEN // CMP_STRIDE + 1
    n_sel = -(-s // SEL_LEN)
    nsb_pad = -(-n_sel // 8) * 8
    mselt = jnp.asarray(_cmp_to_sel(n_chunks, n_cmp, nsb_pad, n_sel).T)
    expand = jnp.asarray(_block_expand(nsb_pad, n_sel, SEL_LEN, chunk)).astype(BF16)
    full = lambda n: pl.BlockSpec((1, n, s, HEAD_DIM), lambda i, j: (i, 0, 0, 0))
    qspec = pl.BlockSpec((1, NSA_HEADS, tq, HEAD_DIM), lambda i, j: (i, 0, j, 0))
    cspec = pl.BlockSpec((1, n_chunks, D_NSA_KV), lambda i, j: (i, 0, 0))
    rows = NSA_GROUP * tq
    return pl.pallas_call(
        functools.partial(_nsa_prompt_kernel, n_cmp=n_cmp, n_sel=n_sel),
        grid=(b, s // tq),
        in_specs=[qspec, qspec, pl.BlockSpec((1, tq, LANE), lambda i, j: (i, j, 0)), cspec, cspec,
                  full(NSA_KV_HEADS), full(NSA_KV_HEADS), full(NSA_KV_HEADS), full(NSA_KV_HEADS),
                  pl.BlockSpec(mselt.shape, lambda i, j: (0, 0)),
                  pl.BlockSpec(expand.shape, lambda i, j: (0, 0, 0))],
        out_specs=pl.BlockSpec((1, tq, D_NSA_Q), lambda i, j: (i, j, 0)),
        out_shape=jax.ShapeDtypeStruct((b, s, D_NSA_Q), BF16),
        scratch_shapes=[pltpu.VMEM((NSA_KV_HEADS, tq, nsb_pad), BF16),
                        pltpu.VMEM((NSA_KV_HEADS * rows, HEAD_DIM), F32),
                        pltpu.VMEM((NSA_KV_HEADS * rows, HEAD_DIM), F32)]
        + _attn_scratch(NSA_KV_HEADS * rows, s // chunk, chunk),
        compiler_params=pltpu.CompilerParams(dimension_semantics=("arbitrary", "arbitrary"),
                                             vmem_limit_bytes=VMEM_LIMIT),
        name="nsa_prompt",
    )(nqnh, nqh, gates, ckc, cvc, skh, svh, wkh, wvh, mselt, expand)


def _page_copy(cache_ref, buf_ref, sem_ref, pt_ref, b, slot, p):
    return pltpu.make_async_copy(cache_ref.at[pt_ref[b, p]], buf_ref.at[slot, p], sem_ref.at[slot])


def _pages_start(caches, bufs, sems, pt_ref, b, slot, n_pages):
    def body(p, carry):
        for c, bf, sm in zip(caches, bufs, sems):
            _page_copy(c, bf, sm, pt_ref, b, slot, p).start()
        return carry
    lax.fori_loop(0, n_pages, body, 0)


def _pages_wait(caches, bufs, sems, slot, n_pages):
    for c, bf, sm in zip(caches, bufs, sems):
        pltpu.make_async_copy(c.at[pl.ds(0, n_pages)], bf.at[slot], sm.at[slot]).wait()


def _paged_step(caches, bufs, sems, pt_ref, n_pages):
    b = pl.program_id(0)
    nb = pl.num_programs(0)
    slot = b % 2

    @pl.when(b == 0)
    def _():
        _pages_start(caches, bufs, sems, pt_ref, b, slot, n_pages)

    @pl.when(b + 1 < nb)
    def _():
        _pages_start(caches, bufs, sems, pt_ref, b + 1, 1 - slot, n_pages)

    _pages_wait(caches, bufs, sems, slot, n_pages)
    return slot


def _decode_scores(q_t, kbuf, slot, sc_ref, n_kv, group):
    n_pages, _, _, page = kbuf.shape[1:]
    for k in range(n_kv):
        qb = [jnp.broadcast_to(q_t[:, k * group + g:k * group + g + 1], (HEAD_DIM, page)) for g in range(group)]

        def body(p, carry):
            kt = kbuf[slot, p, k]
            for g in range(group):
                h = k * group + g
                sc_ref[p, h:h + 1, :] = jnp.sum(kt * qb[g], axis=0, keepdims=True)
            return carry

        lax.fori_loop(0, n_pages, body, 0, unroll=2)


def _decode_softmax(sc_ref, page_masks, s_new):
    s = jnp.where(jnp.stack(page_masks, axis=0) > 0.5, sc_ref[...], NEG_INF)
    m = jnp.maximum(jnp.max(jnp.max(s, axis=0), axis=1, keepdims=True), s_new)
    e = jnp.exp(s - m[None])
    e_new = jnp.exp(s_new - m)
    inv = 1.0 / jnp.maximum(jnp.sum(jnp.sum(e, axis=0), axis=1, keepdims=True) + e_new, 1e-30)
    sc_ref[...] = e * inv[None]
    return e_new * inv


def _decode_values(vbuf, slot, p_ref, entries, p_new, v_new_t, group):
    cols = []
    for h, ent in enumerate(entries):
        k = h // group
        acc = None
        for pg, idx in ent:
            term = vbuf[slot, pg, k] * p_ref[idx, h:h + 1, :]
            acc = term if acc is None else acc + term
        cols.append(jnp.sum(acc, axis=1, keepdims=True) + p_new[h:h + 1, 0:1] * v_new_t[:, k:k + 1])
    return cols


def _ranked_index(rank, ok, r):
    lane = lax.broadcasted_iota(jnp.int32, rank.shape, 1).astype(F32)
    hit = (rank == float(r)) & ok
    return jnp.sum(jnp.where(hit, lane, 0.0), axis=1, keepdims=True).astype(jnp.int32)


def _place_cols(cols):
    lane = lax.broadcasted_iota(jnp.int32, (HEAD_DIM, len(cols)), 1)
    out = jnp.zeros((HEAD_DIM, len(cols)), F32)
    for h, c in enumerate(cols):
        out = jnp.where(lane == h, c, out)
    return out


def _moba_sample_kernel(pt_ref, qt_ref, q_ref, knew_ref, vnewt_ref, k_hbm, v_hbm, o_ref,
                        kbuf, vbuf, sc_ref, sem_k, sem_v, *, n_blocks):
    n_pages, _, _, page = kbuf.shape[1:]
    slot = _paged_step((k_hbm, v_hbm), (kbuf, vbuf), (sem_k, sem_v), pt_ref, n_pages)
    ppb = MOBA_BLOCK // page
    _decode_scores(qt_ref[0] * SCALE, kbuf, slot, sc_ref, MOBA_KV_HEADS, MOBA_GROUP)
    s_new = jnp.sum(q_ref[0] * knew_ref[0], axis=1, keepdims=True) * SCALE
    lane = lax.broadcasted_iota(jnp.int32, (MOBA_HEADS, LANE), 1)
    sb = jnp.full((MOBA_HEADS, LANE), NEG_INF, F32)
    for j in range(n_blocks):
        tot = sc_ref[j * ppb]
        for r in range(1, ppb):
            tot = tot + sc_ref[j * ppb + r]
        sb = jnp.where(lane == j, jnp.sum(tot, axis=1, keepdims=True) * (1.0 / MOBA_BLOCK), sb)
    rank = _rank(sb, n_blocks, 1)
    in_range = lane < n_blocks
    sel = jnp.where((rank < MOBA_TOPK) & in_range, 1.0, 0.0)
    blk_masks = [jnp.broadcast_to(sel[:, j:j + 1], (MOBA_HEADS, page)) for j in range(n_blocks)]
    p_new = _decode_softmax(sc_ref, [blk_masks[p // ppb] for p in range(n_pages)], s_new)
    tops = [_ranked_index(rank, in_range, r) for r in range(MOBA_TOPK)]
    entries = []
    for h in range(MOBA_HEADS):
        pages = [tops[r][h, 0] * ppb + t for r in range(MOBA_TOPK) for t in range(ppb)]
        entries.append([(pg, pg) for pg in pages])
    cols = _decode_values(vbuf, slot, sc_ref, entries, p_new, vnewt_ref[0], MOBA_GROUP)
    o_ref[0] = _place_cols(cols)


def _native(cache):
    return jnp.transpose(cache, (0, 2, 3, 1))


def _moba_sample(page_table, q, k_new, v_new, cache_k, cache_v):
    db, n_pages = page_table.shape
    page = cache_k.shape[1]
    n_past = n_pages * page
    assert n_past % MOBA_BLOCK == 0 and MOBA_BLOCK % page == 0 and page == LANE
    n_blocks = n_past // MOBA_BLOCK
    assert MOBA_TOPK <= n_blocks <= LANE
    q3 = q.reshape(db, MOBA_HEADS, HEAD_DIM)
    q_t = jnp.transpose(q3, (0, 2, 1))
    k_rows = jnp.repeat(k_new.reshape(db, MOBA_KV_HEADS, HEAD_DIM), MOBA_GROUP, axis=1)
    v_t = jnp.transpose(v_new.reshape(db, MOBA_KV_HEADS, HEAD_DIM), (0, 2, 1))
    per_b = lambda a: pl.BlockSpec((1,) + a.shape[1:], lambda b, pt: (b, 0, 0))
    any_spec = pl.BlockSpec(memory_space=pl.ANY)
    buf = pltpu.VMEM((2, n_pages, MOBA_KV_HEADS, HEAD_DIM, page), F32)
    grid_spec = pltpu.PrefetchScalarGridSpec(
        num_scalar_prefetch=1,
        grid=(db,),
        in_specs=[per_b(q_t), per_b(q3), per_b(k_rows), per_b(v_t), any_spec, any_spec],
        out_specs=pl.BlockSpec((1, HEAD_DIM, MOBA_HEADS), lambda b, pt: (b, 0, 0)),
        scratch_shapes=[buf, buf, pltpu.VMEM((n_pages, MOBA_HEADS, page), F32),
                        pltpu.SemaphoreType.DMA((2,)), pltpu.SemaphoreType.DMA((2,))])
    o_t = pl.pallas_call(
        functools.partial(_moba_sample_kernel, n_blocks=n_blocks),
        grid_spec=grid_spec,
        out_shape=jax.ShapeDtypeStruct((db, HEAD_DIM, MOBA_HEADS), F32),
        compiler_params=pltpu.CompilerParams(dimension_semantics=("arbitrary",), vmem_limit_bytes=VMEM_LIMIT),
        name="moba_sample",
    )(page_table, q_t, q3, k_rows, v_t, _native(cache_k), _native(cache_v))
    return jnp.transpose(o_t, (0, 2, 1)).reshape(db, D_MOBA_Q)


def _row_sort_matrix(page):
    cpp = page // CMP_STRIDE
    out = np.arange(page)
    src = (out % cpp) * CMP_STRIDE + out // cpp
    return (src[:, None] == np.arange(page)[None, :]).astype(np.float32)


def _compress_pages(buf, slot, xs_ref, sort_ref, bias_ref, w1_ref, w2_ref):
    n_pages, _, _, page = buf.shape[1:]
    cpp = page // CMP_STRIDE
    n_chunks = n_pages * cpp
    sort = sort_ref[...]
    first = lax.broadcasted_iota(jnp.int32, (2 * cpp, D_NSA_KV), 1) < HEAD_DIM

    def to_rows(pp, carry):
        ra = _dot_nt(sort, buf[slot, 2 * pp].reshape(D_NSA_KV, page).astype(BF16))
        rb = _dot_nt(sort, buf[slot, 2 * pp + 1].reshape(D_NSA_KV, page).astype(BF16))
        start = pl.multiple_of(pp * 2 * cpp, 2 * cpp)
        for l2 in range(CMP_STRIDE // 2):
            lo, hi = 2 * l2 * cpp, (2 * l2 + 1) * cpp
            even = jnp.concatenate([ra[lo:lo + cpp], rb[lo:lo + cpp]], axis=0)
            odd = jnp.concatenate([ra[hi:hi + cpp], rb[hi:hi + cpp]], axis=0)
            xs_ref[0, l2, pl.ds(start, 2 * cpp), :] = jnp.where(first, even, pltpu.roll(odd, HEAD_DIM, 1)).astype(BF16)
            xs_ref[1, l2, pl.ds(start, 2 * cpp), :] = jnp.where(first, pltpu.roll(even, HEAD_DIM, 1), odd).astype(BF16)
        return carry

    lax.fori_loop(0, n_pages // 2, to_rows, 0, unroll=16)
    x = jnp.concatenate([jnp.concatenate([xs_ref[k, l2] for l2 in range(CMP_STRIDE // 2)], axis=1)
                         for k in range(NSA_KV_HEADS)], axis=0)
    hid = w1_ref.shape[1] // 2
    bias = bias_ref[:, 0:hid] + bias_ref[:, 2 * hid:3 * hid]
    p = _dot(x, w1_ref[...])
    hs = []
    for k in range(NSA_KV_HEADS):
        pk = p[k * n_chunks:(k + 1) * n_chunks]
        hs.append(_silu(pk[:, :hid] + pltpu.roll(pk[:, hid:], n_chunks - 1, 0) + bias).astype(BF16))
    return _dot(jnp.concatenate(hs, axis=1), w2_ref[...])


def _cmp_bias_kernel(kpe_ref, kw_ref, vpe_ref, vw_ref, kb_ref, vb_ref):
    for pe_ref, w_ref, b_ref in ((kpe_ref, kw_ref, kb_ref), (vpe_ref, vw_ref, vb_ref)):
        halves = [jnp.dot(pe_ref[r], w_ref[r], preferred_element_type=F32, precision=HIGHEST) for r in range(2)]
        b_ref[...] = jnp.concatenate(halves, axis=1)


def _cmp_bias(kpe, kw1, vpe, vw1):
    full = lambda a: pl.BlockSpec(a.shape, lambda i: (0,) * a.ndim)
    width = 2 * kw1.shape[2]
    out = pl.BlockSpec((1, width), lambda i: (0, 0))
    return pl.pallas_call(
        _cmp_bias_kernel,
        grid=(1,),
        in_specs=[full(kpe), full(kw1), full(vpe), full(vw1)],
        out_specs=[out, out],
        out_shape=[jax.ShapeDtypeStruct((1, width), F32)] * 2,
        compiler_params=pltpu.CompilerParams(dimension_semantics=("arbitrary",), vmem_limit_bytes=VMEM_LIMIT),
        name="cmp_bias",
    )(kpe, kw1, vpe, vw1)


def _nsa_sample_kernel(pt_ref, qn_ref, qrt_ref, qr_ref, gate_ref, sknew_ref, svnewt_ref, wknew_ref, wknewt_ref, wvnewt_ref,
                       wk_ref, wv_ref, sort_ref, kb_ref, kw1_ref, kw2_ref, vb_ref, vw1_ref, vw2_ref, msel_ref,
                       ck_hbm, cv_hbm, sk_hbm, sv_hbm, o_ref, wko_ref, wvo_ref,
                       ckbuf, cvbuf, skbuf, svbuf, xs_ref, sc_ref, sem_ck, sem_cv, sem_sk, sem_sv, *, n_cmp, n_sel):
    n_pages, _, _, page = skbuf.shape[1:]
    slot = _paged_step((ck_hbm, cv_hbm, sk_hbm, sv_hbm), (ckbuf, cvbuf, skbuf, svbuf),
                       (sem_ck, sem_cv, sem_sk, sem_sv), pt_ref, n_pages)
    n_chunks = n_pages * page // CMP_STRIDE
    qr_t = qrt_ref[0] * SCALE
    gates = gate_ref[0]
    ckc = _compress_pages(ckbuf, slot, xs_ref, sort_ref, kb_ref, kw1_ref, kw2_ref)
    cvc = _compress_pages(cvbuf, slot, xs_ref, sort_ref, vb_ref, vw1_ref, vw2_ref)
    n_idx = lax.broadcasted_iota(jnp.int32, (NSA_HEADS, n_chunks), 1)
    cmask = n_idx < n_cmp
    s = jnp.where(cmask, _dot_nt((qn_ref[0] * SCALE).astype(BF16), ckc.astype(BF16)), NEG_INF)
    m = jnp.max(s, axis=1, keepdims=True)
    e = jnp.where(cmask, jnp.exp(s - m), 0.0)
    p = e / jnp.maximum(jnp.sum(e, axis=1, keepdims=True), 1e-30)
    o_c = _dot(p.astype(BF16), cvc.astype(BF16))
    eye = jnp.where(lax.broadcasted_iota(jnp.int32, (D_NSA_KV, D_NSA_KV), 0)
                    == lax.broadcasted_iota(jnp.int32, (D_NSA_KV, D_NSA_KV), 1), 1.0, 0.0)
    o_c_t = _dot_nt(eye, o_c, precision=HIGHEST)
    p_kv = jnp.concatenate([jnp.sum(p[k * NSA_GROUP:(k + 1) * NSA_GROUP], axis=0, keepdims=True)
                            for k in range(NSA_KV_HEADS)], axis=0)
    nsb_pad = msel_ref.shape[1]
    score = jnp.dot(p_kv, msel_ref[...], preferred_element_type=F32, precision=HIGHEST)
    jb = lax.broadcasted_iota(jnp.int32, (NSA_KV_HEADS, nsb_pad), 1)
    cur = n_sel - 1
    valid = jb < n_sel
    forced = (jb == 0) | (jb == cur) | (jb == cur - 1)
    score = jnp.where(valid, jnp.where(forced, jnp.inf, score), NEG_INF)
    rank = _rank(score, n_sel, 1)
    bpp = page // SEL_LEN
    lane = lax.broadcasted_iota(jnp.int32, (1, page), 1)
    entries = [[] for _ in range(NSA_HEADS)]
    chosen = [_ranked_index(rank, valid, r) for r in range(SEL_TOPN)]
    for k in range(NSA_KV_HEADS):
        qb = [jnp.broadcast_to(qr_t[:, k * NSA_GROUP + g:k * NSA_GROUP + g + 1], (HEAD_DIM, page))
              for g in range(NSA_GROUP)]
        for r in range(SEL_TOPN):
            blk = chosen[r][k, 0]
            pg = jnp.minimum(blk // bpp, n_pages - 1)
            keep = lane // SEL_LEN == jnp.where(blk < cur, blk % bpp, -1)
            kt = skbuf[slot, pg, k]
            for g in range(NSA_GROUP):
                h = k * NSA_GROUP + g
                sc_ref[r, h:h + 1, :] = jnp.where(keep, jnp.sum(kt * qb[g], axis=0, keepdims=True), NEG_INF)
                entries[h].append((pg, r))
    qr = qr_ref[0] * SCALE
    s_new = jnp.sum(qr * sknew_ref[0], axis=1, keepdims=True)
    s = sc_ref[...]
    m = jnp.maximum(jnp.max(jnp.max(s, axis=0), axis=1, keepdims=True), s_new)
    e = jnp.exp(s - m[None])
    e_new = jnp.exp(s_new - m)
    inv = 1.0 / jnp.maximum(jnp.sum(jnp.sum(e, axis=0), axis=1, keepdims=True) + e_new, 1e-30)
    sc_ref[...] = e * inv[None]
    cols_s = _decode_values(svbuf, slot, sc_ref, entries, e_new * inv, svnewt_ref[0], NSA_GROUP)
    w_buf = wk_ref.shape[3]
    widx = lax.broadcasted_iota(jnp.int32, (NSA_HEADS, w_buf), 1)
    hrow = lax.broadcasted_iota(jnp.int32, (NSA_HEADS, w_buf), 0)
    sw = jnp.zeros((NSA_HEADS, w_buf), F32)
    for h in range(NSA_HEADS):
        kt = wk_ref[0, h // NSA_GROUP]
        sw = jnp.where(hrow == h, jnp.sum(kt * qr_t[:, h:h + 1], axis=0, keepdims=True), sw)
    sw = jnp.where(widx > w_buf - WINDOW, sw, NEG_INF)
    sw_new = jnp.sum(qr * wknew_ref[0], axis=1, keepdims=True)
    mw = jnp.maximum(jnp.max(sw, axis=1, keepdims=True), sw_new)
    ew = jnp.exp(sw - mw)
    ew_new = jnp.exp(sw_new - mw)
    invw = 1.0 / jnp.maximum(jnp.sum(ew, axis=1, keepdims=True) + ew_new, 1e-30)
    pw = ew * invw
    pw_new = ew_new * invw
    wvnew_t = wvnewt_ref[0]
    cols = []
    for h in range(NSA_HEADS):
        k = h // NSA_GROUP
        o_w = jnp.sum(wv_ref[0, k] * pw[h:h + 1, :], axis=1, keepdims=True) + pw_new[h:h + 1, 0:1] * wvnew_t[:, k:k + 1]
        o_cmp = o_c_t[k * HEAD_DIM:(k + 1) * HEAD_DIM, h:h + 1]
        cols.append(gates[0:1, h:h + 1] * o_cmp + gates[1:2, h:h + 1] * cols_s[h] + gates[2:3, h:h + 1] * o_w)
    o_ref[0] = _place_cols(cols)
    last = lax.broadcasted_iota(jnp.int32, (HEAD_DIM, w_buf), 1) == w_buf - 1
    wknew_t = wknewt_ref[0]
    for k in range(NSA_KV_HEADS):
        wko_ref[0, k] = jnp.where(last, wknew_t[:, k:k + 1], pltpu.roll(wk_ref[0, k], w_buf - 1, 1))
        wvo_ref[0, k] = jnp.where(last, wvnew_t[:, k:k + 1], pltpu.roll(wv_ref[0, k], w_buf - 1, 1))


def _expand_heads(q, n_heads, n_kv):
    b = q.shape[0]
    group = n_heads // n_kv
    place = jnp.asarray((np.arange(n_heads)[:, None] // group == np.arange(n_kv)[None, :]).astype(np.float32))
    q4 = q.reshape(b, n_heads, 1, HEAD_DIM) * place[None, :, :, None]
    return q4.reshape(b, n_heads, n_kv * HEAD_DIM)


def _nsa_sample(page_table, qn, qr, gates, sk_new, sv_new, wk_new, wv_new, state_wk, state_wv, kw, vw,
                cache_ck, cache_cv, cache_sk, cache_sv):
    db, n_pages = page_table.shape
    page = cache_ck.shape[1]
    n_past = n_pages * page
    assert page == LANE and page % SEL_LEN == 0 and n_pages % 2 == 0
    n_chunks = n_past // CMP_STRIDE
    n_cmp = n_chunks - CMP_LEN // CMP_STRIDE + 1
    n_sel = -(-(n_past + 1) // SEL_LEN)
    assert n_sel >= SEL_TOPN
    nsb_pad = -(-n_sel // LANE) * LANE
    w_buf = state_wk.shape[1]
    heads = lambda a, n: a.reshape(db, n, HEAD_DIM)
    t = lambda a: jnp.transpose(a, (0, 2, 1))
    rep = lambda a: jnp.repeat(heads(a, NSA_KV_HEADS), NSA_GROUP, axis=1)
    qr3 = heads(qr, NSA_HEADS)
    msel = jnp.asarray(_cmp_to_sel(n_chunks, n_cmp, nsb_pad, n_sel))
    small = [_expand_heads(qn, NSA_HEADS, NSA_KV_HEADS), t(qr3), qr3, t(gates.reshape(db, NSA_HEADS, 3)),
             rep(sk_new), t(heads(sv_new, NSA_KV_HEADS)), rep(wk_new), t(heads(wk_new, NSA_KV_HEADS)),
             t(heads(wv_new, NSA_KV_HEADS))]
    kb, vb = _cmp_bias(kw[0], kw[1], vw[0], vw[1])
    def cat(w1):
        one = w1[:, :, :CMP_HIDDEN].reshape(2, CMP_STRIDE, D_NSA_KV, CMP_HIDDEN)[:, :, :HEAD_DIM, :]
        return jnp.transpose(one, (1, 2, 0, 3)).reshape(CMP_STRIDE * HEAD_DIM, 2 * CMP_HIDDEN).astype(BF16)
    consts = [jnp.asarray(_row_sort_matrix(page)).astype(BF16), kb, cat(kw[1]), kw[2], vb, cat(vw[1]), vw[2], msel]
    wk_t, wv_t = _native(state_wk), _native(state_wv)
    per_b = lambda a: pl.BlockSpec((1,) + a.shape[1:], lambda b, pt: (b,) + (0,) * (a.ndim - 1))
    const = lambda a: pl.BlockSpec(a.shape, lambda b, pt: (0,) * a.ndim)
    any_spec = pl.BlockSpec(memory_space=pl.ANY)
    buf = pltpu.VMEM((2, n_pages, NSA_KV_HEADS, HEAD_DIM, page), F32)
    wspec = pl.BlockSpec((1, NSA_KV_HEADS, HEAD_DIM, w_buf), lambda b, pt: (b, 0, 0, 0))
    grid_spec = pltpu.PrefetchScalarGridSpec(
        num_scalar_prefetch=1,
        grid=(db,),
        in_specs=[per_b(a) for a in small] + [wspec, wspec] + [const(a) for a in consts] + [any_spec] * 4,
        out_specs=[pl.BlockSpec((1, HEAD_DIM, NSA_HEADS), lambda b, pt: (b, 0, 0)), wspec, wspec],
        scratch_shapes=[buf, buf, buf, buf,
                        pltpu.VMEM((NSA_KV_HEADS, CMP_STRIDE // 2, n_chunks, D_NSA_KV), BF16),
                        pltpu.VMEM((SEL_TOPN, NSA_HEADS, page), F32)] + [pltpu.SemaphoreType.DMA((2,))] * 4)
    o_t, wko, wvo = pl.pallas_call(
        functools.partial(_nsa_sample_kernel, n_cmp=n_cmp, n_sel=n_sel),
        grid_spec=grid_spec,
        out_shape=[jax.ShapeDtypeStruct((db, HEAD_DIM, NSA_HEADS), F32),
                   jax.ShapeDtypeStruct(wk_t.shape, F32), jax.ShapeDtypeStruct(wv_t.shape, F32)],
        compiler_params=pltpu.CompilerParams(dimension_semantics=("arbitrary",), vmem_limit_bytes=VMEM_LIMIT),
        name="nsa_sample",
    )(page_table, *small, wk_t, wv_t, *consts,
      _native(cache_ck), _native(cache_cv), _native(cache_sk), _native(cache_sv))
    back = lambda a: jnp.transpose(a, (0, 3, 1, 2))
    return jnp.transpose(o_t, (0, 2, 1)).reshape(db, D_NSA_Q), back(wko), back(wvo)


def _ffn_weights(w_in, w_out):
    return w_in.astype(BF16), w_out.astype(BF16)


def kernel(x_prompt, x_sample, cache_moba_k, cache_moba_v, cache_nsa_cmp_k, cache_nsa_cmp_v, cache_nsa_sel_k, cache_nsa_sel_v, state_nsa_win_k, state_nsa_win_v, page_table, c_prompt, c_sample, w_ada, b_ada, norm_ffn1, w_ffn1_in, w_ffn1_out, norm_mix, w_mix_in, w_mix_out, norm_ffn2, w_ffn2_in, w_ffn2_out, cmp_k_pe, cmp_k_w1, cmp_k_w2, cmp_v_pe, cmp_v_w1, cmp_v_w2, norm_final):
    depth = w_ada.shape[0]
    assert depth == 1, "single-layer step"
    b, s, d = x_prompt.shape
    db = x_sample.shape[0]
    assert x_sample.shape[1] == 1
    n_pages = page_table.shape[1]
    page = cache_moba_k.shape[2]
    n_past = n_pages * page
    assert s % Q_TILE == 0 and db % 8 == 0
    l = 0

    ffn1_w = _ffn_weights(w_ffn1_in[l], w_ffn1_out[l])
    ffn2_w = _ffn_weights(w_ffn2_in[l], w_ffn2_out[l])
    w_proj = jnp.pad(w_mix_in[l], ((0, 0), (0, PROJ_PAD - w_mix_in.shape[2]))).astype(BF16)
    w_mo = w_mix_out[l].astype(BF16)
    w_mo_m, w_mo_n = w_mo[:D_MOBA_Q], w_mo[D_MOBA_Q:]
    kw = _compress_weights(cmp_k_pe[l], cmp_k_w1[l], cmp_k_w2[l])
    vw = _compress_weights(cmp_v_pe[l], cmp_v_w1[l], cmp_v_w2[l])

    mods = _ada_mods(jnp.concatenate([c_sample, c_prompt], axis=0), w_ada[l], b_ada[l])
    mods_p = _Mods(mods, db, per_row=False)
    mods_s = _Mods(mods, db, per_row=True)
    xs = x_sample.reshape(1, db, d)
    tm_p = 512 if s % 512 == 0 else Q_TILE

    xp1 = _ffn(x_prompt, mods_p, (0, 1, 2), norm_ffn1[l], *ffn1_w, tm=tm_p)
    xs1 = _ffn(xs, mods_s, (0, 1, 2), norm_ffn1[l], *ffn1_w, tm=db)

    tabs_p = _rope_tables(jnp.arange(s, dtype=jnp.int32))
    pp = _proj(xp1, mods_p, (3, 4), norm_mix[l], w_proj, tabs_p, tm=tm_p, head_major=True)
    pr, ph = pp["rows"], pp["heads"]
    om = _moba_prompt(pr["mq"], ph["mq"], pr["mk"], ph["mk"], ph["mv"])
    bf = lambda w: (w[0], w[1].astype(BF16), w[2])
    ckc, cvc = _compress_prompt(pr["ck"], pr["cv"], bf(kw), bf(vw))
    on = _nsa_prompt(ph["nqn"], ph["nq"], pr["gate"], ckc, cvc, ph["sk"], ph["sv"], ph["wk"], ph["wv"])

    tabs_s = _rope_tables(jnp.full((db,), n_past, dtype=jnp.int32))
    ps = _proj(xs1, mods_s, (3, 4), norm_mix[l], w_proj, tabs_s, tm=db, head_major=False)
    sr = {n: a[0] for n, a in ps["rows"].items()}
    o_m_s = _moba_sample(page_table, sr["mq"], sr["mk"], sr["mv"], cache_moba_k[l], cache_moba_v[l])
    o_n_s, win_k_s, win_v_s = _nsa_sample(
        page_table, sr["nqn"], sr["nq"], sr["gate"][:, :N_GATES], sr["sk"], sr["sv"], sr["wk"], sr["wv"],
        state_nsa_win_k[l], state_nsa_win_v[l], kw, vw,
        cache_nsa_cmp_k[l], cache_nsa_cmp_v[l], cache_nsa_sel_k[l], cache_nsa_sel_v[l])
    om_s = o_m_s.reshape(1, db, D_MOBA_Q).astype(BF16)
    on_s = o_n_s.reshape(1, db, D_NSA_Q).astype(BF16)

    yp = _ffn(xp1, mods_p, (6, 7, 8), norm_ffn2[l], *ffn2_w, tm=tm_p,
              mix=(om, on, w_mo_m, w_mo_n, 5), final_g=norm_final)
    ys = _ffn(xs1, mods_s, (6, 7, 8), norm_ffn2[l], *ffn2_w, tm=db,
              mix=(om_s, on_s, w_mo_m, w_mo_n, 5), final_g=norm_final)

    w_keep = min(WINDOW, s)
    back = lambda a: jnp.transpose(a, (0, 3, 1, 2))[None]
    st = lambda n: back(pp["state"][n])
    ss_ = lambda n: jnp.transpose(ps["state"][n], (0, 3, 1, 2)).reshape(1, db, 1, -1, HEAD_DIM)
    win = lambda n: back(pp["state"][n][:, :, :, s - w_keep:])
    return (yp, ys.reshape(db, 1, d),
            st("mk"), ss_("mk"), st("mv"), ss_("mv"), st("ck"), ss_("ck"), st("cv"), ss_("cv"),
            st("sk"), ss_("sk"), st("sv"), ss_("sv"),
            win("wk"), win_k_s[None], win("wv"), win_v_s[None])
```
